```python
import math
import jax
import jax.numpy as jnp
from jax import lax
import numpy as np

D_MODEL = 1024
BATCH = 2
SEQ = 16384
DEPTH = 2

RMS_EPS = 1e-6
CHUNK = 64

GLA_HEADS = 4
GLA_DK = D_MODEL // 2
GLA_DV = D_MODEL
GLA_HK = GLA_DK // GLA_HEADS
GLA_HV = GLA_DV // GLA_HEADS
GLA_GATE_RANK = 16
GLA_TAU = 16.0

GDN_QK_HEADS = 8
GDN_V_HEADS = 16
GDN_HEAD_DIM = 128
GDN_DK = GDN_QK_HEADS * GDN_HEAD_DIM
GDN_DV = GDN_V_HEADS * GDN_HEAD_DIM
GDN_CONV = 4

PEER_HEADS = 8
PEER_TOPK = 16
PEER_NKEYS = 128
PEER_NEXPERTS = PEER_NKEYS * PEER_NKEYS
PEER_QDIM = 256
PEER_HALF = PEER_QDIM // 2
PEER_TOKEN_BLOCK = 128

N_GLA = (DEPTH + 1) // 2
N_GDN = DEPTH // 2

kernel_name = 'hybrid_gla_gdn_peer_adaln'


def rmsnorm(x, g):
    x32 = x.astype(jnp.float32)
    y = x32 * lax.rsqrt(jnp.mean(x32 * x32, axis=-1, keepdims=True) + RMS_EPS)
    return (y * g.astype(jnp.float32)).astype(x.dtype)


def rms_head(o, g):
    return o * lax.rsqrt(jnp.mean(o * o, axis=-1, keepdims=True) + RMS_EPS) * g.astype(jnp.float32)


def l2norm(x):
    return x * lax.rsqrt(jnp.sum(x * x, axis=-1, keepdims=True) + 1e-6)


def to_heads(t, n_heads):
    b, s, w = t.shape
    return t.reshape(b, s, n_heads, w // n_heads).transpose(0, 2, 1, 3).astype(jnp.float32)


def causal_short_conv(x, w):
    k_w = w.shape[0]
    s = x.shape[1]
    xp = jnp.pad(x, ((0, 0), (k_w - 1, 0), (0, 0)))
    out = xp[:, 0:s] * w[0]
    for j in range(1, k_w):
        out = out + xp[:, j:j + s] * w[j]
    return out


def gla_chunked(q, k, v, log_a):
    b_, h_, s_, dk = q.shape
    dv = v.shape[-1]
    n = s_ // CHUNK
    q = q.reshape(b_, h_, n, CHUNK, dk)
    k = k.reshape(b_, h_, n, CHUNK, dk)
    v = v.reshape(b_, h_, n, CHUNK, dv)
    la = log_a.reshape(b_, h_, n, CHUNK, dk)
    cum = jnp.cumsum(la, axis=3)
    cum_last = cum[:, :, :, -1:, :]
    q_in = q * jnp.exp(cum)
    k_in = k * jnp.exp(-cum)
    k_end = k * jnp.exp(cum_last - cum)
    causal = jnp.tril(jnp.ones((CHUNK, CHUNK), dtype=bool))
    attn = jnp.where(causal, jnp.einsum('bhncd,bhnsd->bhncs', q_in, k_in), 0.0)
    o_intra = jnp.einsum('bhncs,bhnsv->bhncv', attn, v)
    dec = jnp.exp(cum_last[:, :, :, 0, :])

    def step(state, inp):
        qc, kc, vc, dc = inp
        o = jnp.einsum('bhcd,bhdv->bhcv', qc, state)
        state = dc[..., None] * state + jnp.einsum('bhcd,bhcv->bhdv', kc, vc)
        return state, o

    s0 = jnp.zeros((b_, h_, dk, dv), jnp.float32)
    xs = (jnp.moveaxis(q_in, 2, 0), jnp.moveaxis(k_end, 2, 0), jnp.moveaxis(v, 2, 0), jnp.moveaxis(dec, 2, 0))
    _, o_inter = lax.scan(step, s0, xs)
    o = o_intra + jnp.moveaxis(o_inter, 0, 2)
    return o.reshape(b_, h_, s_, dv)


def gated_delta_chunked(q, k, v, g, beta):
    b_, h_, s_, dk = q.shape
    dv = v.shape[-1]
    n = s_ // CHUNK
    q = q.reshape(b_, h_, n, CHUNK, dk)
    k = k.reshape(b_, h_, n, CHUNK, dk)
    v = v.reshape(b_, h_, n, CHUNK, dv)
    g = g.reshape(b_, h_, n, CHUNK)
    beta = beta.reshape(b_, h_, n, CHUNK)
    gc = jnp.cumsum(g, axis=-1)
    lower = jnp.tril(jnp.ones((CHUNK, CHUNK), dtype=bool))
    strict = jnp.tril(jnp.ones((CHUNK, CHUNK), dtype=bool), -1)
    diff = gc[..., :, None] - gc[..., None, :]
    decay = jnp.where(lower, jnp.exp(jnp.where(lower, diff, 0.0)), 0.0)
    kk = jnp.einsum('bhncd,bhnsd->bhncs', k, k)
    a_mat = jnp.where(strict, beta[..., :, None] * kk * decay, 0.0) + jnp.eye(CHUNK, dtype=jnp.float32)
    u = lax.linalg.triangular_solve(a_mat, beta[..., None] * v, left_side=True, lower=True, unit_diagonal=True)
    w = lax.linalg.triangular_solve(a_mat, (beta * jnp.exp(gc))[..., None] * k, left_side=True, lower=True, unit_diagonal=True)
    qk = jnp.einsum('bhncd,bhnsd->bhncs', q, k) * decay
    q_dec = q * jnp.exp(gc)[..., None]
    k_end = k * jnp.exp(gc[..., -1:] - gc)[..., None]
    dec = jnp.exp(gc[..., -1])

    def step(state, inp):
        qd, kd, qkc, uc, wc, dc = inp
        v_new = uc - jnp.einsum('bhcd,bhdv->bhcv', wc, state)
        o = jnp.einsum('bhcd,bhdv->bhcv', qd, state) + jnp.einsum('bhcs,bhsv->bhcv', qkc, v_new)
        state = dc[..., None, None] * state + jnp.einsum('bhcd,bhcv->bhdv', kd, v_new)
        return state, o

    s0 = jnp.zeros((b_, h_, dk, dv), jnp.float32)
    xs = tuple(jnp.moveaxis(t, 2, 0) for t in (q_dec, k_end, qk, u, w, dec))
    _, o = lax.scan(step, s0, xs)
    return jnp.moveaxis(o, 0, 2).reshape(b_, h_, s_, dv)


def gla_mixer(h, w_in, w_g2, b_g2, norm_g, w_out):
    b_, s_, _ = h.shape
    p = h @ w_in
    q, k, v, r, g1 = jnp.split(p, [GLA_DK, 2 * GLA_DK, 2 * GLA_DK + GLA_DV, 2 * GLA_DK + 2 * GLA_DV], axis=-1)
    log_a = jax.nn.log_sigmoid((g1 @ w_g2 + b_g2).astype(jnp.float32)) / GLA_TAU
    o = gla_chunked(to_heads(q, GLA_HEADS) * (GLA_HK ** -0.5), to_heads(k, GLA_HEADS),
                    to_heads(v, GLA_HEADS), to_heads(log_a, GLA_HEADS))
    o = o.transpose(0, 2, 1, 3)
    o = rms_head(o, norm_g) * jax.nn.silu(r.reshape(b_, s_, GLA_HEADS, GLA_HV).astype(jnp.float32))
    return o.reshape(b_, s_, GLA_DV).astype(h.dtype) @ w_out


def gdn_mixer(h, w_in, conv_w, a_log, dt_bias, norm_g, w_out):
    b_, s_, _ = h.shape
    p = h @ w_in
    qkv, z, a, bb = jnp.split(p, [2 * GDN_DK + GDN_DV, 2 * GDN_DK + 2 * GDN_DV,
                                  2 * GDN_DK + 2 * GDN_DV + GDN_V_HEADS], axis=-1)
    qkv = jax.nn.silu(causal_short_conv(qkv.astype(jnp.float32), conv_w.astype(jnp.float32)))
    q, k, v = jnp.split(qkv, [GDN_DK, 2 * GDN_DK], axis=-1)
    rep = GDN_V_HEADS // GDN_QK_HEADS
    q = jnp.repeat(l2norm(to_heads(q, GDN_QK_HEADS)) * (GDN_HEAD_DIM ** -0.5), rep, axis=1)
    k = jnp.repeat(l2norm(to_heads(k, GDN_QK_HEADS)), rep, axis=1)
    v = to_heads(v, GDN_V_HEADS)
    g = -jnp.exp(a_log.astype(jnp.float32)) * jax.nn.softplus(a.astype(jnp.float32) + dt_bias.astype(jnp.float32))
    beta = jax.nn.sigmoid(bb.astype(jnp.float32))
    o = gated_delta_chunked(q, k, v, g.transpose(0, 2, 1), beta.transpose(0, 2, 1))
    o = o.transpose(0, 2, 1, 3)
    o = rms_head(o, norm_g) * jax.nn.silu(z.reshape(b_, s_, GDN_V_HEADS, GDN_HEAD_DIM).astype(jnp.float32))
    return o.reshape(b_, s_, GDN_DV).astype(h.dtype) @ w_out


def peer_ffn(h, wq, k1, k2, u, v):
    b_, s_, d = h.shape
    t = h.reshape(b_ * s_, d)
    n_tok = t.shape[0]
    q = (t @ wq).astype(jnp.float32).reshape(n_tok, PEER_HEADS, 2, PEER_HALF)
    s1 = jnp.einsum('nhd,kd->nhk', q[:, :, 0], k1.astype(jnp.float32))
    s2 = jnp.einsum('nhd,kd->nhk', q[:, :, 1], k2.astype(jnp.float32))
    v1, i1 = lax.top_k(s1, PEER_TOPK)
    v2, i2 = lax.top_k(s2, PEER_TOPK)
    cand = (v1[..., :, None] + v2[..., None, :]).reshape(n_tok, PEER_HEADS, PEER_TOPK * PEER_TOPK)
    cand_idx = (i1[..., :, None] * PEER_NKEYS + i2[..., None, :]).reshape(n_tok, PEER_HEADS, PEER_TOPK * PEER_TOPK)
    sc, pos = lax.top_k(cand, PEER_TOPK)
    idx = jnp.take_along_axis(cand_idx, pos, axis=-1)
    gate = jax.nn.softmax(sc, axis=-1).astype(h.dtype)
    n_blk = n_tok // PEER_TOKEN_BLOCK
    hk = PEER_HEADS * PEER_TOPK

    def block(args):
        tb, ib, gb = args
        ue = jnp.take(u, ib, axis=0)
        act = gb * jax.nn.gelu(jnp.einsum('nd,nkd->nk', tb, ue))
        ve = jnp.take(v, ib, axis=0)
        return jnp.einsum('nk,nkd->nd', act, ve)

    y = lax.map(block, (t.reshape(n_blk, PEER_TOKEN_BLOCK, d),
                        idx.reshape(n_blk, PEER_TOKEN_BLOCK, hk),
                        gate.reshape(n_blk, PEER_TOKEN_BLOCK, hk)))
    return y.reshape(b_, s_, d)


def setup_inputs(seed: int = 0) -> dict:
    key = jax.random.key(seed)
    ks = jax.random.split(key, 24)
    d = D_MODEL

    def nrm(k, shape, scale):
        return jax.random.normal(k, shape, jnp.float32) * scale

    dt = jnp.exp(jax.random.uniform(ks[14], (N_GDN, GDN_V_HEADS), jnp.float32,
                                    minval=math.log(0.001), maxval=math.log(0.1)))
    return {
        'x': nrm(ks[0], (BATCH, SEQ, d), 1.0),
        'c': nrm(ks[1], (BATCH, d), 1.0),
        'ada_w': nrm(ks[2], (DEPTH, d, 6 * d), 0.5 * d ** -0.5),
        'ada_b': nrm(ks[3], (DEPTH, 6 * d), 0.02),
        'norm_mix_g': 1.0 + nrm(ks[4], (DEPTH, d), 0.02),
        'norm_ffn_g': 1.0 + nrm(ks[5], (DEPTH, d), 0.02),
        'gla_w_in': nrm(ks[6], (N_GLA, d, 2 * GLA_DK + 2 * GLA_DV + GLA_GATE_RANK), d ** -0.5),
        'gla_w_g2': nrm(ks[7], (N_GLA, GLA_GATE_RANK, GLA_DK), GLA_GATE_RANK ** -0.5),
        'gla_b_g2': nrm(ks[8], (N_GLA, GLA_DK), 0.1),
        'gla_norm_g': 1.0 + nrm(ks[9], (N_GLA, GLA_HV), 0.02),
        'gla_w_out': nrm(ks[10], (N_GLA, GLA_DV, d), GLA_DV ** -0.5),
        'gdn_w_in': nrm(ks[11], (N_GDN, d, 2 * GDN_DK + 2 * GDN_DV + 2 * GDN_V_HEADS), d ** -0.5),
        'gdn_conv_w': nrm(ks[12], (N_GDN, GDN_CONV, 2 * GDN_DK + GDN_DV), 0.5),
        'gdn_a_log': jnp.log(jax.random.uniform(ks[13], (N_GDN, GDN_V_HEADS), jnp.float32, minval=1.0, maxval=16.0)),
        'gdn_dt_bias': dt + jnp.log(-jnp.expm1(-dt)),
        'gdn_norm_g': 1.0 + nrm(ks[15], (N_GDN, GDN_HEAD_DIM), 0.02),
        'gdn_w_out': nrm(ks[16], (N_GDN, GDN_DV, d), GDN_DV ** -0.5),
        'peer_wq': nrm(ks[17], (DEPTH, d, PEER_HEADS * PEER_QDIM), d ** -0.5),
        'peer_k1': nrm(ks[18], (DEPTH, PEER_NKEYS, PEER_HALF), PEER_HALF ** -0.5),
        'peer_k2': nrm(ks[19], (DEPTH, PEER_NKEYS, PEER_HALF), PEER_HALF ** -0.5),
        'peer_u': nrm(ks[20], (DEPTH, PEER_NEXPERTS, d), d ** -0.5),
        'peer_v': nrm(ks[21], (DEPTH, PEER_NEXPERTS, d), 1.0),
        'final_norm_g': 1.0 + nrm(ks[22], (d,), 0.02),
    }


def reference(x, c, ada_w, ada_b, norm_mix_g, norm_ffn_g, gla_w_in, gla_w_g2, gla_b_g2, gla_norm_g,
              gla_w_out, gdn_w_in, gdn_conv_w, gdn_a_log, gdn_dt_bias, gdn_norm_g, gdn_w_out,
              peer_wq, peer_k1, peer_k2, peer_u, peer_v, final_norm_g):
    c_act = jax.nn.silu(c)
    for i in range(DEPTH):
        mod = (c_act @ ada_w[i] + ada_b[i])[:, None, :]
        sh1, sc1, gt1, sh2, sc2, gt2 = jnp.split(mod, 6, axis=-1)
        h = rmsnorm(x, norm_mix_g[i]) * (1.0 + sc1) + sh1
        j = i // 2
        if i % 2 == 0:
            m = gla_mixer(h, gla_w_in[j], gla_w_g2[j], gla_b_g2[j], gla_norm_g[j], gla_w_out[j])
        else:
            m = gdn_mixer(h, gdn_w_in[j], gdn_conv_w[j], gdn_a_log[j], gdn_dt_bias[j], gdn_norm_g[j], gdn_w_out[j])
        x = x + gt1 * m
        h = rmsnorm(x, norm_ffn_g[i]) * (1.0 + sc2) + sh2
        x = x + gt2 * peer_ffn(h, peer_wq[i], peer_k1[i], peer_k2[i], peer_u[i], peer_v[i])
    return rmsnorm(x, final_norm_g)
```

```python
import functools

import jax
import jax.numpy as jnp
from jax import lax
from jax.experimental import pallas as pl
from jax.experimental.pallas import tpu as pltpu

F32 = jnp.float32
BF16 = jnp.bfloat16
HI = lax.Precision.HIGHEST

RMS_EPS = 1e-6
CHUNK = 64

GLA_HEADS = 4
GLA_GATE_RANK = 16
GLA_TAU = 16.0

GDN_QK_HEADS = 8
GDN_V_HEADS = 16
GDN_HEAD_DIM = 128
GDN_CONV = 4

PEER_HEADS = 8
PEER_TOPK = 16
PEER_NKEYS = 128
PEER_HALF = 128

LANE = 128
SUBLANE = 8
VMEM_LIMIT = 56 * 1024 * 1024

NT = (((1,), (1,)), ((), ()))


def _sigmoid(x):
    return 1.0 / (1.0 + jnp.exp(-x))


def _silu(x):
    return x * _sigmoid(x)


def _softplus(x):
    return jnp.maximum(x, 0.0) + jnp.log1p(jnp.exp(-jnp.abs(x)))


def _gelu_tanh(x):
    c = 0.7978845608028654
    return 0.5 * x * (1.0 + jnp.tanh(c * (x + 0.044715 * (x * x * x))))


def _norm_mod(x, g, sc, sh):
    ms = jnp.mean(x * x, axis=-1, keepdims=True)
    return (x * lax.rsqrt(ms + RMS_EPS) * g) * (1.0 + sc) + sh


def _bdot(a, b):
    return jnp.dot(a.astype(BF16), b.astype(BF16), preferred_element_type=F32)


def _bdot_nt(a, b):
    return lax.dot_general(a.astype(BF16), b.astype(BF16), NT, preferred_element_type=F32)


def _hdot(a, b):
    return jnp.dot(a, b, precision=HI, preferred_element_type=F32)


def _tril(n, k=0):
    r = lax.broadcasted_iota(jnp.int32, (n, n), 0)
    c = lax.broadcasted_iota(jnp.int32, (n, n), 1)
    return (c - r) <= k


def _mod_kernel(c_ref, w_ref, b_ref, o_ref):
    c = c_ref[...]
    o_ref[0] = _hdot(_silu(c), w_ref[0]) + b_ref[0]


def _mod_call(c_pad, ada_w, ada_b):
    depth, d, d6 = ada_w.shape
    nb = d6 // d
    return pl.pallas_call(
        _mod_kernel,
        grid=(depth, nb),
        in_specs=[
            pl.BlockSpec(c_pad.shape, lambda i, j: (0, 0)),
            pl.BlockSpec((1, d, d), lambda i, j: (i, 0, j)),
            pl.BlockSpec((1, 1, d), lambda i, j: (i, 0, j)),
        ],
        out_specs=pl.BlockSpec((1, c_pad.shape[0], d), lambda i, j: (i, 0, j)),
        out_shape=jax.ShapeDtypeStruct((depth, c_pad.shape[0], d6), F32),
        name="adaln_mod",
    )(c_pad, ada_w, ada_b.reshape(depth, 1, d6))


def _gla_kernel(x_ref, sh_ref, sc_ref, gt_ref, ng_ref, wqkvr_ref, wg1_ref, wg2_ref, bg2_ref,
                hg_ref, wout_ref, o_ref, st_ref, p_ref, la_ref, ob_ref, *, tb, dk, dv):
    hk = dk // GLA_HEADS
    hv = dv // GLA_HEADS
    nc = tb // CHUNK

    @pl.when(pl.program_id(1) == 0)
    def _():
        st_ref[...] = jnp.zeros_like(st_ref)

    x = x_ref[0]
    h = _norm_mod(x, ng_ref[...], sc_ref[0], sh_ref[0])
    hb = h.astype(BF16)
    p_ref[...] = jnp.dot(hb, wqkvr_ref[...], preferred_element_type=F32)
    g1 = jnp.dot(hb, wg1_ref[...], preferred_element_type=F32)
    g = jnp.dot(g1.astype(BF16), wg2_ref[...], preferred_element_type=F32) + bg2_ref[...]
    la_ref[...] = (jnp.minimum(g, 0.0) - jnp.log1p(jnp.exp(-jnp.abs(g)))) * (1.0 / GLA_TAU)

    tril = _tril(CHUNK).astype(F32)
    causal = _tril(CHUNK)
    scale = hk ** -0.5

    def chunk(c, carry):
        r0 = pl.multiple_of(c * CHUNK, CHUNK)
        rows = pl.ds(r0, CHUNK)
        cum = _hdot(tril, la_ref[rows, :])
        cl = cum[CHUNK - 1:CHUNK, :]
        e_pos = jnp.exp(cum)
        e_neg = jnp.exp(-cum)
        e_end = jnp.exp(cl - cum)
        dec = jnp.exp(cl)
        for hd in range(GLA_HEADS):
            ks = slice(hd * hk, (hd + 1) * hk)
            q = p_ref[rows, hd * hk:(hd + 1) * hk]
            k = p_ref[rows, dk + hd * hk:dk + (hd + 1) * hk]
            v = p_ref[rows, 2 * dk + hd * hv:2 * dk + (hd + 1) * hv]
            qi = q * (scale * e_pos[:, ks])
            ki = k * e_neg[:, ks]
            ke = k * e_end[:, ks]
            attn = jnp.where(causal, _bdot_nt(qi, ki), 0.0)
            st = st_ref[hd]
            o = _bdot(attn, v) + _bdot_nt(qi, st)
            ob_ref[rows, hd * hv:(hd + 1) * hv] = o
            st_ref[hd] = dec[:, ks] * st + _bdot(v.T, ke)
        return carry

    lax.fori_loop(0, nc, chunk, 0)

    hg = hg_ref[...]
    parts = []
    for hd in range(GLA_HEADS):
        oh = ob_ref[:, hd * hv:(hd + 1) * hv]
        ms = jnp.mean(oh * oh, axis=-1, keepdims=True)
        r = p_ref[:, 2 * dk + dv + hd * hv:2 * dk + dv + (hd + 1) * hv]
        parts.append((oh * lax.rsqrt(ms + RMS_EPS) * hg * _silu(r)).astype(BF16))
    y = jnp.concatenate(parts, axis=-1)
    m = jnp.dot(y, wout_ref[...], preferred_element_type=F32)
    o_ref[0] = x + gt_ref[0] * m


def _gla_call(x, sh, sc, gt, ng, w_in, w_g2, b_g2, head_g, w_out, *, tb):
    b, s, d = x.shape
    dk = w_g2.shape[1]
    dv = w_out.shape[0]
    wqkvr = w_in[:, :2 * dk + 2 * dv].astype(BF16)
    wg1 = jnp.pad(w_in[:, 2 * dk + 2 * dv:], ((0, 0), (0, LANE - GLA_GATE_RANK))).astype(BF16)
    wg2 = jnp.pad(w_g2, ((0, LANE - GLA_GATE_RANK), (0, 0))).astype(BF16)
    hv = dv // GLA_HEADS
    hk = dk // GLA_HEADS
    const2 = lambda bi, i: (0, 0)
    vec = pl.BlockSpec((1, 1, d), lambda bi, i: (bi, 0, 0))
    kern = functools.partial(_gla_kernel, tb=tb, dk=dk, dv=dv)
    return pl.pallas_call(
        kern,
        grid=(b, s // tb),
        in_specs=[
            pl.BlockSpec((1, tb, d), lambda bi, i: (bi, i, 0)),
            vec, vec, vec,
            pl.BlockSpec((1, d), const2),
            pl.BlockSpec(wqkvr.shape, const2),
            pl.BlockSpec(wg1.shape, const2),
            pl.BlockSpec(wg2.shape, const2),
            pl.BlockSpec((1, dk), const2),
            pl.BlockSpec((1, hv), const2),
            pl.BlockSpec((dv, d), const2),
        ],
        out_specs=pl.BlockSpec((1, tb, d), lambda bi, i: (bi, i, 0)),
        out_shape=jax.ShapeDtypeStruct(x.shape, F32),
        scratch_shapes=[
            pltpu.VMEM((GLA_HEADS, hv, hk), F32),
            pltpu.VMEM((tb, 2 * dk + 2 * dv), F32),
            pltpu.VMEM((tb, dk), F32),
            pltpu.VMEM((tb, dv), F32),
        ],
        compiler_params=pltpu.CompilerParams(
            dimension_semantics=("arbitrary", "arbitrary"), vmem_limit_bytes=VMEM_LIMIT),
        name="gla_layer",
    )(x, sh, sc, gt, ng.reshape(1, d), wqkvr, wg1, wg2, b_g2.reshape(1, dk),
      head_g.reshape(1, hv), w_out.astype(BF16))


def _unit_lower_inverse(a_strict):
    n = a_strict.shape[0]
    eye = (lax.broadcasted_iota(jnp.int32, (n, n), 0) == lax.broadcasted_iota(jnp.int32, (n, n), 1)).astype(F32)
    x = -a_strict
    t = eye + x
    steps = max(1, (n - 1).bit_length()) - 1
    for _ in range(steps):
        x = _hdot(x, x)
        t = t + _hdot(t, x)
    return t


def _gdn_kernel(x_ref, sh_ref, sc_ref, gt_ref, ng_ref, wqkv_ref, wz_ref, wa_ref, wb_ref, cw_ref,
                alog_ref, dtb_ref, hg_ref, wout_ref, o_ref,
                s_ref, cb_ref, q_ref, k_ref, v_ref, z_ref, g_ref, bt_ref, ob_ref, *, tb):
    dqk = GDN_QK_HEADS * GDN_HEAD_DIM
    dvv = GDN_V_HEADS * GDN_HEAD_DIM
    hd_ = GDN_HEAD_DIM
    nc = tb // CHUNK
    rep = GDN_V_HEADS // GDN_QK_HEADS

    @pl.when(pl.program_id(1) == 0)
    def _():
        s_ref[...] = jnp.zeros_like(s_ref)
        cb_ref[0:SUBLANE, :] = jnp.zeros((SUBLANE, cb_ref.shape[1]), F32)

    x = x_ref[0]
    h = _norm_mod(x, ng_ref[...], sc_ref[0], sh_ref[0])
    hb = h.astype(BF16)
    cb_ref[SUBLANE:SUBLANE + tb, :] = jnp.dot(hb, wqkv_ref[...], preferred_element_type=F32)
    z_ref[...] = jnp.dot(hb, wz_ref[...], preferred_element_type=F32)
    a = jnp.dot(hb, wa_ref[...], preferred_element_type=F32)
    bb = jnp.dot(hb, wb_ref[...], preferred_element_type=F32)
    g_ref[...] = -jnp.exp(alog_ref[...]) * _softplus(a + dtb_ref[...])
    bt_ref[...] = _sigmoid(bb)

    off = SUBLANE - (GDN_CONV - 1)
    conv = cw_ref[0:1, :] * cb_ref[off:off + tb, :]
    for j in range(1, GDN_CONV):
        conv = conv + cw_ref[j:j + 1, :] * cb_ref[off + j:off + j + tb, :]
    cb_ref[0:SUBLANE, :] = cb_ref[tb:tb + SUBLANE, :]
    qkv = _silu(conv)
    qscale = hd_ ** -0.5
    for hq in range(GDN_QK_HEADS):
        sl = slice(hq * hd_, (hq + 1) * hd_)
        qh = qkv[:, hq * hd_:(hq + 1) * hd_]
        kh = qkv[:, dqk + hq * hd_:dqk + (hq + 1) * hd_]
        q_ref[:, sl] = qh * (lax.rsqrt(jnp.sum(qh * qh, axis=-1, keepdims=True) + 1e-6) * qscale)
        k_ref[:, sl] = kh * lax.rsqrt(jnp.sum(kh * kh, axis=-1, keepdims=True) + 1e-6)
    v_ref[...] = qkv[:, 2 * dqk:]

    tril = _tril(CHUNK).astype(F32)
    lower = _tril(CHUNK)
    strict = _tril(CHUNK, -1)

    def chunk(c, carry):
        r0 = pl.multiple_of(c * CHUNK, CHUNK)
        rows = pl.ds(r0, CHUNK)
        gc = _hdot(tril, g_ref[rows, :])
        gct = gc.T
        gl = gc[CHUNK - 1:CHUNK, :]
        eg = jnp.exp(gc)
        eendt = jnp.exp(gl - gc).T
        dec = jnp.exp(gl)
        bt = bt_ref[rows, :]
        beg = bt * eg
        for hq in range(GDN_QK_HEADS):
            sl = slice(hq * hd_, (hq + 1) * hd_)
            qh = q_ref[rows, sl]
            kh = k_ref[rows, sl]
            kk = _bdot_nt(kh, kh)
            qk0 = _bdot_nt(qh, kh)
            kht = kh.T
            for r in range(rep):
                hvi = hq * rep + r
                vs = slice(hvi * hd_, (hvi + 1) * hd_)
                col = slice(hvi, hvi + 1)
                diff = gc[:, col] - gct[hvi:hvi + 1, :]
                decay = jnp.where(lower, jnp.exp(jnp.where(lower, diff, 0.0)), 0.0)
                a_mat = jnp.where(strict, bt[:, col] * kk * decay, 0.0)
                tinv = _unit_lower_inverse(a_mat)
                vh = v_ref[rows, vs]
                u = _hdot(tinv, bt[:, col] * vh)
                w = _hdot(tinv, beg[:, col] * kh)
                qk = qk0 * decay
                qd = qh * eg[:, col]
                s = s_ref[hvi]
                vn = u - _bdot(w, s)
                ob_ref[rows, vs] = _bdot(qd, s) + _bdot(qk, vn)
                s_ref[hvi] = dec[:, col] * s + _bdot(kht * eendt[hvi:hvi + 1, :], vn)
        return carry

    lax.fori_loop(0, nc, chunk, 0)

    hg = hg_ref[...]
    parts = []
    for hvi in range(GDN_V_HEADS):
        vs = slice(hvi * hd_, (hvi + 1) * hd_)
        oh = ob_ref[:, vs]
        ms = jnp.mean(oh * oh, axis=-1, keepdims=True)
        parts.append((oh * lax.rsqrt(ms + RMS_EPS) * hg * _silu(z_ref[:, vs])).astype(BF16))
    y = jnp.concatenate(parts, axis=-1)
    m = jnp.dot(y, wout_ref[...], preferred_element_type=F32)
    o_ref[0] = x + gt_ref[0] * m


def _gdn_call(x, sh, sc, gt, ng, w_in, conv_w, a_log, dt_bias, head_g, w_out, *, tb):
    b, s, d = x.shape
    dqk = GDN_QK_HEADS * GDN_HEAD_DIM
    dvv = GDN_V_HEADS * GDN_HEAD_DIM
    nqkv = 2 * dqk + dvv
    wqkv = w_in[:, :nqkv].astype(BF16)
    wz = w_in[:, nqkv:nqkv + dvv].astype(BF16)
    padh = ((0, 0), (0, LANE - GDN_V_HEADS))
    wa = jnp.pad(w_in[:, nqkv + dvv:nqkv + dvv + GDN_V_HEADS], padh).astype(BF16)
    wb = jnp.pad(w_in[:, nqkv + dvv + GDN_V_HEADS:], padh).astype(BF16)
    alog = jnp.pad(a_log.reshape(1, -1), padh)
    dtb = jnp.pad(dt_bias.reshape(1, -1), padh)
    const2 = lambda bi, i: (0, 0)
    vec = pl.BlockSpec((1, 1, d), lambda bi, i: (bi, 0, 0))
    kern = functools.partial(_gdn_kernel, tb=tb)
    return pl.pallas_call(
        kern,
        grid=(b, s // tb),
        in_specs=[
            pl.BlockSpec((1, tb, d), lambda bi, i: (bi, i, 0)),
            vec, vec, vec,
            pl.BlockSpec((1, d), const2),
            pl.BlockSpec(wqkv.shape, const2),
            pl.BlockSpec(wz.shape, const2),
            pl.BlockSpec(wa.shape, const2),
            pl.BlockSpec(wb.shape, const2),
            pl.BlockSpec(conv_w.shape, const2),
            pl.BlockSpec((1, LANE), const2),
            pl.BlockSpec((1, LANE), const2),
            pl.BlockSpec((1, GDN_HEAD_DIM), const2),
            pl.BlockSpec((dvv, d), const2),
        ],
        out_specs=pl.BlockSpec((1, tb, d), lambda bi, i: (bi, i, 0)),
        out_shape=jax.ShapeDtypeStruct(x.shape, F32),
        scratch_shapes=[
            pltpu.VMEM((GDN_V_HEADS, GDN_HEAD_DIM, GDN_HEAD_DIM), F32),
            pltpu.VMEM((tb + SUBLANE, nqkv), F32),
            pltpu.VMEM((tb, dqk), F32),
            pltpu.VMEM((tb, dqk), F32),
            pltpu.VMEM((tb, dvv), F32),
            pltpu.VMEM((tb, dvv), F32),
            pltpu.VMEM((tb, LANE), F32),
            pltpu.VMEM((tb, LANE), F32),
            pltpu.VMEM((tb, dvv), F32),
        ],
        compiler_params=pltpu.CompilerParams(
            dimension_semantics=("arbitrary", "arbitrary"), vmem_limit_bytes=VMEM_LIMIT),
        name="gdn_layer",
    )(x, sh, sc, gt, ng.reshape(1, d), wqkv, wz, wa, wb, conv_w, alog, dtb,
      head_g.reshape(1, GDN_HEAD_DIM), w_out.astype(BF16))


def _top_values(x, n):
    vals = []
    for r in range(n):
        m = jnp.max(x, axis=0, keepdims=True)
        vals.append(m)
        if r + 1 < n:
            x = jnp.where(x == m, -jnp.inf, x)
    return vals


def _stack_rows(rows):
    l = rows[0].shape[1]
    sub = lax.broadcasted_iota(jnp.int32, (SUBLANE, l), 0)
    out = jnp.broadcast_to(rows[0], (SUBLANE, l))
    for i in range(1, len(rows)):
        out = jnp.where(sub == i, rows[i], out)
    return out


def _peer_kernel(x_ref, sh_ref, sc_ref, gt_ref, ng_ref, wq_ref, k1_ref, k2_ref, u_ref, vt_ref, o_ref,
                 tb_ref, s1_ref, w1_ref, s2_ref, e2_ref, thr_ref, yt_ref, ht_ref, at_ref, *, tt, eb):
    e = pl.program_id(2)
    ne = pl.num_programs(2)
    nj = tt // LANE
    ab = eb // PEER_NKEYS
    k = PEER_TOPK

    @pl.when(e == 0)
    def _():
        x = x_ref[0]
        t = _norm_mod(x, ng_ref[...], sc_ref[0], sh_ref[0])
        tb = t.astype(BF16)
        tb_ref[...] = tb
        q = jnp.dot(tb, wq_ref[...], preferred_element_type=F32)
        for h in range(PEER_HEADS):
            q1 = q[:, (2 * h) * PEER_HALF:(2 * h + 1) * PEER_HALF]
            q2 = q[:, (2 * h + 1) * PEER_HALF:(2 * h + 2) * PEER_HALF]
            s1_ref[h] = lax.dot_general(k1_ref[...], q1, NT, precision=HI, preferred_element_type=F32)
            s2_ref[h] = lax.dot_general(k2_ref[...], q2, NT, precision=HI, preferred_element_type=F32)
        yt_ref[...] = jnp.zeros_like(yt_ref)

        def route(idx, carry):
            h = idx // nj
            c0 = pl.multiple_of((idx % nj) * LANE, LANE)
            cols = pl.ds(c0, LANE)
            x1 = s1_ref[h, :, cols]
            x2 = s2_ref[h, :, cols]
            v1 = _top_values(x1, k)
            v2 = _top_values(x2, k)
            v2lo = _stack_rows(v2[:SUBLANE])
            v2hi = _stack_rows(v2[SUBLANE:])
            v1hi = _stack_rows(v1[SUBLANE:])
            tiles = [v1[0] + v2lo, v1[0] + v2hi, v1hi + v2[0]]
            tiles += [v1[i] + v2lo for i in range(1, SUBLANE)]
            cand = jnp.concatenate(tiles, axis=0)
            thr = _top_values(cand, k)[-1]
            top = v1[0] + v2[0]
            z = jnp.sum(jnp.where(cand >= thr, jnp.exp(cand - top), 0.0), axis=0, keepdims=True)
            w1_ref[h, :, cols] = jnp.exp(x1 - v1[0]) / z
            e2_ref[h, :, cols] = jnp.exp(x2 - v2[0])
            thr_ref[h, :, cols] = jnp.broadcast_to(thr, (SUBLANE, LANE))
            return carry

        lax.fori_loop(0, PEER_HEADS * nj, route, 0)

    ht_ref[...] = lax.dot_general(u_ref[...], tb_ref[...], NT, preferred_element_type=F32)

    def mix(j, carry):
        cols = pl.ds(pl.multiple_of(j * LANE, LANE), LANE)
        arows = pl.ds(pl.multiple_of(e * ab, SUBLANE), SUBLANE)
        s1t = [s1_ref[h, arows, cols] for h in range(PEER_HEADS)]
        w1t = [w1_ref[h, arows, cols] for h in range(PEER_HEADS)]
        thr = [thr_ref[h, 0:1, cols] for h in range(PEER_HEADS)]
        for al in range(ab):
            rows = slice(al * PEER_NKEYS, (al + 1) * PEER_NKEYS)
            acc = jnp.zeros((PEER_NKEYS, LANE), F32)
            for h in range(PEER_HEADS):
                sel = (s2_ref[h, :, cols] + s1t[h][al:al + 1, :]) >= thr[h]
                acc = acc + jnp.where(sel, e2_ref[h, :, cols] * w1t[h][al:al + 1, :], 0.0)
            at_ref[rows, cols] = (acc * _gelu_tanh(ht_ref[rows, cols])).astype(BF16)
        return carry

    lax.fori_loop(0, nj, mix, 0)

    yt_ref[...] += jnp.dot(vt_ref[...], at_ref[...], preferred_element_type=F32)

    @pl.when(e == ne - 1)
    def _():
        o_ref[0] = x_ref[0] + gt_ref[0] * yt_ref[...].T


def _peer_call(x, sh, sc, gt, ng, wq, k1, k2, u, v, *, tt, eb):
    b, s, d = x.shape
    assert eb == SUBLANE * PEER_NKEYS and u.shape[0] % eb == 0
    ne = u.shape[0] // eb
    ub = u.astype(BF16)
    vtb = v.T.astype(BF16)
    const3 = lambda bi, i, e: (0, 0)
    vec = pl.BlockSpec((1, 1, d), lambda bi, i, e: (bi, 0, 0))
    kern = functools.partial(_peer_kernel, tt=tt, eb=eb)
    sshape = (PEER_HEADS, PEER_NKEYS, tt)
    return pl.pallas_call(
        kern,
        grid=(b, s // tt, ne),
        in_specs=[
            pl.BlockSpec((1, tt, d), lambda bi, i, e: (bi, i, 0)),
            vec, vec, vec,
            pl.BlockSpec((1, d), const3),
            pl.BlockSpec(wq.shape, const3),
            pl.BlockSpec(k1.shape, const3),
            pl.BlockSpec(k2.shape, const3),
            pl.BlockSpec((eb, d), lambda bi, i, e: (e, 0)),
            pl.BlockSpec((d, eb), lambda bi, i, e: (0, e)),
        ],
        out_specs=pl.BlockSpec((1, tt, d), lambda bi, i, e: (bi, i, 0)),
        out_shape=jax.ShapeDtypeStruct(x.shape, F32),
        scratch_shapes=[
            pltpu.VMEM((tt, d), BF16),
            pltpu.VMEM(sshape, F32),
            pltpu.VMEM(sshape, F32),
            pltpu.VMEM(sshape, F32),
            pltpu.VMEM(sshape, F32),
            pltpu.VMEM((PEER_HEADS, SUBLANE, tt), F32),
            pltpu.VMEM((d, tt), F32),
            pltpu.VMEM((eb, tt), F32),
            pltpu.VMEM((eb, tt), BF16),
        ],
        compiler_params=pltpu.CompilerParams(
            dimension_semantics=("arbitrary", "arbitrary", "arbitrary"), vmem_limit_bytes=VMEM_LIMIT),
        name="peer_layer",
    )(x, sh, sc, gt, ng.reshape(1, d), wq.astype(BF16), k1, k2, ub, vtb)


def _final_kernel(x_ref, g_ref, o_ref):
    x = x_ref[0]
    ms = jnp.mean(x * x, axis=-1, keepdims=True)
    o_ref[0] = x * lax.rsqrt(ms + RMS_EPS) * g_ref[...]


def _final_call(x, g, *, tb):
    b, s, d = x.shape
    return pl.pallas_call(
        _final_kernel,
        grid=(b, s // tb),
        in_specs=[pl.BlockSpec((1, tb, d), lambda bi, i: (bi, i, 0)),
                  pl.BlockSpec((1, d), lambda bi, i: (0, 0))],
        out_specs=pl.BlockSpec((1, tb, d), lambda bi, i: (bi, i, 0)),
        out_shape=jax.ShapeDtypeStruct(x.shape, F32),
        name="final_norm",
    )(x, g.reshape(1, d))


def _pick(n, pref):
    return pref if n % pref == 0 else n


def kernel(x, c, ada_w, ada_b, norm_mix_g, norm_ffn_g, gla_w_in, gla_w_g2, gla_b_g2, gla_norm_g,
           gla_w_out, gdn_w_in, gdn_conv_w, gdn_a_log, gdn_dt_bias, gdn_norm_g, gdn_w_out,
           peer_wq, peer_k1, peer_k2, peer_u, peer_v, final_norm_g):
    b, s, d = x.shape
    depth = ada_w.shape[0]
    c_pad = jnp.pad(c, ((0, SUBLANE - b % SUBLANE if b % SUBLANE else 0), (0, 0)))
    mod = _mod_call(c_pad, ada_w, ada_b)[:, :b, :].reshape(depth, b, 6, 1, d)
    tb_mix = _pick(s, 256)
    tt = _pick(s, 512)
    eb = 1024
    for i in range(depth):
        sh1, sc1, gt1, sh2, sc2, gt2 = (mod[i, :, j] for j in range(6))
        j = i // 2
        if i % 2 == 0:
            x = _gla_call(x, sh1, sc1, gt1, norm_mix_g[i], gla_w_in[j], gla_w_g2[j], gla_b_g2[j],
                          gla_norm_g[j], gla_w_out[j], tb=tb_mix)
        else:
            x = _gdn_call(x, sh1, sc1, gt1, norm_mix_g[i], gdn_w_in[j], gdn_conv_w[j], gdn_a_log[j],
                          gdn_dt_bias[j], gdn_norm_g[j], gdn_w_out[j], tb=tb_mix)
        x = _peer_call(x, sh2, sc2, gt2, norm_ffn_g[i], peer_wq[i], peer_k1[i], peer_k2[i],
                       peer_u[i], peer_v[i], tt=tt, eb=eb)
    return _final_call(x, final_norm_g, tb=_pick(s, 512))
```

```python
import functools

import jax
import jax.numpy as jnp
from jax import lax
from jax.experimental import pallas as pl
from jax.experimental.pallas import tpu as pltpu

F32 = jnp.float32
BF16 = jnp.bfloat16
HI = lax.Precision.HIGHEST

RMS_EPS = 1e-6
CHUNK = 64

GLA_HEADS = 4
GLA_GATE_RANK = 16
GLA_TAU = 16.0

GDN_QK_HEADS = 8
GDN_V_HEADS = 16
GDN_HEAD_DIM = 128
GDN_CONV = 4

PEER_HEADS = 8
PEER_TOPK = 16
PEER_NKEYS = 128
PEER_HALF = 128

LANE = 128
SUBLANE = 8
VMEM_LIMIT = 56 * 1024 * 1024

NT = (((1,), (1,)), ((), ()))


def _sigmoid(x):
    return 1.0 / (1.0 + jnp.exp(-x))


def _silu(x):
    return x * _sigmoid(x)


def _softplus(x):
    return jnp.maximum(x, 0.0) + jnp.log1p(jnp.exp(-jnp.abs(x)))


def _gelu_tanh(x):
    c = 0.7978845608028654
    return 0.5 * x * (1.0 + jnp.tanh(c * (x + 0.044715 * (x * x * x))))


def _norm_mod(x, g, sc, sh):
    ms = jnp.mean(x * x, axis=-1, keepdims=True)
    return (x * lax.rsqrt(ms + RMS_EPS) * g) * (1.0 + sc) + sh


def _bdot(a, b):
    return jnp.dot(a.astype(BF16), b.astype(BF16), preferred_element_type=F32)


def _bdot_nt(a, b):
    return lax.dot_general(a.astype(BF16), b.astype(BF16), NT, preferred_element_type=F32)


def _hdot(a, b):
    return jnp.dot(a, b, precision=HI, preferred_element_type=F32)


def _tril(n, k=0):
    r = lax.broadcasted_iota(jnp.int32, (n, n), 0)
    c = lax.broadcasted_iota(jnp.int32, (n, n), 1)
    return (c - r) <= k


def _mod_kernel(c_ref, w_ref, b_ref, o_ref):
    c = c_ref[...]
    o_ref[0] = _hdot(_silu(c), w_ref[0]) + b_ref[0]


def _mod_call(c_pad, ada_w, ada_b):
    depth, d, d6 = ada_w.shape
    nb = d6 // d
    return pl.pallas_call(
        _mod_kernel,
        grid=(depth, nb),
        in_specs=[
            pl.BlockSpec(c_pad.shape, lambda i, j: (0, 0)),
            pl.BlockSpec((1, d, d), lambda i, j: (i, 0, j)),
            pl.BlockSpec((1, 1, d), lambda i, j: (i, 0, j)),
        ],
        out_specs=pl.BlockSpec((1, c_pad.shape[0], d), lambda i, j: (i, 0, j)),
        out_shape=jax.ShapeDtypeStruct((depth, c_pad.shape[0], d6), F32),
        name="adaln_mod",
    )(c_pad, ada_w, ada_b.reshape(depth, 1, d6))


def _gla_kernel(x_ref, sh_ref, sc_ref, gt_ref, ng_ref, wqkvr_ref, wg1_ref, wg2_ref, bg2_ref,
                hg_ref, wout_ref, o_ref, st_ref, p_ref, la_ref, ob_ref, *, tb, dk, dv):
    hk = dk // GLA_HEADS
    hv = dv // GLA_HEADS
    nc = tb // CHUNK

    @pl.when(pl.program_id(1) == 0)
    def _():
        st_ref[...] = jnp.zeros_like(st_ref)

    x = x_ref[0]
    h = _norm_mod(x, ng_ref[...], sc_ref[0], sh_ref[0])
    hb = h.astype(BF16)
    p_ref[...] = jnp.dot(hb, wqkvr_ref[...], preferred_element_type=F32)
    g1 = jnp.dot(hb, wg1_ref[...], preferred_element_type=F32)
    g = jnp.dot(g1.astype(BF16), wg2_ref[...], preferred_element_type=F32) + bg2_ref[...]
    la_ref[...] = (jnp.minimum(g, 0.0) - jnp.log1p(jnp.exp(-jnp.abs(g)))) * (1.0 / GLA_TAU)

    tril = _tril(CHUNK).astype(F32)
    causal = _tril(CHUNK)
    scale = hk ** -0.5

    for c in range(nc):
        rows = slice(c * CHUNK, (c + 1) * CHUNK)
        cum = _hdot(tril, la_ref[rows, :])
        cl = cum[CHUNK - 1:CHUNK, :]
        e_pos = jnp.exp(cum)
        e_neg = jnp.exp(-cum)
        e_end = jnp.exp(cl - cum)
        dec = jnp.exp(cl)
        hr = range(GLA_HEADS)
        ksl = [slice(hd * hk, (hd + 1) * hk) for hd in hr]
        qis = [p_ref[rows, hd * hk:(hd + 1) * hk] * (scale * e_pos[:, ksl[hd]]) for hd in hr]
        kraw = [p_ref[rows, dk + hd * hk:dk + (hd + 1) * hk] for hd in hr]
        vs = [p_ref[rows, 2 * dk + hd * hv:2 * dk + (hd + 1) * hv] for hd in hr]
        attns = [jnp.where(causal, _bdot_nt(qis[hd], kraw[hd] * e_neg[:, ksl[hd]]), 0.0) for hd in hr]
        sts = [st_ref[hd] for hd in hr]
        inter = [_bdot_nt(qis[hd], sts[hd]) for hd in hr]
        upds = [_bdot(vs[hd].T, kraw[hd] * e_end[:, ksl[hd]]) for hd in hr]
        intra = [_bdot(attns[hd], vs[hd]) for hd in hr]
        for hd in hr:
            ob_ref[rows, hd * hv:(hd + 1) * hv] = intra[hd] + inter[hd]
            st_ref[hd] = dec[:, ksl[hd]] * sts[hd] + upds[hd]

    hg = hg_ref[...]
    parts = []
    for hd in range(GLA_HEADS):
        oh = ob_ref[:, hd * hv:(hd + 1) * hv]
        ms = jnp.mean(oh * oh, axis=-1, keepdims=True)
        r = p_ref[:, 2 * dk + dv + hd * hv:2 * dk + dv + (hd + 1) * hv]
        parts.append((oh * lax.rsqrt(ms + RMS_EPS) * hg * _silu(r)).astype(BF16))
    y = jnp.concatenate(parts, axis=-1)
    m = jnp.dot(y, wout_ref[...], preferred_element_type=F32)
    o_ref[0] = x + gt_ref[0] * m


def _gla_call(x, sh, sc, gt, ng, w_in, w_g2, b_g2, head_g, w_out, *, tb):
    b, s, d = x.shape
    dk = w_g2.shape[1]
    dv = w_out.shape[0]
    wqkvr = w_in[:, :2 * dk + 2 * dv].astype(BF16)
    wg1 = jnp.pad(w_in[:, 2 * dk + 2 * dv:], ((0, 0), (0, LANE - GLA_GATE_RANK))).astype(BF16)
    wg2 = jnp.pad(w_g2, ((0, LANE - GLA_GATE_RANK), (0, 0))).astype(BF16)
    hv = dv // GLA_HEADS
    hk = dk // GLA_HEADS
    const2 = lambda bi, i: (0, 0)
    vec = pl.BlockSpec((1, 1, d), lambda bi, i: (bi, 0, 0))
    kern = functools.partial(_gla_kernel, tb=tb, dk=dk, dv=dv)
    return pl.pallas_call(
        kern,
        grid=(b, s // tb),
        in_specs=[
            pl.BlockSpec((1, tb, d), lambda bi, i: (bi, i, 0)),
            vec, vec, vec,
            pl.BlockSpec((1, d), const2),
            pl.BlockSpec(wqkvr.shape, const2),
            pl.BlockSpec(wg1.shape, const2),
            pl.BlockSpec(wg2.shape, const2),
            pl.BlockSpec((1, dk), const2),
            pl.BlockSpec((1, hv), const2),
            pl.BlockSpec((dv, d), const2),
        ],
        out_specs=pl.BlockSpec((1, tb, d), lambda bi, i: (bi, i, 0)),
        out_shape=jax.ShapeDtypeStruct(x.shape, F32),
        scratch_shapes=[
            pltpu.VMEM((GLA_HEADS, hv, hk), F32),
            pltpu.VMEM((tb, 2 * dk + 2 * dv), F32),
            pltpu.VMEM((tb, dk), F32),
            pltpu.VMEM((tb, dv), F32),
        ],
        compiler_params=pltpu.CompilerParams(
            dimension_semantics=("arbitrary", "arbitrary"), vmem_limit_bytes=VMEM_LIMIT),
        name="gla_layer",
    )(x, sh, sc, gt, ng.reshape(1, d), wqkvr, wg1, wg2, b_g2.reshape(1, dk),
      head_g.reshape(1, hv), w_out.astype(BF16))


def _block_inverse_masks(n):
    r = lax.broadcasted_iota(jnp.int32, (n, n), 0)
    c = lax.broadcasted_iota(jnp.int32, (n, n), 1)
    eye = (r == c).astype(F32)
    first = (r // 2) == (c // 2)
    quads = []
    s = 2
    while s < n:
        quads.append(((r // (2 * s)) == (c // (2 * s))) & ((r // s) != (c // s)))
        s *= 2
    return eye, first, quads


def _unit_lower_inverse(a_list, masks):
    eye, first, quads = masks
    ds = [eye - jnp.where(first, a, 0.0) for a in a_list]
    for quad in quads:
        ms = [_bdot(d, jnp.where(quad, a, 0.0)) for d, a in zip(ds, a_list)]
        ds = [d - _bdot(m, d) for d, m in zip(ds, ms)]
    return ds


def _gdn_kernel(x_ref, sh_ref, sc_ref, gt_ref, ng_ref, wqkv_ref, wz_ref, wa_ref, wb_ref, cw_ref,
                alog_ref, dtb_ref, hg_ref, wout_ref, o_ref,
                s_ref, cb_ref, q_ref, k_ref, v_ref, z_ref, g_ref, bt_ref, ob_ref, *, tb):
    dqk = GDN_QK_HEADS * GDN_HEAD_DIM
    dvv = GDN_V_HEADS * GDN_HEAD_DIM
    hd_ = GDN_HEAD_DIM
    nc = tb // CHUNK
    rep = GDN_V_HEADS // GDN_QK_HEADS

    @pl.when(pl.program_id(1) == 0)
    def _():
        s_ref[...] = jnp.zeros_like(s_ref)
        cb_ref[0:SUBLANE, :] = jnp.zeros((SUBLANE, cb_ref.shape[1]), F32)

    x = x_ref[0]
    h = _norm_mod(x, ng_ref[...], sc_ref[0], sh_ref[0])
    hb = h.astype(BF16)
    cb_ref[SUBLANE:SUBLANE + tb, :] = jnp.dot(hb, wqkv_ref[...], preferred_element_type=F32)
    z_ref[...] = jnp.dot(hb, wz_ref[...], preferred_element_type=F32)
    a = jnp.dot(hb, wa_ref[...], preferred_element_type=F32)
    bb = jnp.dot(hb, wb_ref[...], preferred_element_type=F32)
    g_ref[...] = -jnp.exp(alog_ref[...]) * _softplus(a + dtb_ref[...])
    bt_ref[...] = _sigmoid(bb)

    off = SUBLANE - (GDN_CONV - 1)
    conv = cw_ref[0:1, :] * cb_ref[off:off + tb, :]
    for j in range(1, GDN_CONV):
        conv = conv + cw_ref[j:j + 1, :] * cb_ref[off + j:off + j + tb, :]
    cb_ref[0:SUBLANE, :] = cb_ref[tb:tb + SUBLANE, :]
    qkv = _silu(conv)
    qscale = hd_ ** -0.5
    for hq in range(GDN_QK_HEADS):
        sl = slice(hq * hd_, (hq + 1) * hd_)
        qh = qkv[:, hq * hd_:(hq + 1) * hd_]
        kh = qkv[:, dqk + hq * hd_:dqk + (hq + 1) * hd_]
        q_ref[:, sl] = qh * (lax.rsqrt(jnp.sum(qh * qh, axis=-1, keepdims=True) + 1e-6) * qscale)
        k_ref[:, sl] = kh * lax.rsqrt(jnp.sum(kh * kh, axis=-1, keepdims=True) + 1e-6)
    v_ref[...] = qkv[:, 2 * dqk:]

    tril = _tril(CHUNK).astype(F32)
    lower = _tril(CHUNK)
    strict = _tril(CHUNK, -1)
    inv_masks = _block_inverse_masks(CHUNK)

    for c in range(nc):
        rows = slice(c * CHUNK, (c + 1) * CHUNK)
        gc = _hdot(tril, g_ref[rows, :])
        gct = gc.T
        gl = gc[CHUNK - 1:CHUNK, :]
        eg = jnp.exp(gc)
        eendt = jnp.exp(gl - gc).T
        dec = jnp.exp(gl)
        bt = bt_ref[rows, :]
        beg = bt * eg
        hq_r = range(GDN_QK_HEADS)
        hv_r = range(GDN_V_HEADS)
        qs = [q_ref[rows, hq * hd_:(hq + 1) * hd_] for hq in hq_r]
        ks = [k_ref[rows, hq * hd_:(hq + 1) * hd_] for hq in hq_r]
        kks = [_bdot_nt(ks[hq], ks[hq]) for hq in hq_r]
        qk0s = [_bdot_nt(qs[hq], ks[hq]) for hq in hq_r]
        khts = [ks[hq].T for hq in hq_r]
        cols = [slice(hvi, hvi + 1) for hvi in hv_r]
        decays = []
        for hvi in hv_r:
            diff = gc[:, cols[hvi]] - gct[hvi:hvi + 1, :]
            decays.append(jnp.where(lower, jnp.exp(jnp.where(lower, diff, 0.0)), 0.0))
        amats = [jnp.where(strict, bt[:, cols[hvi]] * kks[hvi // rep] * decays[hvi], 0.0) for hvi in hv_r]
        tinvs = _unit_lower_inverse(amats, inv_masks)
        rhs = [jnp.concatenate([bt[:, cols[hvi]] * v_ref[rows, hvi * hd_:(hvi + 1) * hd_],
                                beg[:, cols[hvi]] * ks[hvi // rep]], axis=1) for hvi in hv_r]
        uws = [_bdot(tinvs[hvi], rhs[hvi]) for hvi in hv_r]
        ss = [s_ref[hvi] for hvi in hv_r]
        wss = [_bdot(uws[hvi][:, hd_:], ss[hvi]) for hvi in hv_r]
        qss = [_bdot(qs[hvi // rep] * eg[:, cols[hvi]], ss[hvi]) for hvi in hv_r]
        vns = [uws[hvi][:, :hd_] - wss[hvi] for hvi in hv_r]
        outs = [qss[hvi] + _bdot(qk0s[hvi // rep] * decays[hvi], vns[hvi]) for hvi in hv_r]
        upds = [_bdot(khts[hvi // rep] * eendt[hvi:hvi + 1, :], vns[hvi]) for hvi in hv_r]
        for hvi in hv_r:
            ob_ref[rows, hvi * hd_:(hvi + 1) * hd_] = outs[hvi]
            s_ref[hvi] = dec[:, cols[hvi]] * ss[hvi] + upds[hvi]

    hg = hg_ref[...]
    parts = []
    for hvi in range(GDN_V_HEADS):
        vs = slice(hvi * hd_, (hvi + 1) * hd_)
        oh = ob_ref[:, vs]
        ms = jnp.mean(oh * oh, axis=-1, keepdims=True)
        parts.append((oh * lax.rsqrt(ms + RMS_EPS) * hg * _silu(z_ref[:, vs])).astype(BF16))
    y = jnp.concatenate(parts, axis=-1)
    m = jnp.dot(y, wout_ref[...], preferred_element_type=F32)
    o_ref[0] = x + gt_ref[0] * m


def _gdn_call(x, sh, sc, gt, ng, w_in, conv_w, a_log, dt_bias, head_g, w_out, *, tb):
    b, s, d = x.shape
    dqk = GDN_QK_HEADS * GDN_HEAD_DIM
    dvv = GDN_V_HEADS * GDN_HEAD_DIM
    nqkv = 2 * dqk + dvv
    wqkv = w_in[:, :nqkv].astype(BF16)
    wz = w_in[:, nqkv:nqkv + dvv].astype(BF16)
    padh = ((0, 0), (0, LANE - GDN_V_HEADS))
    wa = jnp.pad(w_in[:, nqkv + dvv:nqkv + dvv + GDN_V_HEADS], padh).astype(BF16)
    wb = jnp.pad(w_in[:, nqkv + dvv + GDN_V_HEADS:], padh).astype(BF16)
    alog = jnp.pad(a_log.reshape(1, -1), padh)
    dtb = jnp.pad(dt_bias.reshape(1, -1), padh)
    const2 = lambda bi, i: (0, 0)
    vec = pl.BlockSpec((1, 1, d), lambda bi, i: (bi, 0, 0))
    kern = functools.partial(_gdn_kernel, tb=tb)
    return pl.pallas_call(
        kern,
        grid=(b, s // tb),
        in_specs=[
            pl.BlockSpec((1, tb, d), lambda bi, i: (bi, i, 0)),
            vec, vec, vec,
            pl.BlockSpec((1, d), const2),
            pl.BlockSpec(wqkv.shape, const2),
            pl.BlockSpec(wz.shape, const2),
            pl.BlockSpec(wa.shape, const2),
            pl.BlockSpec(wb.shape, const2),
            pl.BlockSpec(conv_w.shape, const2),
            pl.BlockSpec((1, LANE), const2),
            pl.BlockSpec((1, LANE), const2),
            pl.BlockSpec((1, GDN_HEAD_DIM), const2),
            pl.BlockSpec((dvv, d), const2),
        ],
        out_specs=pl.BlockSpec((1, tb, d), lambda bi, i: (bi, i, 0)),
        out_shape=jax.ShapeDtypeStruct(x.shape, F32),
        scratch_shapes=[
            pltpu.VMEM((GDN_V_HEADS, GDN_HEAD_DIM, GDN_HEAD_DIM), F32),
            pltpu.VMEM((tb + SUBLANE, nqkv), F32),
            pltpu.VMEM((tb, dqk), F32),
            pltpu.VMEM((tb, dqk), F32),
            pltpu.VMEM((tb, dvv), F32),
            pltpu.VMEM((tb, dvv), F32),
            pltpu.VMEM((tb, LANE), F32),
            pltpu.VMEM((tb, LANE), F32),
            pltpu.VMEM((tb, dvv), F32),
        ],
        compiler_params=pltpu.CompilerParams(
            dimension_semantics=("arbitrary", "arbitrary"), vmem_limit_bytes=VMEM_LIMIT),
        name="gdn_layer",
    )(x, sh, sc, gt, ng.reshape(1, d), wqkv, wz, wa, wb, conv_w, alog, dtb,
      head_g.reshape(1, GDN_HEAD_DIM), w_out.astype(BF16))


def _top_values(x, n):
    vals = []
    for r in range(n):
        m = jnp.max(x, axis=0, keepdims=True)
        vals.append(m)
        if r + 1 < n:
            x = jnp.where(x == m, -jnp.inf, x)
    return vals


def _stack_rows(rows):
    l = rows[0].shape[1]
    sub = lax.broadcasted_iota(jnp.int32, (SUBLANE, l), 0)
    out = jnp.broadcast_to(rows[0], (SUBLANE, l))
    for i in range(1, len(rows)):
        out = jnp.where(sub == i, rows[i], out)
    return out


def _peer_kernel(x_ref, sh_ref, sc_ref, gt_ref, ng_ref, wq_ref, k1_ref, k2_ref, u_ref, vt_ref, o_ref,
                 tb_ref, s1_ref, w1_ref, s2_ref, e2_ref, thr_ref, yt_ref, ht_ref, at_ref, *, tt, eb):
    e = pl.program_id(2)
    ne = pl.num_programs(2)
    nj = tt // LANE
    ab = eb // PEER_NKEYS
    k = PEER_TOPK

    @pl.when(e == 0)
    def _():
        x = x_ref[0]
        t = _norm_mod(x, ng_ref[...], sc_ref[0], sh_ref[0])
        tb = t.astype(BF16)
        tb_ref[...] = tb
        q = jnp.dot(tb, wq_ref[...], preferred_element_type=F32)
        for h in range(PEER_HEADS):
            q1 = q[:, (2 * h) * PEER_HALF:(2 * h + 1) * PEER_HALF]
            q2 = q[:, (2 * h + 1) * PEER_HALF:(2 * h + 2) * PEER_HALF]
            s1_ref[h] = lax.dot_general(k1_ref[...], q1, NT, precision=HI, preferred_element_type=F32)
            s2_ref[h] = lax.dot_general(k2_ref[...], q2, NT, precision=HI, preferred_element_type=F32)
        yt_ref[...] = jnp.zeros_like(yt_ref)

        def route(idx, carry):
            h = idx // nj
            c0 = pl.multiple_of((idx % nj) * LANE, LANE)
            cols = pl.ds(c0, LANE)
            x1 = s1_ref[h, :, cols]
            x2 = s2_ref[h, :, cols]
            v1 = _top_values(x1, k)
            v2 = _top_values(x2, k)
            v2lo = _stack_rows(v2[:SUBLANE])
            v2hi = _stack_rows(v2[SUBLANE:])
            v1hi = _stack_rows(v1[SUBLANE:])
            tiles = [v1[0] + v2lo, v1[0] + v2hi, v1hi + v2[0]]
            tiles += [v1[i] + v2lo for i in range(1, SUBLANE)]
            cand = jnp.concatenate(tiles, axis=0)
            thr = _top_values(cand, k)[-1]
            top = v1[0] + v2[0]
            z = jnp.sum(jnp.where(cand >= thr, jnp.exp(cand - top), 0.0), axis=0, keepdims=True)
            w1_ref[h, :, cols] = jnp.exp(x1 - v1[0]) / z
            e2_ref[h, :, cols] = jnp.exp(x2 - v2[0])
            thr_ref[h, :, cols] = jnp.broadcast_to(thr, (SUBLANE, LANE))
            return carry

        lax.fori_loop(0, PEER_HEADS * nj, route, 0)

    ht_ref[...] = lax.dot_general(u_ref[...], tb_ref[...], NT, preferred_element_type=F32)

    def mix(j, carry):
        cols = pl.ds(pl.multiple_of(j * LANE, LANE), LANE)
        arows = pl.ds(pl.multiple_of(e * ab, SUBLANE), SUBLANE)
        s1t = [s1_ref[h, arows, cols] for h in range(PEER_HEADS)]
        w1t = [w1_ref[h, arows, cols] for h in range(PEER_HEADS)]
        thr = [thr_ref[h, 0:1, cols] for h in range(PEER_HEADS)]
        for al in range(ab):
            rows = slice(al * PEER_NKEYS, (al + 1) * PEER_NKEYS)
            acc = jnp.zeros((PEER_NKEYS, LANE), F32)
            for h in range(PEER_HEADS):
                sel = (s2_ref[h, :, cols] + s1t[h][al:al + 1, :]) >= thr[h]
                acc = acc + jnp.where(sel, e2_ref[h, :, cols] * w1t[h][al:al + 1, :], 0.0)
            at_ref[rows, cols] = (acc * _gelu_tanh(ht_ref[rows, cols])).astype(BF16)
        return carry

    lax.fori_loop(0, nj, mix, 0)

    yt_ref[...] += jnp.dot(vt_ref[...], at_ref[...], preferred_element_type=F32)

    @pl.when(e == ne - 1)
    def _():
        o_ref[0] = x_ref[0] + gt_ref[0] * yt_ref[...].T


def _peer_call(x, sh, sc, gt, ng, wq, k1, k2, u, v, *, tt, eb):
    b, s, d = x.shape
    assert eb == SUBLANE * PEER_NKEYS and u.shape[0] % eb == 0
    ne = u.shape[0] // eb
    ub = u.astype(BF16)
    vtb = v.T.astype(BF16)
    const3 = lambda bi, i, e: (0, 0)
    vec = pl.BlockSpec((1, 1, d), lambda bi, i, e: (bi, 0, 0))
    kern = functools.partial(_peer_kernel, tt=tt, eb=eb)
    sshape = (PEER_HEADS, PEER_NKEYS, tt)
    return pl.pallas_call(
        kern,
        grid=(b, s // tt, ne),
        in_specs=[
            pl.BlockSpec((1, tt, d), lambda bi, i, e: (bi, i, 0)),
            vec, vec, vec,
            pl.BlockSpec((1, d), const3),
            pl.BlockSpec(wq.shape, const3),
            pl.BlockSpec(k1.shape, const3),
            pl.BlockSpec(k2.shape, const3),
            pl.BlockSpec((eb, d), lambda bi, i, e: (e, 0)),
            pl.BlockSpec((d, eb), lambda bi, i, e: (0, e)),
        ],
        out_specs=pl.BlockSpec((1, tt, d), lambda bi, i, e: (bi, i, 0)),
        out_shape=jax.ShapeDtypeStruct(x.shape, F32),
        scratch_shapes=[
            pltpu.VMEM((tt, d), BF16),
            pltpu.VMEM(sshape, F32),
            pltpu.VMEM(sshape, F32),
            pltpu.VMEM(sshape, F32),
            pltpu.VMEM(sshape, F32),
            pltpu.VMEM((PEER_HEADS, SUBLANE, tt), F32),
            pltpu.VMEM((d, tt), F32),
            pltpu.VMEM((eb, tt), F32),
            pltpu.VMEM((eb, tt), BF16),
        ],
        compiler_params=pltpu.CompilerParams(
            dimension_semantics=("arbitrary", "arbitrary", "arbitrary"), vmem_limit_bytes=VMEM_LIMIT),
        name="peer_layer",
    )(x, sh, sc, gt, ng.reshape(1, d), wq.astype(BF16), k1, k2, ub, vtb)


def _final_kernel(x_ref, g_ref, o_ref):
    x = x_ref[0]
    ms = jnp.mean(x * x, axis=-1, keepdims=True)
    o_ref[0] = x * lax.rsqrt(ms + RMS_EPS) * g_ref[...]


def _final_call(x, g, *, tb):
    b, s, d = x.shape
    return pl.pallas_call(
        _final_kernel,
        grid=(b, s // tb),
        in_specs=[pl.BlockSpec((1, tb, d), lambda bi, i: (bi, i, 0)),
                  pl.BlockSpec((1, d), lambda bi, i: (0, 0))],
        out_specs=pl.BlockSpec((1, tb, d), lambda bi, i: (bi, i, 0)),
        out_shape=jax.ShapeDtypeStruct(x.shape, F32),
        name="final_norm",
    )(x, g.reshape(1, d))


def _pick(n, pref):
    return pref if n % pref == 0 else n


def kernel(x, c, ada_w, ada_b, norm_mix_g, norm_ffn_g, gla_w_in, gla_w_g2, gla_b_g2, gla_norm_g,
           gla_w_out, gdn_w_in, gdn_conv_w, gdn_a_log, gdn_dt_bias, gdn_norm_g, gdn_w_out,
           peer_wq, peer_k1, peer_k2, peer_u, peer_v, final_norm_g):
    b, s, d = x.shape
    depth = ada_w.shape[0]
    c_pad = jnp.pad(c, ((0, SUBLANE - b % SUBLANE if b % SUBLANE else 0), (0, 0)))
    mod = _mod_call(c_pad, ada_w, ada_b)[:, :b, :].reshape(depth, b, 6, 1, d)
    tb_mix = _pick(s, 256)
    tt = _pick(s, 512)
    eb = 1024
    for i in range(depth):
        sh1, sc1, gt1, sh2, sc2, gt2 = (mod[i, :, j] for j in range(6))
        j = i // 2
        if i % 2 == 0:
            x = _gla_call(x, sh1, sc1, gt1, norm_mix_g[i], gla_w_in[j], gla_w_g2[j], gla_b_g2[j],
                          gla_norm_g[j], gla_w_out[j], tb=tb_mix)
        else:
            x = _gdn_call(x, sh1, sc1, gt1, norm_mix_g[i], gdn_w_in[j], gdn_conv_w[j], gdn_a_log[j],
                          gdn_dt_bias[j], gdn_norm_g[j], gdn_w_out[j], tb=tb_mix)
        x = _peer_call(x, sh2, sc2, gt2, norm_ffn_g[i], peer_wq[i], peer_k1[i], peer_k2[i],
                       peer_u[i], peer_v[i], tt=tt, eb=eb)
    return _final_call(x, final_norm_g, tb=_pick(s, 512))
```

```python
import functools

import jax
import jax.numpy as jnp
from jax import lax
from jax.experimental import pallas as pl
from jax.experimental.pallas import tpu as pltpu

F32 = jnp.float32
BF16 = jnp.bfloat16
HI = lax.Precision.HIGHEST

RMS_EPS = 1e-6
CHUNK = 64

GLA_HEADS = 4
GLA_GATE_RANK = 16
GLA_TAU = 16.0

GDN_QK_HEADS = 8
GDN_V_HEADS = 16
GDN_HEAD_DIM = 128
GDN_CONV = 4

PEER_HEADS = 8
PEER_TOPK = 16
PEER_NKEYS = 128
PEER_HALF = 128
PEER_PARTS = 4
PEER_MROWS = 256
PEER_NCOLS = 256

LANE = 128
SUBLANE = 8
VMEM_LIMIT = 56 * 1024 * 1024

NT = (((1,), (1,)), ((), ()))


def _sigmoid(x):
    return 1.0 / (1.0 + jnp.exp(-x))


def _silu(x):
    return x * _sigmoid(x)


def _softplus(x):
    return jnp.maximum(x, 0.0) + jnp.log1p(jnp.exp(-jnp.abs(x)))


def _gelu_tanh(x):
    c = 0.7978845608028654
    hx = 0.5 * x
    return hx + hx * jnp.tanh(x * (c + (c * 0.044715) * (x * x)))


def _norm_mod(x, g, sc, sh):
    ms = jnp.mean(x * x, axis=-1, keepdims=True)
    return (x * lax.rsqrt(ms + RMS_EPS) * g) * (1.0 + sc) + sh


def _bdot(a, b):
    return jnp.dot(a.astype(BF16), b.astype(BF16), preferred_element_type=F32)


def _bdot_nt(a, b):
    return lax.dot_general(a.astype(BF16), b.astype(BF16), NT, preferred_element_type=F32)


def _hdot(a, b):
    return jnp.dot(a, b, precision=HI, preferred_element_type=F32)


def _tril(n, k=0):
    r = lax.broadcasted_iota(jnp.int32, (n, n), 0)
    c = lax.broadcasted_iota(jnp.int32, (n, n), 1)
    return (c - r) <= k


def _mod_kernel(c_ref, w_ref, b_ref, o_ref):
    c = c_ref[...]
    o_ref[0] = _hdot(_silu(c), w_ref[0]) + b_ref[0]


def _mod_call(c_pad, ada_w, ada_b):
    depth, d, d6 = ada_w.shape
    nb = d6 // d
    return pl.pallas_call(
        _mod_kernel,
        grid=(depth, nb),
        in_specs=[
            pl.BlockSpec(c_pad.shape, lambda i, j: (0, 0)),
            pl.BlockSpec((1, d, d), lambda i, j: (i, 0, j)),
            pl.BlockSpec((1, 1, d), lambda i, j: (i, 0, j)),
        ],
        out_specs=pl.BlockSpec((1, c_pad.shape[0], d), lambda i, j: (i, 0, j)),
        out_shape=jax.ShapeDtypeStruct((depth, c_pad.shape[0], d6), F32),
        name="adaln_mod",
    )(c_pad, ada_w, ada_b.reshape(depth, 1, d6))


def _gla_kernel(x_ref, sh_ref, sc_ref, gt_ref, ng_ref, wqkvr_ref, wg1_ref, wg2_ref, bg2_ref,
                hg_ref, wout_ref, o_ref, st_ref, p_ref, la_ref, ob_ref, *, tb, dk, dv):
    hk = dk // GLA_HEADS
    hv = dv // GLA_HEADS
    nc = tb // CHUNK

    @pl.when(pl.program_id(1) == 0)
    def _():
        st_ref[...] = jnp.zeros_like(st_ref)

    x = x_ref[0]
    h = _norm_mod(x, ng_ref[...], sc_ref[0], sh_ref[0])
    hb = h.astype(BF16)
    p_ref[...] = jnp.dot(hb, wqkvr_ref[...], preferred_element_type=F32)
    g1 = jnp.dot(hb, wg1_ref[...], preferred_element_type=F32)
    g = jnp.dot(g1.astype(BF16), wg2_ref[...], preferred_element_type=F32) + bg2_ref[...]
    la_ref[...] = (jnp.minimum(g, 0.0) - jnp.log1p(jnp.exp(-jnp.abs(g)))) * (1.0 / GLA_TAU)

    tril = _tril(CHUNK).astype(F32)
    causal = _tril(CHUNK)
    scale = hk ** -0.5

    for c in range(nc):
        rows = slice(c * CHUNK, (c + 1) * CHUNK)
        cum = _hdot(tril, la_ref[rows, :])
        cl = cum[CHUNK - 1:CHUNK, :]
        e_pos = jnp.exp(cum)
        e_neg = jnp.exp(-cum)
        e_end = jnp.exp(cl - cum)
        dec = jnp.exp(cl)
        hr = range(GLA_HEADS)
        ksl = [slice(hd * hk, (hd + 1) * hk) for hd in hr]
        qis = [p_ref[rows, hd * hk:(hd + 1) * hk] * (scale * e_pos[:, ksl[hd]]) for hd in hr]
        kraw = [p_ref[rows, dk + hd * hk:dk + (hd + 1) * hk] for hd in hr]
        vs = [p_ref[rows, 2 * dk + hd * hv:2 * dk + (hd + 1) * hv] for hd in hr]
        attns = [jnp.where(causal, _bdot_nt(qis[hd], kraw[hd] * e_neg[:, ksl[hd]]), 0.0) for hd in hr]
        sts = [st_ref[hd] for hd in hr]
        inter = [_bdot_nt(qis[hd], sts[hd]) for hd in hr]
        upds = [_bdot(vs[hd].T, kraw[hd] * e_end[:, ksl[hd]]) for hd in hr]
        intra = [_bdot(attns[hd], vs[hd]) for hd in hr]
        for hd in hr:
            ob_ref[rows, hd * hv:(hd + 1) * hv] = intra[hd] + inter[hd]
            st_ref[hd] = dec[:, ksl[hd]] * sts[hd] + upds[hd]

    hg = hg_ref[...]
    parts = []
    for hd in range(GLA_HEADS):
        oh = ob_ref[:, hd * hv:(hd + 1) * hv]
        ms = jnp.mean(oh * oh, axis=-1, keepdims=True)
        r = p_ref[:, 2 * dk + dv + hd * hv:2 * dk + dv + (hd + 1) * hv]
        parts.append((oh * lax.rsqrt(ms + RMS_EPS) * hg * _silu(r)).astype(BF16))
    y = jnp.concatenate(parts, axis=-1)
    m = jnp.dot(y, wout_ref[...], preferred_element_type=F32)
    o_ref[0] = x + gt_ref[0] * m


def _gla_call(x, sh, sc, gt, ng, w_in, w_g2, b_g2, head_g, w_out, *, tb):
    b, s, d = x.shape
    dk = w_g2.shape[1]
    dv = w_out.shape[0]
    wqkvr = w_in[:, :2 * dk + 2 * dv].astype(BF16)
    wg1 = jnp.pad(w_in[:, 2 * dk + 2 * dv:], ((0, 0), (0, LANE - GLA_GATE_RANK))).astype(BF16)
    wg2 = jnp.pad(w_g2, ((0, LANE - GLA_GATE_RANK), (0, 0))).astype(BF16)
    hv = dv // GLA_HEADS
    hk = dk // GLA_HEADS
    const2 = lambda bi, i: (0, 0)
    vec = pl.BlockSpec((1, 1, d), lambda bi, i: (bi, 0, 0))
    kern = functools.partial(_gla_kernel, tb=tb, dk=dk, dv=dv)
    return pl.pallas_call(
        kern,
        grid=(b, s // tb),
        in_specs=[
            pl.BlockSpec((1, tb, d), lambda bi, i: (bi, i, 0)),
            vec, vec, vec,
            pl.BlockSpec((1, d), const2),
            pl.BlockSpec(wqkvr.shape, const2),
            pl.BlockSpec(wg1.shape, const2),
            pl.BlockSpec(wg2.shape, const2),
            pl.BlockSpec((1, dk), const2),
            pl.BlockSpec((1, hv), const2),
            pl.BlockSpec((dv, d), const2),
        ],
        out_specs=pl.BlockSpec((1, tb, d), lambda bi, i: (bi, i, 0)),
        out_shape=jax.ShapeDtypeStruct(x.shape, F32),
        scratch_shapes=[
            pltpu.VMEM((GLA_HEADS, hv, hk), F32),
            pltpu.VMEM((tb, 2 * dk + 2 * dv), F32),
            pltpu.VMEM((tb, dk), F32),
            pltpu.VMEM((tb, dv), F32),
        ],
        compiler_params=pltpu.CompilerParams(
            dimension_semantics=("arbitrary", "arbitrary"), vmem_limit_bytes=VMEM_LIMIT),
        name="gla_layer",
    )(x, sh, sc, gt, ng.reshape(1, d), wqkvr, wg1, wg2, b_g2.reshape(1, dk),
      head_g.reshape(1, hv), w_out.astype(BF16))


def _block_inverse_masks(n):
    r = lax.broadcasted_iota(jnp.int32, (n, n), 0)
    c = lax.broadcasted_iota(jnp.int32, (n, n), 1)
    eye = (r == c).astype(F32)
    first = (r // 2) == (c // 2)
    quads = []
    s = 2
    while s < n:
        quads.append(((r // (2 * s)) == (c // (2 * s))) & ((r // s) != (c // s)))
        s *= 2
    return eye, first, quads


def _unit_lower_inverse(a_list, masks):
    eye, first, quads = masks
    ds = [eye - jnp.where(first, a, 0.0) for a in a_list]
    for quad in quads:
        ms = [_bdot(d, jnp.where(quad, a, 0.0)) for d, a in zip(ds, a_list)]
        ds = [d - _bdot(m, d) for d, m in zip(ds, ms)]
    return ds


def _gdn_kernel(x_ref, sh_ref, sc_ref, gt_ref, ng_ref, wqkv_ref, wz_ref, wa_ref, wb_ref, cw_ref,
                alog_ref, dtb_ref, hg_ref, wout_ref, o_ref,
                s_ref, cb_ref, q_ref, k_ref, v_ref, z_ref, g_ref, bt_ref, ob_ref, *, tb):
    dqk = GDN_QK_HEADS * GDN_HEAD_DIM
    dvv = GDN_V_HEADS * GDN_HEAD_DIM
    hd_ = GDN_HEAD_DIM
    nc = tb // CHUNK
    rep = GDN_V_HEADS // GDN_QK_HEADS

    @pl.when(pl.program_id(1) == 0)
    def _():
        s_ref[...] = jnp.zeros_like(s_ref)
        cb_ref[0:SUBLANE, :] = jnp.zeros((SUBLANE, cb_ref.shape[1]), F32)

    x = x_ref[0]
    h = _norm_mod(x, ng_ref[...], sc_ref[0], sh_ref[0])
    hb = h.astype(BF16)
    cb_ref[SUBLANE:SUBLANE + tb, :] = jnp.dot(hb, wqkv_ref[...], preferred_element_type=F32)
    z_ref[...] = jnp.dot(hb, wz_ref[...], preferred_element_type=F32)
    a = jnp.dot(hb, wa_ref[...], preferred_element_type=F32)
    bb = jnp.dot(hb, wb_ref[...], preferred_element_type=F32)
    g_ref[...] = -jnp.exp(alog_ref[...]) * _softplus(a + dtb_ref[...])
    bt_ref[...] = _sigmoid(bb)

    off = SUBLANE - (GDN_CONV - 1)
    conv = cw_ref[0:1, :] * cb_ref[off:off + tb, :]
    for j in range(1, GDN_CONV):
        conv = conv + cw_ref[j:j + 1, :] * cb_ref[off + j:off + j + tb, :]
    cb_ref[0:SUBLANE, :] = cb_ref[tb:tb + SUBLANE, :]
    qkv = _silu(conv)
    qscale = hd_ ** -0.5
    for hq in range(GDN_QK_HEADS):
        sl = slice(hq * hd_, (hq + 1) * hd_)
        qh = qkv[:, hq * hd_:(hq + 1) * hd_]
        kh = qkv[:, dqk + hq * hd_:dqk + (hq + 1) * hd_]
        q_ref[:, sl] = qh * (lax.rsqrt(jnp.sum(qh * qh, axis=-1, keepdims=True) + 1e-6) * qscale)
        k_ref[:, sl] = kh * lax.rsqrt(jnp.sum(kh * kh, axis=-1, keepdims=True) + 1e-6)
    v_ref[...] = qkv[:, 2 * dqk:]

    tril = _tril(CHUNK).astype(F32)
    lower = _tril(CHUNK)
    strict = _tril(CHUNK, -1)
    inv_masks = _block_inverse_masks(CHUNK)

    for c in range(nc):
        rows = slice(c * CHUNK, (c + 1) * CHUNK)
        gc = _hdot(tril, g_ref[rows, :])
        gct = gc.T
        gl = gc[CHUNK - 1:CHUNK, :]
        eg = jnp.exp(gc)
        eendt = jnp.exp(gl - gc).T
        dec = jnp.exp(gl)
        bt = bt_ref[rows, :]
        beg = bt * eg
        hq_r = range(GDN_QK_HEADS)
        hv_r = range(GDN_V_HEADS)
        qs = [q_ref[rows, hq * hd_:(hq + 1) * hd_] for hq in hq_r]
        ks = [k_ref[rows, hq * hd_:(hq + 1) * hd_] for hq in hq_r]
        kks = [_bdot_nt(ks[hq], ks[hq]) for hq in hq_r]
        qk0s = [_bdot_nt(qs[hq], ks[hq]) for hq in hq_r]
        khts = [ks[hq].T for hq in hq_r]
        cols = [slice(hvi, hvi + 1) for hvi in hv_r]
        decays = []
        for hvi in hv_r:
            diff = gc[:, cols[hvi]] - gct[hvi:hvi + 1, :]
            decays.append(jnp.where(lower, jnp.exp(jnp.where(lower, diff, 0.0)), 0.0))
        amats = [jnp.where(strict, bt[:, cols[hvi]] * kks[hvi // rep] * decays[hvi], 0.0) for hvi in hv_r]
        tinvs = _unit_lower_inverse(amats, inv_masks)
        rhs = [jnp.concatenate([bt[:, cols[hvi]] * v_ref[rows, hvi * hd_:(hvi + 1) * hd_],
                                beg[:, cols[hvi]] * ks[hvi // rep]], axis=1) for hvi in hv_r]
        uws = [_bdot(tinvs[hvi], rhs[hvi]) for hvi in hv_r]
        ss = [s_ref[hvi] for hvi in hv_r]
        wss = [_bdot(uws[hvi][:, hd_:], ss[hvi]) for hvi in hv_r]
        qss = [_bdot(qs[hvi // rep] * eg[:, cols[hvi]], ss[hvi]) for hvi in hv_r]
        vns = [uws[hvi][:, :hd_] - wss[hvi] for hvi in hv_r]
        outs = [qss[hvi] + _bdot(qk0s[hvi // rep] * decays[hvi], vns[hvi]) for hvi in hv_r]
        upds = [_bdot(khts[hvi // rep] * eendt[hvi:hvi + 1, :], vns[hvi]) for hvi in hv_r]
        for hvi in hv_r:
            ob_ref[rows, hvi * hd_:(hvi + 1) * hd_] = outs[hvi]
            s_ref[hvi] = dec[:, cols[hvi]] * ss[hvi] + upds[hvi]

    hg = hg_ref[...]
    parts = []
    for hvi in range(GDN_V_HEADS):
        vs = slice(hvi * hd_, (hvi + 1) * hd_)
        oh = ob_ref[:, vs]
        ms = jnp.mean(oh * oh, axis=-1, keepdims=True)
        parts.append((oh * lax.rsqrt(ms + RMS_EPS) * hg * _silu(z_ref[:, vs])).astype(BF16))
    y = jnp.concatenate(parts, axis=-1)
    m = jnp.dot(y, wout_ref[...], preferred_element_type=F32)
    o_ref[0] = x + gt_ref[0] * m


def _gdn_call(x, sh, sc, gt, ng, w_in, conv_w, a_log, dt_bias, head_g, w_out, *, tb):
    b, s, d = x.shape
    dqk = GDN_QK_HEADS * GDN_HEAD_DIM
    dvv = GDN_V_HEADS * GDN_HEAD_DIM
    nqkv = 2 * dqk + dvv
    wqkv = w_in[:, :nqkv].astype(BF16)
    wz = w_in[:, nqkv:nqkv + dvv].astype(BF16)
    padh = ((0, 0), (0, LANE - GDN_V_HEADS))
    wa = jnp.pad(w_in[:, nqkv + dvv:nqkv + dvv + GDN_V_HEADS], padh).astype(BF16)
    wb = jnp.pad(w_in[:, nqkv + dvv + GDN_V_HEADS:], padh).astype(BF16)
    alog = jnp.pad(a_log.reshape(1, -1), padh)
    dtb = jnp.pad(dt_bias.reshape(1, -1), padh)
    const2 = lambda bi, i: (0, 0)
    vec = pl.BlockSpec((1, 1, d), lambda bi, i: (bi, 0, 0))
    kern = functools.partial(_gdn_kernel, tb=tb)
    return pl.pallas_call(
        kern,
        grid=(b, s // tb),
        in_specs=[
            pl.BlockSpec((1, tb, d), lambda bi, i: (bi, i, 0)),
            vec, vec, vec,
            pl.BlockSpec((1, d), const2),
            pl.BlockSpec(wqkv.shape, const2),
            pl.BlockSpec(wz.shape, const2),
            pl.BlockSpec(wa.shape, const2),
            pl.BlockSpec(wb.shape, const2),
            pl.BlockSpec(conv_w.shape, const2),
            pl.BlockSpec((1, LANE), const2),
            pl.BlockSpec((1, LANE), const2),
            pl.BlockSpec((1, GDN_HEAD_DIM), const2),
            pl.BlockSpec((dvv, d), const2),
        ],
        out_specs=pl.BlockSpec((1, tb, d), lambda bi, i: (bi, i, 0)),
        out_shape=jax.ShapeDtypeStruct(x.shape, F32),
        scratch_shapes=[
            pltpu.VMEM((GDN_V_HEADS, GDN_HEAD_DIM, GDN_HEAD_DIM), F32),
            pltpu.VMEM((tb + SUBLANE, nqkv), F32),
            pltpu.VMEM((tb, dqk), F32),
            pltpu.VMEM((tb, dqk), F32),
            pltpu.VMEM((tb, dvv), F32),
            pltpu.VMEM((tb, dvv), F32),
            pltpu.VMEM((tb, LANE), F32),
            pltpu.VMEM((tb, LANE), F32),
            pltpu.VMEM((tb, dvv), F32),
        ],
        compiler_params=pltpu.CompilerParams(
            dimension_semantics=("arbitrary", "arbitrary"), vmem_limit_bytes=VMEM_LIMIT),
        name="gdn_layer",
    )(x, sh, sc, gt, ng.reshape(1, d), wqkv, wz, wa, wb, conv_w, alog, dtb,
      head_g.reshape(1, GDN_HEAD_DIM), w_out.astype(BF16))


def _top_values(x, n):
    vals = []
    for r in range(n):
        m = jnp.max(x, axis=0, keepdims=True)
        vals.append(m)
        if r + 1 < n:
            x = jnp.where(x == m, -jnp.inf, x)
    return vals


def _stack_rows(rows):
    l = rows[0].shape[1]
    sub = lax.broadcasted_iota(jnp.int32, (SUBLANE, l), 0)
    out = jnp.broadcast_to(rows[0], (SUBLANE, l))
    for i in range(1, len(rows)):
        out = jnp.where(sub == i, rows[i], out)
    return out


def _peer_kernel(x_ref, sh_ref, sc_ref, gt_ref, ng_ref, wq_ref, k1_ref, k2_ref, u_ref, vt_ref, o_ref,
                 tb_ref, s1_ref, w1_ref, s2_ref, e2_ref, c2_ref, yt_ref, *part_refs, tt, eb, ne):
    g = pl.program_id(2)
    ht_refs = part_refs[:PEER_PARTS]
    at_refs = part_refs[PEER_PARTS:]
    nj = tt // LANE
    ab = eb // PEER_NKEYS
    k = PEER_TOPK

    @pl.when(g == 0)
    def _():
        x = x_ref[0]
        t = _norm_mod(x, ng_ref[...], sc_ref[0], sh_ref[0])
        tb = t.astype(BF16)
        tb_ref[...] = tb
        q = jnp.dot(tb, wq_ref[...], preferred_element_type=F32)
        for h in range(PEER_HEADS):
            q1 = q[:, (2 * h) * PEER_HALF:(2 * h + 1) * PEER_HALF]
            q2 = q[:, (2 * h + 1) * PEER_HALF:(2 * h + 2) * PEER_HALF]
            s1_ref[h] = lax.dot_general(k1_ref[...], q1, NT, precision=HI, preferred_element_type=F32)
            s2_ref[h] = lax.dot_general(k2_ref[...], q2, NT, precision=HI, preferred_element_type=F32)
        yt_ref[...] = jnp.zeros_like(yt_ref)
        for ht_ref in ht_refs:
            ht_ref[...] = jnp.zeros_like(ht_ref)

        def route(idx, carry):
            h = idx // nj
            c0 = pl.multiple_of((idx % nj) * LANE, LANE)
            cols = pl.ds(c0, LANE)
            x1 = s1_ref[h, :, cols]
            x2 = s2_ref[h, :, cols]
            v1 = _top_values(x1, k + 1)
            v2 = _top_values(x2, k + 1)
            v2lo = _stack_rows(v2[:SUBLANE])
            v2hi = _stack_rows(v2[SUBLANE:2 * SUBLANE])
            v1hi = _stack_rows(v1[SUBLANE:2 * SUBLANE])
            sub = lax.broadcasted_iota(jnp.int32, (SUBLANE, LANE), 0)
            last = jnp.where(sub == 0, v1[0] + v2[k], jnp.where(sub == 1, v1[k] + v2[0], -jnp.inf))
            tiles = [v1[0] + v2lo, v1[0] + v2hi, v1hi + v2[0], last]
            tiles += [v1[i] + v2lo for i in range(1, SUBLANE)]
            cand = jnp.concatenate(tiles, axis=0)
            tops = _top_values(cand, k + 1)
            kth = tops[k - 1]
            cut = 0.5 * (kth + tops[k])
            top = v1[0] + v2[0]
            z = jnp.sum(jnp.where(cand >= kth, jnp.exp(cand - top), 0.0), axis=0, keepdims=True)
            w1_ref[h, :, cols] = jnp.exp(x1 - v1[0]) / z
            e2_ref[h, :, cols] = jnp.exp(x2 - v2[0])
            c2_ref[h, :, cols] = cut - x2
            return carry

        lax.fori_loop(0, PEER_HEADS * nj, route, 0)

    em = jnp.clip(g - 1, 0, ne - 1)
    arows = pl.ds(pl.multiple_of(em * ab, SUBLANE), SUBLANE)
    part = eb // PEER_PARTS
    bh = PEER_NKEYS // 2
    d = yt_ref.shape[0]

    def gate_block(p, j, hb):
        als = list(range(p * part // PEER_NKEYS, (p + 1) * part // PEER_NKEYS))
        cols = slice(j * LANE, (j + 1) * LANE)
        brows = slice(hb * bh, (hb + 1) * bh)
        s1t = [s1_ref[h, arows, cols] for h in range(PEER_HEADS)]
        w1t = [w1_ref[h, arows, cols] for h in range(PEER_HEADS)]
        accs = [jnp.zeros((bh, LANE), F32) for _ in als]
        for h in range(PEER_HEADS):
            c2 = c2_ref[h, brows, cols]
            e2 = e2_ref[h, brows, cols]
            for i, al in enumerate(als):
                sel = s1t[h][al:al + 1, :] >= c2
                accs[i] = accs[i] + jnp.where(sel, e2 * w1t[h][al:al + 1, :], 0.0)
        fold = None
        for i in range(len(als)):
            rows = slice(i * PEER_NKEYS + hb * bh, i * PEER_NKEYS + (hb + 1) * bh)
            out = accs[i] * _gelu_tanh(ht_refs[p][rows, cols])
            at_refs[p][rows, cols] = out.astype(BF16)
            bits = lax.bitcast_convert_type(out, jnp.int32)
            fold = bits if fold is None else fold | bits
        while fold.shape[0] > SUBLANE:
            half = fold.shape[0] // 2
            fold = fold[:half] | fold[half:]
        return fold[0, 0] & jnp.minimum(g, 0)

    def matmul_pieces(p):
        prow = slice(p * part, (p + 1) * part)
        pieces = []
        for m in range(d // PEER_MROWS):

            def second(zero, m=m):
                mrows = pl.ds(pl.multiple_of(zero + m * PEER_MROWS, PEER_MROWS), PEER_MROWS)
                yt_ref[mrows, :] += jnp.dot(vt_ref[mrows, prow], at_refs[p][...], preferred_element_type=F32)
            pieces.append(second)
        for n in range(tt // PEER_NCOLS):
            ncols = slice(n * PEER_NCOLS, (n + 1) * PEER_NCOLS)

            def first(zero, ncols=ncols):
                urows = pl.ds(pl.multiple_of(zero + p * part, part), part)
                ht_refs[p][:, ncols] = lax.dot_general(u_ref[urows, :], tb_ref[ncols, :], NT,
                                                       preferred_element_type=F32)
            pieces.append(first)
        return pieces

    pending = []
    for p in range(PEER_PARTS):
        for j in range(nj):
            for hb in range(2):
                zero = gate_block(p, j, hb)
                if pending:
                    pending.pop(0)(zero)
        for piece in pending:
            piece(zero)
        pending = matmul_pieces(p)
    for piece in pending:
        piece(zero)

    @pl.when(g == ne)
    def _():
        o_ref[0] = x_ref[0] + gt_ref[0] * yt_ref[...].T


def _peer_call(x, sh, sc, gt, ng, wq, k1, k2, u, v, *, tt, eb):
    b, s, d = x.shape
    assert eb == SUBLANE * PEER_NKEYS and u.shape[0] % eb == 0
    ne = u.shape[0] // eb
    ub = u.astype(BF16)
    vtb = v.T.astype(BF16)
    const3 = lambda bi, i, e: (0, 0)
    vec = pl.BlockSpec((1, 1, d), lambda bi, i, e: (bi, 0, 0))
    kern = functools.partial(_peer_kernel, tt=tt, eb=eb, ne=ne)
    sshape = (PEER_HEADS, PEER_NKEYS, tt)
    return pl.pallas_call(
        kern,
        grid=(b, s // tt, ne + 1),
        in_specs=[
            pl.BlockSpec((1, tt, d), lambda bi, i, e: (bi, i, 0)),
            vec, vec, vec,
            pl.BlockSpec((1, d), const3),
            pl.BlockSpec(wq.shape, const3),
            pl.BlockSpec(k1.shape, const3),
            pl.BlockSpec(k2.shape, const3),
            pl.BlockSpec((eb, d), lambda bi, i, g: (jnp.minimum(g, ne - 1), 0)),
            pl.BlockSpec((d, eb), lambda bi, i, g: (0, jnp.maximum(g - 1, 0))),
        ],
        out_specs=pl.BlockSpec((1, tt, d), lambda bi, i, e: (bi, i, 0)),
        out_shape=jax.ShapeDtypeStruct(x.shape, F32),
        scratch_shapes=[
            pltpu.VMEM((tt, d), BF16),
            pltpu.VMEM(sshape, F32),
            pltpu.VMEM(sshape, F32),
            pltpu.VMEM(sshape, F32),
            pltpu.VMEM(sshape, F32),
            pltpu.VMEM(sshape, F32),
            pltpu.VMEM((d, tt), F32),
        ] + [pltpu.VMEM((eb // PEER_PARTS, tt), F32)] * PEER_PARTS
          + [pltpu.VMEM((eb // PEER_PARTS, tt), BF16)] * PEER_PARTS,
        compiler_params=pltpu.CompilerParams(
            dimension_semantics=("arbitrary", "arbitrary", "arbitrary"), vmem_limit_bytes=VMEM_LIMIT),
        name="peer_layer",
    )(x, sh, sc, gt, ng.reshape(1, d), wq.astype(BF16), k1, k2, ub, vtb)


def _final_kernel(x_ref, g_ref, o_ref):
    x = x_ref[0]
    ms = jnp.mean(x * x, axis=-1, keepdims=True)
    o_ref[0] = x * lax.rsqrt(ms + RMS_EPS) * g_ref[...]


def _final_call(x, g, *, tb):
    b, s, d = x.shape
    return pl.pallas_call(
        _final_kernel,
        grid=(b, s // tb),
        in_specs=[pl.BlockSpec((1, tb, d), lambda bi, i: (bi, i, 0)),
                  pl.BlockSpec((1, d), lambda bi, i: (0, 0))],
        out_specs=pl.BlockSpec((1, tb, d), lambda bi, i: (bi, i, 0)),
        out_shape=jax.ShapeDtypeStruct(x.shape, F32),
        name="final_norm",
    )(x, g.reshape(1, d))


def _pick(n, pref):
    return pref if n % pref == 0 else n


def kernel(x, c, ada_w, ada_b, norm_mix_g, norm_ffn_g, gla_w_in, gla_w_g2, gla_b_g2, gla_norm_g,
           gla_w_out, gdn_w_in, gdn_conv_w, gdn_a_log, gdn_dt_bias, gdn_norm_g, gdn_w_out,
           peer_wq, peer_k1, peer_k2, peer_u, peer_v, final_norm_g):
    b, s, d = x.shape
    depth = ada_w.shape[0]
    c_pad = jnp.pad(c, ((0, SUBLANE - b % SUBLANE if b % SUBLANE else 0), (0, 0)))
    mod = _mod_call(c_pad, ada_w, ada_b)[:, :b, :].reshape(depth, b, 6, 1, d)
    tb_mix = _pick(s, 256)
    tt = _pick(s, 512)
    eb = 1024
    for i in range(depth):
        sh1, sc1, gt1, sh2, sc2, gt2 = (mod[i, :, j] for j in range(6))
        j = i // 2
        if i % 2 == 0:
            x = _gla_call(x, sh1, sc1, gt1, norm_mix_g[i], gla_w_in[j], gla_w_g2[j], gla_b_g2[j],
                          gla_norm_g[j], gla_w_out[j], tb=tb_mix)
        else:
            x = _gdn_call(x, sh1, sc1, gt1, norm_mix_g[i], gdn_w_in[j], gdn_conv_w[j], gdn_a_log[j],
                          gdn_dt_bias[j], gdn_norm_g[j], gdn_w_out[j], tb=tb_mix)
        x = _peer_call(x, sh2, sc2, gt2, norm_ffn_g[i], peer_wq[i], peer_k1[i], peer_k2[i],
                       peer_u[i], peer_v[i], tt=tt, eb=eb)
    return _final_call(x, final_norm_g, tb=_pick(s, 512))
```

```python
import functools

import jax
import jax.numpy as jnp
from jax import lax
from jax.experimental import pallas as pl
from jax.experimental.pallas import tpu as pltpu

F32 = jnp.float32
BF16 = jnp.bfloat16
HI = lax.Precision.HIGHEST

RMS_EPS = 1e-6
CHUNK = 64

GLA_HEADS = 4
GLA_GATE_RANK = 16
GLA_TAU = 16.0

GDN_QK_HEADS = 8
GDN_V_HEADS = 16
GDN_HEAD_DIM = 128
GDN_CONV = 4

PEER_HEADS = 8
PEER_TOPK = 16
PEER_NKEYS = 128
PEER_HALF = 128
PEER_NCOLS = 256
PEER_MROWS = 512
PEER_GATE_PER_PIECE = 4

LANE = 128
SUBLANE = 8
VMEM_LIMIT = 56 * 1024 * 1024

NT = (((1,), (1,)), ((), ()))


def _sigmoid(x):
    return 1.0 / (1.0 + jnp.exp(-x))


def _silu(x):
    return x * _sigmoid(x)


def _softplus(x):
    return jnp.maximum(x, 0.0) + jnp.log1p(jnp.exp(-jnp.abs(x)))


def _gelu_tanh(x):
    c = 0.7978845608028654
    hx = 0.5 * x
    return hx + hx * jnp.tanh(x * (c + (c * 0.044715) * (x * x)))


def _norm_mod(x, g, sc, sh):
    ms = jnp.mean(x * x, axis=-1, keepdims=True)
    return (x * lax.rsqrt(ms + RMS_EPS) * g) * (1.0 + sc) + sh


def _bdot(a, b):
    return jnp.dot(a.astype(BF16), b.astype(BF16), preferred_element_type=F32)


def _bdot_nt(a, b):
    return lax.dot_general(a.astype(BF16), b.astype(BF16), NT, preferred_element_type=F32)


def _hdot(a, b):
    return jnp.dot(a, b, precision=HI, preferred_element_type=F32)


def _tril(n, k=0):
    r = lax.broadcasted_iota(jnp.int32, (n, n), 0)
    c = lax.broadcasted_iota(jnp.int32, (n, n), 1)
    return (c - r) <= k


def _mod_kernel(c_ref, w_ref, b_ref, o_ref):
    c = c_ref[...]
    o_ref[0] = _hdot(_silu(c), w_ref[0]) + b_ref[0]


def _mod_call(c_pad, ada_w, ada_b):
    depth, d, d6 = ada_w.shape
    nb = d6 // d
    return pl.pallas_call(
        _mod_kernel,
        grid=(depth, nb),
        in_specs=[
            pl.BlockSpec(c_pad.shape, lambda i, j: (0, 0)),
            pl.BlockSpec((1, d, d), lambda i, j: (i, 0, j)),
            pl.BlockSpec((1, 1, d), lambda i, j: (i, 0, j)),
        ],
        out_specs=pl.BlockSpec((1, c_pad.shape[0], d), lambda i, j: (i, 0, j)),
        out_shape=jax.ShapeDtypeStruct((depth, c_pad.shape[0], d6), F32),
        name="adaln_mod",
    )(c_pad, ada_w, ada_b.reshape(depth, 1, d6))


def _gla_kernel(x_ref, sh_ref, sc_ref, gt_ref, ng_ref, wqkvr_ref, wg1_ref, wg2_ref, bg2_ref,
                hg_ref, wout_ref, o_ref, st_ref, p_ref, la_ref, ob_ref, *, tb, dk, dv):
    hk = dk // GLA_HEADS
    hv = dv // GLA_HEADS
    nc = tb // CHUNK

    @pl.when(pl.program_id(1) == 0)
    def _():
        st_ref[...] = jnp.zeros_like(st_ref)

    x = x_ref[0]
    h = _norm_mod(x, ng_ref[...], sc_ref[0], sh_ref[0])
    hb = h.astype(BF16)
    p_ref[...] = jnp.dot(hb, wqkvr_ref[...], preferred_element_type=F32)
    g1 = jnp.dot(hb, wg1_ref[...], preferred_element_type=F32)
    g = jnp.dot(g1.astype(BF16), wg2_ref[...], preferred_element_type=F32) + bg2_ref[...]
    la_ref[...] = (jnp.minimum(g, 0.0) - jnp.log1p(jnp.exp(-jnp.abs(g)))) * (1.0 / GLA_TAU)

    tril = _tril(CHUNK).astype(F32)
    causal = _tril(CHUNK)
    scale = hk ** -0.5

    for c in range(nc):
        rows = slice(c * CHUNK, (c + 1) * CHUNK)
        cum = _hdot(tril, la_ref[rows, :])
        cl = cum[CHUNK - 1:CHUNK, :]
        e_pos = jnp.exp(cum)
        e_neg = jnp.exp(-cum)
        e_end = jnp.exp(cl - cum)
        dec = jnp.exp(cl)
        hr = range(GLA_HEADS)
        ksl = [slice(hd * hk, (hd + 1) * hk) for hd in hr]
        qis = [p_ref[rows, hd * hk:(hd + 1) * hk] * (scale * e_pos[:, ksl[hd]]) for hd in hr]
        kraw = [p_ref[rows, dk + hd * hk:dk + (hd + 1) * hk] for hd in hr]
        vs = [p_ref[rows, 2 * dk + hd * hv:2 * dk + (hd + 1) * hv] for hd in hr]
        attns = [jnp.where(causal, _bdot_nt(qis[hd], kraw[hd] * e_neg[:, ksl[hd]]), 0.0) for hd in hr]
        sts = [st_ref[hd] for hd in hr]
        inter = [_bdot_nt(qis[hd], sts[hd]) for hd in hr]
        upds = [_bdot(vs[hd].T, kraw[hd] * e_end[:, ksl[hd]]) for hd in hr]
        intra = [_bdot(attns[hd], vs[hd]) for hd in hr]
        for hd in hr:
            ob_ref[rows, hd * hv:(hd + 1) * hv] = intra[hd] + inter[hd]
            st_ref[hd] = dec[:, ksl[hd]] * sts[hd] + upds[hd]

    hg = hg_ref[...]
    parts = []
    for hd in range(GLA_HEADS):
        oh = ob_ref[:, hd * hv:(hd + 1) * hv]
        ms = jnp.mean(oh * oh, axis=-1, keepdims=True)
        r = p_ref[:, 2 * dk + dv + hd * hv:2 * dk + dv + (hd + 1) * hv]
        parts.append((oh * lax.rsqrt(ms + RMS_EPS) * hg * _silu(r)).astype(BF16))
    y = jnp.concatenate(parts, axis=-1)
    m = jnp.dot(y, wout_ref[...], preferred_element_type=F32)
    o_ref[0] = x + gt_ref[0] * m


def _gla_call(x, sh, sc, gt, ng, w_in, w_g2, b_g2, head_g, w_out, *, tb):
    b, s, d = x.shape
    dk = w_g2.shape[1]
    dv = w_out.shape[0]
    wqkvr = w_in[:, :2 * dk + 2 * dv].astype(BF16)
    wg1 = jnp.pad(w_in[:, 2 * dk + 2 * dv:], ((0, 0), (0, LANE - GLA_GATE_RANK))).astype(BF16)
    wg2 = jnp.pad(w_g2, ((0, LANE - GLA_GATE_RANK), (0, 0))).astype(BF16)
    hv = dv // GLA_HEADS
    hk = dk // GLA_HEADS
    const2 = lambda bi, i: (0, 0)
    vec = pl.BlockSpec((1, 1, d), lambda bi, i: (bi, 0, 0))
    kern = functools.partial(_gla_kernel, tb=tb, dk=dk, dv=dv)
    return pl.pallas_call(
        kern,
        grid=(b, s // tb),
        in_specs=[
            pl.BlockSpec((1, tb, d), lambda bi, i: (bi, i, 0)),
            vec, vec, vec,
            pl.BlockSpec((1, d), const2),
            pl.BlockSpec(wqkvr.shape, const2),
            pl.BlockSpec(wg1.shape, const2),
            pl.BlockSpec(wg2.shape, const2),
            pl.BlockSpec((1, dk), const2),
            pl.BlockSpec((1, hv), const2),
            pl.BlockSpec((dv, d), const2),
        ],
        out_specs=pl.BlockSpec((1, tb, d), lambda bi, i: (bi, i, 0)),
        out_shape=jax.ShapeDtypeStruct(x.shape, F32),
        scratch_shapes=[
            pltpu.VMEM((GLA_HEADS, hv, hk), F32),
            pltpu.VMEM((tb, 2 * dk + 2 * dv), F32),
            pltpu.VMEM((tb, dk), F32),
            pltpu.VMEM((tb, dv), F32),
        ],
        compiler_params=pltpu.CompilerParams(
            dimension_semantics=("arbitrary", "arbitrary"), vmem_limit_bytes=VMEM_LIMIT),
        name="gla_layer",
    )(x, sh, sc, gt, ng.reshape(1, d), wqkvr, wg1, wg2, b_g2.reshape(1, dk),
      head_g.reshape(1, hv), w_out.astype(BF16))


def _block_inverse_masks(n):
    r = lax.broadcasted_iota(jnp.int32, (n, n), 0)
    c = lax.broadcasted_iota(jnp.int32, (n, n), 1)
    eye = (r == c).astype(F32)
    first = (r // 2) == (c // 2)
    quads = []
    s = 2
    while s < n:
        quads.append(((r // (2 * s)) == (c // (2 * s))) & ((r // s) != (c // s)))
        s *= 2
    return eye, first, quads


def _unit_lower_inverse(a_list, masks):
    eye, first, quads = masks
    ds = [eye - jnp.where(first, a, 0.0) for a in a_list]
    for quad in quads:
        ms = [_bdot(d, jnp.where(quad, a, 0.0)) for d, a in zip(ds, a_list)]
        ds = [d - _bdot(m, d) for d, m in zip(ds, ms)]
    return ds


def _gdn_kernel(x_ref, sh_ref, sc_ref, gt_ref, ng_ref, wqkv_ref, wz_ref, wa_ref, wb_ref, cw_ref,
                alog_ref, dtb_ref, hg_ref, wout_ref, o_ref,
                s_ref, cb_ref, q_ref, k_ref, v_ref, z_ref, g_ref, bt_ref, ob_ref, *, tb):
    dqk = GDN_QK_HEADS * GDN_HEAD_DIM
    dvv = GDN_V_HEADS * GDN_HEAD_DIM
    hd_ = GDN_HEAD_DIM
    nc = tb // CHUNK
    rep = GDN_V_HEADS // GDN_QK_HEADS

    @pl.when(pl.program_id(1) == 0)
    def _():
        s_ref[...] = jnp.zeros_like(s_ref)
        cb_ref[0:SUBLANE, :] = jnp.zeros((SUBLANE, cb_ref.shape[1]), F32)

    x = x_ref[0]
    h = _norm_mod(x, ng_ref[...], sc_ref[0], sh_ref[0])
    hb = h.astype(BF16)
    cb_ref[SUBLANE:SUBLANE + tb, :] = jnp.dot(hb, wqkv_ref[...], preferred_element_type=F32)
    z_ref[...] = jnp.dot(hb, wz_ref[...], preferred_element_type=F32)
    a = jnp.dot(hb, wa_ref[...], preferred_element_type=F32)
    bb = jnp.dot(hb, wb_ref[...], preferred_element_type=F32)
    g_ref[...] = -jnp.exp(alog_ref[...]) * _softplus(a + dtb_ref[...])
    bt_ref[...] = _sigmoid(bb)

    off = SUBLANE - (GDN_CONV - 1)
    conv = cw_ref[0:1, :] * cb_ref[off:off + tb, :]
    for j in range(1, GDN_CONV):
        conv = conv + cw_ref[j:j + 1, :] * cb_ref[off + j:off + j + tb, :]
    cb_ref[0:SUBLANE, :] = cb_ref[tb:tb + SUBLANE, :]
    qkv = _silu(conv)
    qscale = hd_ ** -0.5
    for hq in range(GDN_QK_HEADS):
        sl = slice(hq * hd_, (hq + 1) * hd_)
        qh = qkv[:, hq * hd_:(hq + 1) * hd_]
        kh = qkv[:, dqk + hq * hd_:dqk + (hq + 1) * hd_]
        q_ref[:, sl] = qh * (lax.rsqrt(jnp.sum(qh * qh, axis=-1, keepdims=True) + 1e-6) * qscale)
        k_ref[:, sl] = kh * lax.rsqrt(jnp.sum(kh * kh, axis=-1, keepdims=True) + 1e-6)
    v_ref[...] = qkv[:, 2 * dqk:]

    tril = _tril(CHUNK).astype(F32)
    lower = _tril(CHUNK)
    strict = _tril(CHUNK, -1)
    inv_masks = _block_inverse_masks(CHUNK)

    for c in range(nc):
        rows = slice(c * CHUNK, (c + 1) * CHUNK)
        gc = _hdot(tril, g_ref[rows, :])
        gct = gc.T
        gl = gc[CHUNK - 1:CHUNK, :]
        eg = jnp.exp(gc)
        eendt = jnp.exp(gl - gc).T
        dec = jnp.exp(gl)
        bt = bt_ref[rows, :]
        beg = bt * eg
        hq_r = range(GDN_QK_HEADS)
        hv_r = range(GDN_V_HEADS)
        qs = [q_ref[rows, hq * hd_:(hq + 1) * hd_] for hq in hq_r]
        ks = [k_ref[rows, hq * hd_:(hq + 1) * hd_] for hq in hq_r]
        kks = [_bdot_nt(ks[hq], ks[hq]) for hq in hq_r]
        qk0s = [_bdot_nt(qs[hq], ks[hq]) for hq in hq_r]
        khts = [ks[hq].T for hq in hq_r]
        cols = [slice(hvi, hvi + 1) for hvi in hv_r]
        decays = []
        for hvi in hv_r:
            diff = gc[:, cols[hvi]] - gct[hvi:hvi + 1, :]
            decays.append(jnp.where(lower, jnp.exp(jnp.where(lower, diff, 0.0)), 0.0))
        amats = [jnp.where(strict, bt[:, cols[hvi]] * kks[hvi // rep] * decays[hvi], 0.0) for hvi in hv_r]
        tinvs = _unit_lower_inverse(amats, inv_masks)
        rhs = [jnp.concatenate([bt[:, cols[hvi]] * v_ref[rows, hvi * hd_:(hvi + 1) * hd_],
                                beg[:, cols[hvi]] * ks[hvi // rep]], axis=1) for hvi in hv_r]
        uws = [_bdot(tinvs[hvi], rhs[hvi]) for hvi in hv_r]
        ss = [s_ref[hvi] for hvi in hv_r]
        wss = [_bdot(uws[hvi][:, hd_:], ss[hvi]) for hvi in hv_r]
        qss = [_bdot(qs[hvi // rep] * eg[:, cols[hvi]], ss[hvi]) for hvi in hv_r]
        vns = [uws[hvi][:, :hd_] - wss[hvi] for hvi in hv_r]
        outs = [qss[hvi] + _bdot(qk0s[hvi // rep] * decays[hvi], vns[hvi]) for hvi in hv_r]
        upds = [_bdot(khts[hvi // rep] * eendt[hvi:hvi + 1, :], vns[hvi]) for hvi in hv_r]
        for hvi in hv_r:
            ob_ref[rows, hvi * hd_:(hvi + 1) * hd_] = outs[hvi]
            s_ref[hvi] = dec[:, cols[hvi]] * ss[hvi] + upds[hvi]

    hg = hg_ref[...]
    parts = []
    for hvi in range(GDN_V_HEADS):
        vs = slice(hvi * hd_, (hvi + 1) * hd_)
        oh = ob_ref[:, vs]
        ms = jnp.mean(oh * oh, axis=-1, keepdims=True)
        parts.append((oh * lax.rsqrt(ms + RMS_EPS) * hg * _silu(z_ref[:, vs])).astype(BF16))
    y = jnp.concatenate(parts, axis=-1)
    m = jnp.dot(y, wout_ref[...], preferred_element_type=F32)
    o_ref[0] = x + gt_ref[0] * m


def _gdn_call(x, sh, sc, gt, ng, w_in, conv_w, a_log, dt_bias, head_g, w_out, *, tb):
    b, s, d = x.shape
    dqk = GDN_QK_HEADS * GDN_HEAD_DIM
    dvv = GDN_V_HEADS * GDN_HEAD_DIM
    nqkv = 2 * dqk + dvv
    wqkv = w_in[:, :nqkv].astype(BF16)
    wz = w_in[:, nqkv:nqkv + dvv].astype(BF16)
    padh = ((0, 0), (0, LANE - GDN_V_HEADS))
    wa = jnp.pad(w_in[:, nqkv + dvv:nqkv + dvv + GDN_V_HEADS], padh).astype(BF16)
    wb = jnp.pad(w_in[:, nqkv + dvv + GDN_V_HEADS:], padh).astype(BF16)
    alog = jnp.pad(a_log.reshape(1, -1), padh)
    dtb = jnp.pad(dt_bias.reshape(1, -1), padh)
    const2 = lambda bi, i: (0, 0)
    vec = pl.BlockSpec((1, 1, d), lambda bi, i: (bi, 0, 0))
    kern = functools.partial(_gdn_kernel, tb=tb)
    return pl.pallas_call(
        kern,
        grid=(b, s // tb),
        in_specs=[
            pl.BlockSpec((1, tb, d), lambda bi, i: (bi, i, 0)),
            vec, vec, vec,
            pl.BlockSpec((1, d), const2),
            pl.BlockSpec(wqkv.shape, const2),
            pl.BlockSpec(wz.shape, const2),
            pl.BlockSpec(wa.shape, const2),
            pl.BlockSpec(wb.shape, const2),
            pl.BlockSpec(conv_w.shape, const2),
            pl.BlockSpec((1, LANE), const2),
            pl.BlockSpec((1, LANE), const2),
            pl.BlockSpec((1, GDN_HEAD_DIM), const2),
            pl.BlockSpec((dvv, d), const2),
        ],
        out_specs=pl.BlockSpec((1, tb, d), lambda bi, i: (bi, i, 0)),
        out_shape=jax.ShapeDtypeStruct(x.shape, F32),
        scratch_shapes=[
            pltpu.VMEM((GDN_V_HEADS, GDN_HEAD_DIM, GDN_HEAD_DIM), F32),
            pltpu.VMEM((tb + SUBLANE, nqkv), F32),
            pltpu.VMEM((tb, dqk), F32),
            pltpu.VMEM((tb, dqk), F32),
            pltpu.VMEM((tb, dvv), F32),
            pltpu.VMEM((tb, dvv), F32),
            pltpu.VMEM((tb, LANE), F32),
            pltpu.VMEM((tb, LANE), F32),
            pltpu.VMEM((tb, dvv), F32),
        ],
        compiler_params=pltpu.CompilerParams(
            dimension_semantics=("arbitrary", "arbitrary"), vmem_limit_bytes=VMEM_LIMIT),
        name="gdn_layer",
    )(x, sh, sc, gt, ng.reshape(1, d), wqkv, wz, wa, wb, conv_w, alog, dtb,
      head_g.reshape(1, GDN_HEAD_DIM), w_out.astype(BF16))


def _top_values(x, n):
    vals = []
    for r in range(n):
        m = jnp.max(x, axis=0, keepdims=True)
        vals.append(m)
        if r + 1 < n:
            x = jnp.where(x == m, -jnp.inf, x)
    return vals


def _stack_rows(rows):
    l = rows[0].shape[1]
    sub = lax.broadcasted_iota(jnp.int32, (SUBLANE, l), 0)
    out = jnp.broadcast_to(rows[0], (SUBLANE, l))
    for i in range(1, len(rows)):
        out = jnp.where(sub == i, rows[i], out)
    return out


def _peer_kernel(x_ref, sh_ref, sc_ref, gt_ref, ng_ref, wq_ref, k1_ref, k2_ref, u_ref, vt_ref, o_ref,
                 tt_ref, s1_ref, w1_ref, s2_ref, e2_ref, c2_ref, yt_ref, *half_refs, tt, eb, ne):
    g = pl.program_id(2)
    nh = tt // PEER_NCOLS
    ht_refs = half_refs[:nh]
    at_refs = half_refs[nh:]
    nj = tt // LANE
    ab = eb // PEER_NKEYS
    k = PEER_TOPK

    @pl.when(g == 0)
    def _():
        x = x_ref[0]
        t = _norm_mod(x, ng_ref[...], sc_ref[0], sh_ref[0])
        tb = t.astype(BF16)
        tt_ref[...] = t.T.astype(BF16)
        q = jnp.dot(tb, wq_ref[...], preferred_element_type=F32)
        for h in range(PEER_HEADS):
            q1 = q[:, (2 * h) * PEER_HALF:(2 * h + 1) * PEER_HALF]
            q2 = q[:, (2 * h + 1) * PEER_HALF:(2 * h + 2) * PEER_HALF]
            s1_ref[h] = lax.dot_general(k1_ref[...], q1, NT, precision=HI, preferred_element_type=F32)
            s2_ref[h] = lax.dot_general(k2_ref[...], q2, NT, precision=HI, preferred_element_type=F32)
        yt_ref[...] = jnp.zeros_like(yt_ref)
        for ht_ref in ht_refs:
            ht_ref[...] = jnp.zeros_like(ht_ref)

        def route(idx, carry):
            h = idx // nj
            c0 = pl.multiple_of((idx % nj) * LANE, LANE)
            cols = pl.ds(c0, LANE)
            x1 = s1_ref[h, :, cols]
            x2 = s2_ref[h, :, cols]
            v1 = _top_values(x1, k + 1)
            v2 = _top_values(x2, k + 1)
            v2lo = _stack_rows(v2[:SUBLANE])
            v2hi = _stack_rows(v2[SUBLANE:2 * SUBLANE])
            v1hi = _stack_rows(v1[SUBLANE:2 * SUBLANE])
            sub = lax.broadcasted_iota(jnp.int32, (SUBLANE, LANE), 0)
            last = jnp.where(sub == 0, v1[0] + v2[k], jnp.where(sub == 1, v1[k] + v2[0], -jnp.inf))
            tiles = [v1[0] + v2lo, v1[0] + v2hi, v1hi + v2[0], last]
            tiles += [v1[i] + v2lo for i in range(1, SUBLANE)]
            cand = jnp.concatenate(tiles, axis=0)
            tops = _top_values(cand, k + 1)
            kth = tops[k - 1]
            cut = 0.5 * (kth + tops[k])
            top = v1[0] + v2[0]
            z = jnp.sum(jnp.where(cand >= kth, jnp.exp(cand - top), 0.0), axis=0, keepdims=True)
            w1_ref[h, :, cols] = jnp.exp(x1 - v1[0]) / z
            e2_ref[h, :, cols] = jnp.exp(x2 - v2[0])
            c2_ref[h, :, cols] = cut - x2
            return carry

        lax.fori_loop(0, PEER_HEADS * nj, route, 0)

    em = jnp.clip(g - 1, 0, ne - 1)
    arows = pl.ds(pl.multiple_of(em * ab, SUBLANE), SUBLANE)
    bh = PEER_NKEYS // 2
    d = yt_ref.shape[0]
    ngroup = PEER_NCOLS // LANE

    def gate_block(n, jj, a2, hb):
        cols = slice(n * PEER_NCOLS + jj * LANE, n * PEER_NCOLS + (jj + 1) * LANE)
        lcols = slice(jj * LANE, (jj + 1) * LANE)
        brows = slice(hb * bh, (hb + 1) * bh)
        als = (2 * a2, 2 * a2 + 1)
        s1t = [s1_ref[h, arows, cols] for h in range(PEER_HEADS)]
        w1t = [w1_ref[h, arows, cols] for h in range(PEER_HEADS)]
        accs = [jnp.zeros((bh, LANE), F32) for _ in als]
        for h in range(PEER_HEADS):
            c2 = c2_ref[h, brows, cols]
            e2 = e2_ref[h, brows, cols]
            for i, al in enumerate(als):
                sel = s1t[h][al:al + 1, :] >= c2
                accs[i] = accs[i] + jnp.where(sel, e2 * w1t[h][al:al + 1, :], 0.0)
        fold = None
        for i, al in enumerate(als):
            rows = slice(al * PEER_NKEYS + hb * bh, al * PEER_NKEYS + (hb + 1) * bh)
            out = accs[i] * _gelu_tanh(ht_refs[n][rows, lcols])
            at_refs[n][rows, lcols] = out.astype(BF16)
            bits = lax.bitcast_convert_type(out, jnp.int32)
            fold = bits if fold is None else fold | bits
        while fold.shape[0] > SUBLANE:
            half = fold.shape[0] // 2
            fold = fold[:half] | fold[half:]
        return fold[0, 0] & jnp.minimum(g, 0)

    def matmul_pieces(n):
        ncols = slice(n * PEER_NCOLS, (n + 1) * PEER_NCOLS)
        pieces = []
        for m in range(d // PEER_MROWS):
            mrows = slice(m * PEER_MROWS, (m + 1) * PEER_MROWS)

            def second(zero, m=m, mrows=mrows):
                lrows = pl.ds(pl.multiple_of(zero + m * PEER_MROWS, PEER_MROWS), PEER_MROWS)
                yt_ref[mrows, ncols] += jnp.dot(vt_ref[lrows, :], at_refs[n][...], preferred_element_type=F32)

            def first(zero, m=m, mrows=mrows):
                lrows = pl.ds(pl.multiple_of(zero + m * PEER_MROWS, PEER_MROWS), PEER_MROWS)
                ht_refs[n][mrows, :] = jnp.dot(u_ref[lrows, :], tt_ref[:, ncols], preferred_element_type=F32)
            pieces += [second, first]
        return pieces

    pending = []
    for n in range(tt // PEER_NCOLS):
        blk = 0
        for jj in range(ngroup):
            for a2 in range(ab // 2):
                for hb in range(2):
                    zero = gate_block(n, jj, a2, hb)
                    if pending and blk % PEER_GATE_PER_PIECE == 1:
                        pending.pop(0)(zero)
                    blk += 1
        for piece in pending:
            piece(zero)
        pending = matmul_pieces(n)
    for piece in pending:
        piece(zero)

    @pl.when(g == ne)
    def _():
        o_ref[0] = x_ref[0] + gt_ref[0] * yt_ref[...].T


def _peer_call(x, sh, sc, gt, ng, wq, k1, k2, u, v, *, tt, eb):
    b, s, d = x.shape
    assert eb == SUBLANE * PEER_NKEYS and u.shape[0] % eb == 0
    ne = u.shape[0] // eb
    ub = u.astype(BF16)
    vtb = v.T.astype(BF16)
    const3 = lambda bi, i, e: (0, 0)
    vec = pl.BlockSpec((1, 1, d), lambda bi, i, e: (bi, 0, 0))
    kern = functools.partial(_peer_kernel, tt=tt, eb=eb, ne=ne)
    sshape = (PEER_HEADS, PEER_NKEYS, tt)
    return pl.pallas_call(
        kern,
        grid=(b, s // tt, ne + 1),
        in_specs=[
            pl.BlockSpec((1, tt, d), lambda bi, i, e: (bi, i, 0)),
            vec, vec, vec,
            pl.BlockSpec((1, d), const3),
            pl.BlockSpec(wq.shape, const3),
            pl.BlockSpec(k1.shape, const3),
            pl.BlockSpec(k2.shape, const3),
            pl.BlockSpec((eb, d), lambda bi, i, g: (jnp.minimum(g, ne - 1), 0)),
            pl.BlockSpec((d, eb), lambda bi, i, g: (0, jnp.maximum(g - 1, 0))),
        ],
        out_specs=pl.BlockSpec((1, tt, d), lambda bi, i, e: (bi, i, 0)),
        out_shape=jax.ShapeDtypeStruct(x.shape, F32),
        scratch_shapes=[
            pltpu.VMEM((d, tt), BF16),
            pltpu.VMEM(sshape, F32),
            pltpu.VMEM(sshape, F32),
            pltpu.VMEM(sshape, F32),
            pltpu.VMEM(sshape, F32),
            pltpu.VMEM(sshape, F32),
            pltpu.VMEM((d, tt), F32),
        ] + [pltpu.VMEM((eb, PEER_NCOLS), F32)] * (tt // PEER_NCOLS)
          + [pltpu.VMEM((eb, PEER_NCOLS), BF16)] * (tt // PEER_NCOLS),
        compiler_params=pltpu.CompilerParams(
            dimension_semantics=("arbitrary", "arbitrary", "arbitrary"), vmem_limit_bytes=VMEM_LIMIT),
        name="peer_layer",
    )(x, sh, sc, gt, ng.reshape(1, d), wq.astype(BF16), k1, k2, ub, vtb)


def _final_kernel(x_ref, g_ref, o_ref):
    x = x_ref[0]
    ms = jnp.mean(x * x, axis=-1, keepdims=True)
    o_ref[0] = x * lax.rsqrt(ms + RMS_EPS) * g_ref[...]


def _final_call(x, g, *, tb):
    b, s, d = x.shape
    return pl.pallas_call(
        _final_kernel,
        grid=(b, s // tb),
        in_specs=[pl.BlockSpec((1, tb, d), lambda bi, i: (bi, i, 0)),
                  pl.BlockSpec((1, d), lambda bi, i: (0, 0))],
        out_specs=pl.BlockSpec((1, tb, d), lambda bi, i: (bi, i, 0)),
        out_shape=jax.ShapeDtypeStruct(x.shape, F32),
        name="final_norm",
    )(x, g.reshape(1, d))


def _pick(n, pref):
    return pref if n % pref == 0 else n


def kernel(x, c, ada_w, ada_b, norm_mix_g, norm_ffn_g, gla_w_in, gla_w_g2, gla_b_g2, gla_norm_g,
           gla_w_out, gdn_w_in, gdn_conv_w, gdn_a_log, gdn_dt_bias, gdn_norm_g, gdn_w_out,
           peer_wq, peer_k1, peer_k2, peer_u, peer_v, final_norm_g):
    b, s, d = x.shape
    depth = ada_w.shape[0]
    c_pad = jnp.pad(c, ((0, SUBLANE - b % SUBLANE if b % SUBLANE else 0), (0, 0)))
    mod = _mod_call(c_pad, ada_w, ada_b)[:, :b, :].reshape(depth, b, 6, 1, d)
    tb_mix = _pick(s, 256)
    tt = _pick(s, 512)
    eb = 1024
    for i in range(depth):
        sh1, sc1, gt1, sh2, sc2, gt2 = (mod[i, :, j] for j in range(6))
        j = i // 2
        if i % 2 == 0:
            x = _gla_call(x, sh1, sc1, gt1, norm_mix_g[i], gla_w_in[j], gla_w_g2[j], gla_b_g2[j],
                          gla_norm_g[j], gla_w_out[j], tb=tb_mix)
        else:
            x = _gdn_call(x, sh1, sc1, gt1, norm_mix_g[i], gdn_w_in[j], gdn_conv_w[j], gdn_a_log[j],
                          gdn_dt_bias[j], gdn_norm_g[j], gdn_w_out[j], tb=tb_mix)
        x = _peer_call(x, sh2, sc2, gt2, norm_ffn_g[i], peer_wq[i], peer_k1[i], peer_k2[i],
                       peer_u[i], peer_v[i], tt=tt, eb=eb)
    return _final_call(x, final_norm_g, tb=_pick(s, 512))
```

```python
import functools

import jax
import jax.numpy as jnp
from jax import lax
from jax.experimental import pallas as pl
from jax.experimental.pallas import tpu as pltpu

F32 = jnp.float32
BF16 = jnp.bfloat16
HI = lax.Precision.HIGHEST

RMS_EPS = 1e-6
CHUNK = 64

GLA_HEADS = 4
GLA_GATE_RANK = 16
GLA_TAU = 16.0

GDN_QK_HEADS = 8
GDN_V_HEADS = 16
GDN_HEAD_DIM = 128
GDN_CONV = 4

PEER_HEADS = 8
PEER_TOPK = 16
PEER_NKEYS = 128
PEER_HALF = 128
PEER_NCOLS = 256
PEER_MROWS = 512
PEER_GATE_PER_PIECE = 4

LANE = 128
SUBLANE = 8
VMEM_LIMIT = 56 * 1024 * 1024

NT = (((1,), (1,)), ((), ()))


def _sigmoid(x):
    return 1.0 / (1.0 + jnp.exp(-x))


def _silu(x):
    return x * _sigmoid(x)


def _softplus(x):
    return jnp.maximum(x, 0.0) + jnp.log1p(jnp.exp(-jnp.abs(x)))


def _gelu_tanh(x):
    c = 0.7978845608028654
    hx = 0.5 * x
    return hx + hx * jnp.tanh(x * (c + (c * 0.044715) * (x * x)))


def _norm_mod(x, g, sc, sh):
    ms = jnp.mean(x * x, axis=-1, keepdims=True)
    return (x * lax.rsqrt(ms + RMS_EPS) * g) * (1.0 + sc) + sh


def _bdot(a, b):
    return jnp.dot(a.astype(BF16), b.astype(BF16), preferred_element_type=F32)


def _bdot_nt(a, b):
    return lax.dot_general(a.astype(BF16), b.astype(BF16), NT, preferred_element_type=F32)


def _hdot(a, b):
    return jnp.dot(a, b, precision=HI, preferred_element_type=F32)


def _tril(n, k=0):
    r = lax.broadcasted_iota(jnp.int32, (n, n), 0)
    c = lax.broadcasted_iota(jnp.int32, (n, n), 1)
    return (c - r) <= k


def _mod_kernel(c_ref, w_ref, b_ref, o_ref):
    c = c_ref[...]
    o_ref[0] = _hdot(_silu(c), w_ref[0]) + b_ref[0]


def _mod_call(c_pad, ada_w, ada_b):
    depth, d, d6 = ada_w.shape
    nb = d6 // d
    return pl.pallas_call(
        _mod_kernel,
        grid=(depth, nb),
        in_specs=[
            pl.BlockSpec(c_pad.shape, lambda i, j: (0, 0)),
            pl.BlockSpec((1, d, d), lambda i, j: (i, 0, j)),
            pl.BlockSpec((1, 1, d), lambda i, j: (i, 0, j)),
        ],
        out_specs=pl.BlockSpec((1, c_pad.shape[0], d), lambda i, j: (i, 0, j)),
        out_shape=jax.ShapeDtypeStruct((depth, c_pad.shape[0], d6), F32),
        name="adaln_mod",
    )(c_pad, ada_w, ada_b.reshape(depth, 1, d6))


def _gla_kernel(x_ref, sh_ref, sc_ref, gt_ref, ng_ref, wqkvr_ref, wg1_ref, wg2_ref, bg2_ref,
                hg_ref, wout_ref, o_ref, st_ref, p_ref, la_ref, ob_ref, *, tb, dk, dv):
    hk = dk // GLA_HEADS
    hv = dv // GLA_HEADS
    nc = tb // CHUNK

    @pl.when(pl.program_id(1) == 0)
    def _():
        st_ref[...] = jnp.zeros_like(st_ref)

    x = x_ref[0]
    h = _norm_mod(x, ng_ref[...], sc_ref[0], sh_ref[0])
    hb = h.astype(BF16)
    p_ref[...] = jnp.dot(hb, wqkvr_ref[...], preferred_element_type=F32)
    g1 = jnp.dot(hb, wg1_ref[...], preferred_element_type=F32)
    g = jnp.dot(g1.astype(BF16), wg2_ref[...], preferred_element_type=F32) + bg2_ref[...]
    la_ref[...] = (jnp.minimum(g, 0.0) - jnp.log1p(jnp.exp(-jnp.abs(g)))) * (1.0 / GLA_TAU)

    tril = _tril(CHUNK).astype(F32)
    causal = _tril(CHUNK)
    scale = hk ** -0.5

    for c in range(nc):
        rows = slice(c * CHUNK, (c + 1) * CHUNK)
        cum = _hdot(tril, la_ref[rows, :])
        cl = cum[CHUNK - 1:CHUNK, :]
        e_pos = jnp.exp(cum)
        e_neg = jnp.exp(-cum)
        e_end = jnp.exp(cl - cum)
        dec = jnp.exp(cl)
        hr = range(GLA_HEADS)
        ksl = [slice(hd * hk, (hd + 1) * hk) for hd in hr]
        qis = [p_ref[rows, hd * hk:(hd + 1) * hk] * (scale * e_pos[:, ksl[hd]]) for hd in hr]
        kraw = [p_ref[rows, dk + hd * hk:dk + (hd + 1) * hk] for hd in hr]
        vs = [p_ref[rows, 2 * dk + hd * hv:2 * dk + (hd + 1) * hv] for hd in hr]
        attns = [jnp.where(causal, _bdot_nt(qis[hd], kraw[hd] * e_neg[:, ksl[hd]]), 0.0) for hd in hr]
        sts = [st_ref[hd] for hd in hr]
        inter = [_bdot_nt(qis[hd], sts[hd]) for hd in hr]
        upds = [_bdot(vs[hd].T, kraw[hd] * e_end[:, ksl[hd]]) for hd in hr]
        intra = [_bdot(attns[hd], vs[hd]) for hd in hr]
        for hd in hr:
            ob_ref[rows, hd * hv:(hd + 1) * hv] = intra[hd] + inter[hd]
            st_ref[hd] = dec[:, ksl[hd]] * sts[hd] + upds[hd]

    hg = hg_ref[...]
    parts = []
    for hd in range(GLA_HEADS):
        oh = ob_ref[:, hd * hv:(hd + 1) * hv]
        ms = jnp.mean(oh * oh, axis=-1, keepdims=True)
        r = p_ref[:, 2 * dk + dv + hd * hv:2 * dk + dv + (hd + 1) * hv]
        parts.append((oh * lax.rsqrt(ms + RMS_EPS) * hg * _silu(r)).astype(BF16))
    y = jnp.concatenate(parts, axis=-1)
    m = jnp.dot(y, wout_ref[...], preferred_element_type=F32)
    o_ref[0] = x + gt_ref[0] * m


def _gla_call(x, sh, sc, gt, ng, w_in, w_g2, b_g2, head_g, w_out, *, tb):
    b, s, d = x.shape
    dk = w_g2.shape[1]
    dv = w_out.shape[0]
    wqkvr = w_in[:, :2 * dk + 2 * dv].astype(BF16)
    wg1 = jnp.pad(w_in[:, 2 * dk + 2 * dv:], ((0, 0), (0, LANE - GLA_GATE_RANK))).astype(BF16)
    wg2 = jnp.pad(w_g2, ((0, LANE - GLA_GATE_RANK), (0, 0))).astype(BF16)
    hv = dv // GLA_HEADS
    hk = dk // GLA_HEADS
    const2 = lambda bi, i: (0, 0)
    vec = pl.BlockSpec((1, 1, d), lambda bi, i: (bi, 0, 0))
    kern = functools.partial(_gla_kernel, tb=tb, dk=dk, dv=dv)
    return pl.pallas_call(
        kern,
        grid=(b, s // tb),
        in_specs=[
            pl.BlockSpec((1, tb, d), lambda bi, i: (bi, i, 0)),
            vec, vec, vec,
            pl.BlockSpec((1, d), const2),
            pl.BlockSpec(wqkvr.shape, const2),
            pl.BlockSpec(wg1.shape, const2),
            pl.BlockSpec(wg2.shape, const2),
            pl.BlockSpec((1, dk), const2),
            pl.BlockSpec((1, hv), const2),
            pl.BlockSpec((dv, d), const2),
        ],
        out_specs=pl.BlockSpec((1, tb, d), lambda bi, i: (bi, i, 0)),
        out_shape=jax.ShapeDtypeStruct(x.shape, F32),
        scratch_shapes=[
            pltpu.VMEM((GLA_HEADS, hv, hk), F32),
            pltpu.VMEM((tb, 2 * dk + 2 * dv), F32),
            pltpu.VMEM((tb, dk), F32),
            pltpu.VMEM((tb, dv), F32),
        ],
        compiler_params=pltpu.CompilerParams(
            dimension_semantics=("arbitrary", "arbitrary"), vmem_limit_bytes=VMEM_LIMIT),
        name="gla_layer",
    )(x, sh, sc, gt, ng.reshape(1, d), wqkvr, wg1, wg2, b_g2.reshape(1, dk),
      head_g.reshape(1, hv), w_out.astype(BF16))


def _block_inverse_masks(n):
    r = lax.broadcasted_iota(jnp.int32, (n, n), 0)
    c = lax.broadcasted_iota(jnp.int32, (n, n), 1)
    eye = (r == c).astype(F32)
    first = (r // 2) == (c // 2)
    quads = []
    s = 2
    while s < n:
        quads.append(((r // (2 * s)) == (c // (2 * s))) & ((r // s) != (c // s)))
        s *= 2
    return eye, first, quads


def _unit_lower_inverse(a_list, masks):
    eye, first, quads = masks
    ds = [eye - jnp.where(first, a, 0.0) for a in a_list]
    for quad in quads:
        ms = [_bdot(d, jnp.where(quad, a, 0.0)) for d, a in zip(ds, a_list)]
        ds = [d - _bdot(m, d) for d, m in zip(ds, ms)]
    return ds


def _gdn_kernel(x_ref, sh_ref, sc_ref, gt_ref, ng_ref, wqkv_ref, wz_ref, wa_ref, wb_ref, cw_ref,
                alog_ref, dtb_ref, hg_ref, wout_ref, o_ref,
                s_ref, cb_ref, q_ref, k_ref, v_ref, z_ref, g_ref, bt_ref, ob_ref, *, tb):
    dqk = GDN_QK_HEADS * GDN_HEAD_DIM
    dvv = GDN_V_HEADS * GDN_HEAD_DIM
    hd_ = GDN_HEAD_DIM
    nc = tb // CHUNK
    rep = GDN_V_HEADS // GDN_QK_HEADS

    @pl.when(pl.program_id(1) == 0)
    def _():
        s_ref[...] = jnp.zeros_like(s_ref)
        cb_ref[0:SUBLANE, :] = jnp.zeros((SUBLANE, cb_ref.shape[1]), F32)

    x = x_ref[0]
    h = _norm_mod(x, ng_ref[...], sc_ref[0], sh_ref[0])
    hb = h.astype(BF16)
    cb_ref[SUBLANE:SUBLANE + tb, :] = jnp.dot(hb, wqkv_ref[...], preferred_element_type=F32)
    z_ref[...] = jnp.dot(hb, wz_ref[...], preferred_element_type=F32)
    a = jnp.dot(hb, wa_ref[...], preferred_element_type=F32)
    bb = jnp.dot(hb, wb_ref[...], preferred_element_type=F32)
    g_ref[...] = -jnp.exp(alog_ref[...]) * _softplus(a + dtb_ref[...])
    bt_ref[...] = _sigmoid(bb)

    off = SUBLANE - (GDN_CONV - 1)
    conv = cw_ref[0:1, :] * cb_ref[off:off + tb, :]
    for j in range(1, GDN_CONV):
        conv = conv + cw_ref[j:j + 1, :] * cb_ref[off + j:off + j + tb, :]
    cb_ref[0:SUBLANE, :] = cb_ref[tb:tb + SUBLANE, :]
    qkv = _silu(conv)
    qscale = hd_ ** -0.5
    for hq in range(GDN_QK_HEADS):
        sl = slice(hq * hd_, (hq + 1) * hd_)
        qh = qkv[:, hq * hd_:(hq + 1) * hd_]
        kh = qkv[:, dqk + hq * hd_:dqk + (hq + 1) * hd_]
        q_ref[:, sl] = qh * (lax.rsqrt(jnp.sum(qh * qh, axis=-1, keepdims=True) + 1e-6) * qscale)
        k_ref[:, sl] = kh * lax.rsqrt(jnp.sum(kh * kh, axis=-1, keepdims=True) + 1e-6)
    v_ref[...] = qkv[:, 2 * dqk:]

    tril = _tril(CHUNK).astype(F32)
    lower = _tril(CHUNK)
    strict = _tril(CHUNK, -1)
    inv_masks = _block_inverse_masks(CHUNK)

    for c in range(nc):
        rows = slice(c * CHUNK, (c + 1) * CHUNK)
        gc = _hdot(tril, g_ref[rows, :])
        gct = gc.T
        gl = gc[CHUNK - 1:CHUNK, :]
        eg = jnp.exp(gc)
        eendt = jnp.exp(gl - gc).T
        dec = jnp.exp(gl)
        bt = bt_ref[rows, :]
        beg = bt * eg
        hq_r = range(GDN_QK_HEADS)
        hv_r = range(GDN_V_HEADS)
        qs = [q_ref[rows, hq * hd_:(hq + 1) * hd_] for hq in hq_r]
        ks = [k_ref[rows, hq * hd_:(hq + 1) * hd_] for hq in hq_r]
        kks = [_bdot_nt(ks[hq], ks[hq]) for hq in hq_r]
        qk0s = [_bdot_nt(qs[hq], ks[hq]) for hq in hq_r]
        khts = [ks[hq].T for hq in hq_r]
        cols = [slice(hvi, hvi + 1) for hvi in hv_r]
        decays = []
        for hvi in hv_r:
            diff = gc[:, cols[hvi]] - gct[hvi:hvi + 1, :]
            decays.append(jnp.where(lower, jnp.exp(jnp.where(lower, diff, 0.0)), 0.0))
        amats = [jnp.where(strict, bt[:, cols[hvi]] * kks[hvi // rep] * decays[hvi], 0.0) for hvi in hv_r]
        tinvs = _unit_lower_inverse(amats, inv_masks)
        rhs = [jnp.concatenate([bt[:, cols[hvi]] * v_ref[rows, hvi * hd_:(hvi + 1) * hd_],
                                beg[:, cols[hvi]] * ks[hvi // rep]], axis=1) for hvi in hv_r]
        uws = [_bdot(tinvs[hvi], rhs[hvi]) for hvi in hv_r]
        ss = [s_ref[hvi] for hvi in hv_r]
        wss = [_bdot(uws[hvi][:, hd_:], ss[hvi]) for hvi in hv_r]
        qss = [_bdot(qs[hvi // rep] * eg[:, cols[hvi]], ss[hvi]) for hvi in hv_r]
        vns = [uws[hvi][:, :hd_] - wss[hvi] for hvi in hv_r]
        outs = [qss[hvi] + _bdot(qk0s[hvi // rep] * decays[hvi], vns[hvi]) for hvi in hv_r]
        upds = [_bdot(khts[hvi // rep] * eendt[hvi:hvi + 1, :], vns[hvi]) for hvi in hv_r]
        for hvi in hv_r:
            ob_ref[rows, hvi * hd_:(hvi + 1) * hd_] = outs[hvi]
            s_ref[hvi] = dec[:, cols[hvi]] * ss[hvi] + upds[hvi]

    hg = hg_ref[...]
    parts = []
    for hvi in range(GDN_V_HEADS):
        vs = slice(hvi * hd_, (hvi + 1) * hd_)
        oh = ob_ref[:, vs]
        ms = jnp.mean(oh * oh, axis=-1, keepdims=True)
        parts.append((oh * lax.rsqrt(ms + RMS_EPS) * hg * _silu(z_ref[:, vs])).astype(BF16))
    y = jnp.concatenate(parts, axis=-1)
    m = jnp.dot(y, wout_ref[...], preferred_element_type=F32)
    o_ref[0] = x + gt_ref[0] * m


def _gdn_call(x, sh, sc, gt, ng, w_in, conv_w, a_log, dt_bias, head_g, w_out, *, tb):
    b, s, d = x.shape
    dqk = GDN_QK_HEADS * GDN_HEAD_DIM
    dvv = GDN_V_HEADS * GDN_HEAD_DIM
    nqkv = 2 * dqk + dvv
    wqkv = w_in[:, :nqkv].astype(BF16)
    wz = w_in[:, nqkv:nqkv + dvv].astype(BF16)
    padh = ((0, 0), (0, LANE - GDN_V_HEADS))
    wa = jnp.pad(w_in[:, nqkv + dvv:nqkv + dvv + GDN_V_HEADS], padh).astype(BF16)
    wb = jnp.pad(w_in[:, nqkv + dvv + GDN_V_HEADS:], padh).astype(BF16)
    alog = jnp.pad(a_log.reshape(1, -1), padh)
    dtb = jnp.pad(dt_bias.reshape(1, -1), padh)
    const2 = lambda bi, i: (0, 0)
    vec = pl.BlockSpec((1, 1, d), lambda bi, i: (bi, 0, 0))
    kern = functools.partial(_gdn_kernel, tb=tb)
    return pl.pallas_call(
        kern,
        grid=(b, s // tb),
        in_specs=[
            pl.BlockSpec((1, tb, d), lambda bi, i: (bi, i, 0)),
            vec, vec, vec,
            pl.BlockSpec((1, d), const2),
            pl.BlockSpec(wqkv.shape, const2),
            pl.BlockSpec(wz.shape, const2),
            pl.BlockSpec(wa.shape, const2),
            pl.BlockSpec(wb.shape, const2),
            pl.BlockSpec(conv_w.shape, const2),
            pl.BlockSpec((1, LANE), const2),
            pl.BlockSpec((1, LANE), const2),
            pl.BlockSpec((1, GDN_HEAD_DIM), const2),
            pl.BlockSpec((dvv, d), const2),
        ],
        out_specs=pl.BlockSpec((1, tb, d), lambda bi, i: (bi, i, 0)),
        out_shape=jax.ShapeDtypeStruct(x.shape, F32),
        scratch_shapes=[
            pltpu.VMEM((GDN_V_HEADS, GDN_HEAD_DIM, GDN_HEAD_DIM), F32),
            pltpu.VMEM((tb + SUBLANE, nqkv), F32),
            pltpu.VMEM((tb, dqk), F32),
            pltpu.VMEM((tb, dqk), F32),
            pltpu.VMEM((tb, dvv), F32),
            pltpu.VMEM((tb, dvv), F32),
            pltpu.VMEM((tb, LANE), F32),
            pltpu.VMEM((tb, LANE), F32),
            pltpu.VMEM((tb, dvv), F32),
        ],
        compiler_params=pltpu.CompilerParams(
            dimension_semantics=("arbitrary", "arbitrary"), vmem_limit_bytes=VMEM_LIMIT),
        name="gdn_layer",
    )(x, sh, sc, gt, ng.reshape(1, d), wqkv, wz, wa, wb, conv_w, alog, dtb,
      head_g.reshape(1, GDN_HEAD_DIM), w_out.astype(BF16))


def _top_values(x, n):
    vals = []
    for r in range(n):
        m = jnp.max(x, axis=0, keepdims=True)
        vals.append(m)
        if r + 1 < n:
            x = jnp.where(x == m, -jnp.inf, x)
    return vals


def _stack_rows(rows):
    l = rows[0].shape[1]
    sub = lax.broadcasted_iota(jnp.int32, (SUBLANE, l), 0)
    out = jnp.broadcast_to(rows[0], (SUBLANE, l))
    for i in range(1, len(rows)):
        out = jnp.where(sub == i, rows[i], out)
    return out


def _peer_kernel(x_ref, sh_ref, sc_ref, gt_ref, ng_ref, wq_ref, k1_ref, k2_ref, u_ref, vt_ref, o_ref,
                 tt_ref, s1_ref, w1_ref, s2_ref, e2_ref, c2_ref, yt_ref, *half_refs, tt, eb, ne):
    g = pl.program_id(2)
    nh = tt // PEER_NCOLS
    ht_refs = half_refs[:nh]
    at_refs = half_refs[nh:]
    nj = tt // LANE
    ab = eb // PEER_NKEYS
    k = PEER_TOPK

    @pl.when(g == 0)
    def _():
        x = x_ref[0]
        t = _norm_mod(x, ng_ref[...], sc_ref[0], sh_ref[0])
        tb = t.astype(BF16)
        tt_ref[...] = t.T.astype(BF16)
        q = jnp.dot(tb, wq_ref[...], preferred_element_type=F32)
        for h in range(PEER_HEADS):
            q1 = q[:, (2 * h) * PEER_HALF:(2 * h + 1) * PEER_HALF]
            q2 = q[:, (2 * h + 1) * PEER_HALF:(2 * h + 2) * PEER_HALF]
            s1_ref[h] = lax.dot_general(k1_ref[...], q1, NT, precision=HI, preferred_element_type=F32)
            s2_ref[h] = lax.dot_general(k2_ref[...], q2, NT, precision=HI, preferred_element_type=F32)
        yt_ref[...] = jnp.zeros_like(yt_ref)
        for ht_ref in ht_refs:
            ht_ref[...] = jnp.zeros_like(ht_ref)

        def route(idx, carry):
            h = idx // nj
            c0 = pl.multiple_of((idx % nj) * LANE, LANE)
            cols = pl.ds(c0, LANE)
            x1 = s1_ref[h, :, cols]
            x2 = s2_ref[h, :, cols]
            v1 = _top_values(x1, k + 1)
            v2 = _top_values(x2, k + 1)
            v2lo = _stack_rows(v2[:SUBLANE])
            v2hi = _stack_rows(v2[SUBLANE:2 * SUBLANE])
            v1hi = _stack_rows(v1[SUBLANE:2 * SUBLANE])
            sub = lax.broadcasted_iota(jnp.int32, (SUBLANE, LANE), 0)
            last = jnp.where(sub == 0, v1[0] + v2[k], jnp.where(sub == 1, v1[k] + v2[0], -jnp.inf))
            tiles = [v1[0] + v2lo, v1[0] + v2hi, v1hi + v2[0], last]
            tiles += [v1[i] + v2lo for i in range(1, SUBLANE)]
            cand = jnp.concatenate(tiles, axis=0)
            tops = _top_values(cand, k + 1)
            kth = tops[k - 1]
            cut = 0.5 * (kth + tops[k])
            top = v1[0] + v2[0]
            z = jnp.sum(jnp.where(cand >= kth, jnp.exp(cand - top), 0.0), axis=0, keepdims=True)
            w1_ref[h, :, cols] = jnp.exp(x1 - v1[0]) / z
            e2_ref[h, :, cols] = jnp.exp(x2 - v2[0])
            c2_ref[h, :, cols] = cut - x2
            return carry

        lax.fori_loop(0, PEER_HEADS * nj, route, 0)

    em = jnp.clip(g - 1, 0, ne - 1)
    arows = pl.ds(pl.multiple_of(em * ab, SUBLANE), SUBLANE)
    bh = PEER_NKEYS // 2
    d = yt_ref.shape[0]
    ngroup = PEER_NCOLS // LANE

    def gate_block(n, jj, a2, hb):
        cols = slice(n * PEER_NCOLS + jj * LANE, n * PEER_NCOLS + (jj + 1) * LANE)
        lcols = slice(jj * LANE, (jj + 1) * LANE)
        brows = slice(hb * bh, (hb + 1) * bh)
        als = (2 * a2, 2 * a2 + 1)
        s1t = [s1_ref[h, arows, cols] for h in range(PEER_HEADS)]
        w1t = [w1_ref[h, arows, cols] for h in range(PEER_HEADS)]
        accs = [jnp.zeros((bh, LANE), F32) for _ in als]
        for h in range(PEER_HEADS):
            c2 = c2_ref[h, brows, cols]
            e2 = e2_ref[h, brows, cols]
            for i, al in enumerate(als):
                sel = s1t[h][al:al + 1, :] >= c2
                accs[i] = accs[i] + jnp.where(sel, e2 * w1t[h][al:al + 1, :], 0.0)
        fold = None
        for i, al in enumerate(als):
            rows = slice(al * PEER_NKEYS + hb * bh, al * PEER_NKEYS + (hb + 1) * bh)
            out = accs[i] * _gelu_tanh(ht_refs[n][rows, lcols])
            at_refs[n][rows, lcols] = out.astype(BF16)
            bits = lax.bitcast_convert_type(out, jnp.int32)
            fold = bits if fold is None else fold | bits
        while fold.shape[0] > SUBLANE:
            half = fold.shape[0] // 2
            fold = fold[:half] | fold[half:]
        return fold & jnp.minimum(g, 0)

    def after(zero, rhs):
        head = pltpu.bitcast(rhs[:PEER_NCOLS], jnp.int32)
        head = head | jnp.tile(zero, (head.shape[0] // SUBLANE, head.shape[1] // LANE))
        return jnp.concatenate([pltpu.bitcast(head, BF16), rhs[PEER_NCOLS:]], axis=0)

    def matmul_pieces(n):
        ncols = slice(n * PEER_NCOLS, (n + 1) * PEER_NCOLS)
        pieces = []
        for m in range(d // PEER_MROWS):
            mrows = slice(m * PEER_MROWS, (m + 1) * PEER_MROWS)

            def second(zero, mrows=mrows):
                yt_ref[mrows, ncols] += jnp.dot(vt_ref[mrows, :], after(zero, at_refs[n][...]),
                                                preferred_element_type=F32)

            def first(zero, mrows=mrows):
                ht_refs[n][mrows, :] = jnp.dot(u_ref[mrows, :], after(zero, tt_ref[:, ncols]),
                                               preferred_element_type=F32)
            pieces += [second, first]
        return pieces

    pending = []
    for n in range(tt // PEER_NCOLS):
        blk = 0
        for jj in range(ngroup):
            for a2 in range(ab // 2):
                for hb in range(2):
                    zero = gate_block(n, jj, a2, hb)
                    if pending and blk % PEER_GATE_PER_PIECE == 1:
                        pending.pop(0)(zero)
                    blk += 1
        for piece in pending:
            piece(zero)
        pending = matmul_pieces(n)
    for piece in pending:
        piece(zero)

    @pl.when(g == ne)
    def _():
        o_ref[0] = x_ref[0] + gt_ref[0] * yt_ref[...].T


def _peer_call(x, sh, sc, gt, ng, wq, k1, k2, u, v, *, tt, eb):
    b, s, d = x.shape
    assert eb == SUBLANE * PEER_NKEYS and u.shape[0] % eb == 0
    ne = u.shape[0] // eb
    ub = u.astype(BF16)
    vtb = v.T.astype(BF16)
    const3 = lambda bi, i, e: (0, 0)
    vec = pl.BlockSpec((1, 1, d), lambda bi, i, e: (bi, 0, 0))
    kern = functools.partial(_peer_kernel, tt=tt, eb=eb, ne=ne)
    sshape = (PEER_HEADS, PEER_NKEYS, tt)
    return pl.pallas_call(
        kern,
        grid=(b, s // tt, ne + 1),
        in_specs=[
            pl.BlockSpec((1, tt, d), lambda bi, i, e: (bi, i, 0)),
            vec, vec, vec,
            pl.BlockSpec((1, d), const3),
            pl.BlockSpec(wq.shape, const3),
            pl.BlockSpec(k1.shape, const3),
            pl.BlockSpec(k2.shape, const3),
            pl.BlockSpec((eb, d), lambda bi, i, g: (jnp.minimum(g, ne - 1), 0)),
            pl.BlockSpec((d, eb), lambda bi, i, g: (0, jnp.maximum(g - 1, 0))),
        ],
        out_specs=pl.BlockSpec((1, tt, d), lambda bi, i, e: (bi, i, 0)),
        out_shape=jax.ShapeDtypeStruct(x.shape, F32),
        scratch_shapes=[
            pltpu.VMEM((d, tt), BF16),
            pltpu.VMEM(sshape, F32),
            pltpu.VMEM(sshape, F32),
            pltpu.VMEM(sshape, F32),
            pltpu.VMEM(sshape, F32),
            pltpu.VMEM(sshape, F32),
            pltpu.VMEM((d, tt), F32),
        ] + [pltpu.VMEM((eb, PEER_NCOLS), F32)] * (tt // PEER_NCOLS)
          + [pltpu.VMEM((eb, PEER_NCOLS), BF16)] * (tt // PEER_NCOLS),
        compiler_params=pltpu.CompilerParams(
            dimension_semantics=("arbitrary", "arbitrary", "arbitrary"), vmem_limit_bytes=VMEM_LIMIT),
        name="peer_layer",
    )(x, sh, sc, gt, ng.reshape(1, d), wq.astype(BF16), k1, k2, ub, vtb)


def _final_kernel(x_ref, g_ref, o_ref):
    x = x_ref[0]
    ms = jnp.mean(x * x, axis=-1, keepdims=True)
    o_ref[0] = x * lax.rsqrt(ms + RMS_EPS) * g_ref[...]


def _final_call(x, g, *, tb):
    b, s, d = x.shape
    return pl.pallas_call(
        _final_kernel,
        grid=(b, s // tb),
        in_specs=[pl.BlockSpec((1, tb, d), lambda bi, i: (bi, i, 0)),
                  pl.BlockSpec((1, d), lambda bi, i: (0, 0))],
        out_specs=pl.BlockSpec((1, tb, d), lambda bi, i: (bi, i, 0)),
        out_shape=jax.ShapeDtypeStruct(x.shape, F32),
        name="final_norm",
    )(x, g.reshape(1, d))


def _pick(n, pref):
    return pref if n % pref == 0 else n


def kernel(x, c, ada_w, ada_b, norm_mix_g, norm_ffn_g, gla_w_in, gla_w_g2, gla_b_g2, gla_norm_g,
           gla_w_out, gdn_w_in, gdn_conv_w, gdn_a_log, gdn_dt_bias, gdn_norm_g, gdn_w_out,
           peer_wq, peer_k1, peer_k2, peer_u, peer_v, final_norm_g):
    b, s, d = x.shape
    depth = ada_w.shape[0]
    c_pad = jnp.pad(c, ((0, SUBLANE - b % SUBLANE if b % SUBLANE else 0), (0, 0)))
    mod = _mod_call(c_pad, ada_w, ada_b)[:, :b, :].reshape(depth, b, 6, 1, d)
    tb_mix = _pick(s, 256)
    tt = _pick(s, 512)
    eb = 1024
    for i in range(depth):
        sh1, sc1, gt1, sh2, sc2, gt2 = (mod[i, :, j] for j in range(6))
        j = i // 2
        if i % 2 == 0:
            x = _gla_call(x, sh1, sc1, gt1, norm_mix_g[i], gla_w_in[j], gla_w_g2[j], gla_b_g2[j],
                          gla_norm_g[j], gla_w_out[j], tb=tb_mix)
        else:
            x = _gdn_call(x, sh1, sc1, gt1, norm_mix_g[i], gdn_w_in[j], gdn_conv_w[j], gdn_a_log[j],
                          gdn_dt_bias[j], gdn_norm_g[j], gdn_w_out[j], tb=tb_mix)
        x = _peer_call(x, sh2, sc2, gt2, norm_ffn_g[i], peer_wq[i], peer_k1[i], peer_k2[i],
                       peer_u[i], peer_v[i], tt=tt, eb=eb)
    return _final_call(x, final_norm_g, tb=_pick(s, 512))
```

```python
import functools

import jax
import jax.numpy as jnp
from jax import lax
from jax.experimental import pallas as pl
from jax.experimental.pallas import tpu as pltpu

F32 = jnp.float32
BF16 = jnp.bfloat16
HI = lax.Precision.HIGHEST

RMS_EPS = 1e-6
CHUNK = 64

GLA_HEADS = 4
GLA_GATE_RANK = 16
GLA_TAU = 16.0

GDN_QK_HEADS = 8
GDN_V_HEADS = 16
GDN_HEAD_DIM = 128
GDN_CONV = 4

PEER_HEADS = 8
PEER_TOPK = 16
PEER_NKEYS = 128
PEER_HALF = 128
PEER_NCOLS = 256
PEER_MROWS = 512
PEER_GATE_PER_PIECE = 4

LANE = 128
SUBLANE = 8
VMEM_LIMIT = 56 * 1024 * 1024

NT = (((1,), (1,)), ((), ()))


def _sigmoid(x):
    return 1.0 / (1.0 + jnp.exp(-x))


def _silu(x):
    return x * _sigmoid(x)


def _softplus(x):
    return jnp.maximum(x, 0.0) + jnp.log1p(jnp.exp(-jnp.abs(x)))


def _gelu_tanh(x):
    c = 0.7978845608028654
    hx = 0.5 * x
    return hx + hx * jnp.tanh(x * (c + (c * 0.044715) * (x * x)))


def _norm_mod(x, g, sc, sh):
    ms = jnp.mean(x * x, axis=-1, keepdims=True)
    return (x * lax.rsqrt(ms + RMS_EPS) * g) * (1.0 + sc) + sh


def _bdot(a, b):
    return jnp.dot(a.astype(BF16), b.astype(BF16), preferred_element_type=F32)


def _bdot_nt(a, b):
    return lax.dot_general(a.astype(BF16), b.astype(BF16), NT, preferred_element_type=F32)


def _hdot(a, b):
    return jnp.dot(a, b, precision=HI, preferred_element_type=F32)


def _tril(n, k=0):
    r = lax.broadcasted_iota(jnp.int32, (n, n), 0)
    c = lax.broadcasted_iota(jnp.int32, (n, n), 1)
    return (c - r) <= k


def _mod_kernel(c_ref, w_ref, b_ref, o_ref):
    c = c_ref[...]
    o_ref[0] = _hdot(_silu(c), w_ref[0]) + b_ref[0]


def _mod_call(c_pad, ada_w, ada_b):
    depth, d, d6 = ada_w.shape
    nb = d6 // d
    return pl.pallas_call(
        _mod_kernel,
        grid=(depth, nb),
        in_specs=[
            pl.BlockSpec(c_pad.shape, lambda i, j: (0, 0)),
            pl.BlockSpec((1, d, d), lambda i, j: (i, 0, j)),
            pl.BlockSpec((1, 1, d), lambda i, j: (i, 0, j)),
        ],
        out_specs=pl.BlockSpec((1, c_pad.shape[0], d), lambda i, j: (i, 0, j)),
        out_shape=jax.ShapeDtypeStruct((depth, c_pad.shape[0], d6), F32),
        name="adaln_mod",
    )(c_pad, ada_w, ada_b.reshape(depth, 1, d6))


def _gla_kernel(x_ref, sh_ref, sc_ref, gt_ref, ng_ref, wqkvr_ref, wg1_ref, wg2_ref, bg2_ref,
                hg_ref, wout_ref, o_ref, st_ref, p_ref, la_ref, ob_ref, *, tb, dk, dv):
    hk = dk // GLA_HEADS
    hv = dv // GLA_HEADS
    nc = tb // CHUNK

    @pl.when(pl.program_id(1) == 0)
    def _():
        st_ref[...] = jnp.zeros_like(st_ref)

    x = x_ref[0]
    h = _norm_mod(x, ng_ref[...], sc_ref[0], sh_ref[0])
    hb = h.astype(BF16)
    p_ref[...] = jnp.dot(hb, wqkvr_ref[...], preferred_element_type=F32)
    g1 = jnp.dot(hb, wg1_ref[...], preferred_element_type=F32)
    g = jnp.dot(g1.astype(BF16), wg2_ref[...], preferred_element_type=F32) + bg2_ref[...]
    la_ref[...] = (jnp.minimum(g, 0.0) - jnp.log1p(jnp.exp(-jnp.abs(g)))) * (1.0 / GLA_TAU)

    tril = _tril(CHUNK).astype(F32)
    causal = _tril(CHUNK)
    scale = hk ** -0.5

    for c in range(nc):
        rows = slice(c * CHUNK, (c + 1) * CHUNK)
        cum = _hdot(tril, la_ref[rows, :])
        cl = cum[CHUNK - 1:CHUNK, :]
        e_pos = jnp.exp(cum)
        e_neg = jnp.exp(-cum)
        e_end = jnp.exp(cl - cum)
        dec = jnp.exp(cl)
        hr = range(GLA_HEADS)
        ksl = [slice(hd * hk, (hd + 1) * hk) for hd in hr]
        qis = [p_ref[rows, hd * hk:(hd + 1) * hk] * (scale * e_pos[:, ksl[hd]]) for hd in hr]
        kraw = [p_ref[rows, dk + hd * hk:dk + (hd + 1) * hk] for hd in hr]
        vs = [p_ref[rows, 2 * dk + hd * hv:2 * dk + (hd + 1) * hv] for hd in hr]
        attns = [jnp.where(causal, _bdot_nt(qis[hd], kraw[hd] * e_neg[:, ksl[hd]]), 0.0) for hd in hr]
        sts = [st_ref[hd] for hd in hr]
        inter = [_bdot_nt(qis[hd], sts[hd]) for hd in hr]
        upds = [_bdot(vs[hd].T, kraw[hd] * e_end[:, ksl[hd]]) for hd in hr]
        intra = [_bdot(attns[hd], vs[hd]) for hd in hr]
        for hd in hr:
            ob_ref[rows, hd * hv:(hd + 1) * hv] = intra[hd] + inter[hd]
            st_ref[hd] = dec[:, ksl[hd]] * sts[hd] + upds[hd]

    hg = hg_ref[...]
    parts = []
    for hd in range(GLA_HEADS):
        oh = ob_ref[:, hd * hv:(hd + 1) * hv]
        ms = jnp.mean(oh * oh, axis=-1, keepdims=True)
        r = p_ref[:, 2 * dk + dv + hd * hv:2 * dk + dv + (hd + 1) * hv]
        parts.append((oh * lax.rsqrt(ms + RMS_EPS) * hg * _silu(r)).astype(BF16))
    y = jnp.concatenate(parts, axis=-1)
    m = jnp.dot(y, wout_ref[...], preferred_element_type=F32)
    o_ref[0] = x + gt_ref[0] * m


def _gla_call(x, sh, sc, gt, ng, w_in, w_g2, b_g2, head_g, w_out, *, tb):
    b, s, d = x.shape
    dk = w_g2.shape[1]
    dv = w_out.shape[0]
    wqkvr = w_in[:, :2 * dk + 2 * dv].astype(BF16)
    wg1 = jnp.pad(w_in[:, 2 * dk + 2 * dv:], ((0, 0), (0, LANE - GLA_GATE_RANK))).astype(BF16)
    wg2 = jnp.pad(w_g2, ((0, LANE - GLA_GATE_RANK), (0, 0))).astype(BF16)
    hv = dv // GLA_HEADS
    hk = dk // GLA_HEADS
    const2 = lambda bi, i: (0, 0)
    vec = pl.BlockSpec((1, 1, d), lambda bi, i: (bi, 0, 0))
    kern = functools.partial(_gla_kernel, tb=tb, dk=dk, dv=dv)
    return pl.pallas_call(
        kern,
        grid=(b, s // tb),
        in_specs=[
            pl.BlockSpec((1, tb, d), lambda bi, i: (bi, i, 0)),
            vec, vec, vec,
            pl.BlockSpec((1, d), const2),
            pl.BlockSpec(wqkvr.shape, const2),
            pl.BlockSpec(wg1.shape, const2),
            pl.BlockSpec(wg2.shape, const2),
            pl.BlockSpec((1, dk), const2),
            pl.BlockSpec((1, hv), const2),
            pl.BlockSpec((dv, d), const2),
        ],
        out_specs=pl.BlockSpec((1, tb, d), lambda bi, i: (bi, i, 0)),
        out_shape=jax.ShapeDtypeStruct(x.shape, F32),
        scratch_shapes=[
            pltpu.VMEM((GLA_HEADS, hv, hk), F32),
            pltpu.VMEM((tb, 2 * dk + 2 * dv), F32),
            pltpu.VMEM((tb, dk), F32),
            pltpu.VMEM((tb, dv), F32),
        ],
        compiler_params=pltpu.CompilerParams(
            dimension_semantics=("arbitrary", "arbitrary"), vmem_limit_bytes=VMEM_LIMIT),
        name="gla_layer",
    )(x, sh, sc, gt, ng.reshape(1, d), wqkvr, wg1, wg2, b_g2.reshape(1, dk),
      head_g.reshape(1, hv), w_out.astype(BF16))


def _block_inverse_masks(n):
    r = lax.broadcasted_iota(jnp.int32, (n, n), 0)
    c = lax.broadcasted_iota(jnp.int32, (n, n), 1)
    eye = (r == c).astype(F32)
    first = (r // 2) == (c // 2)
    quads = []
    s = 2
    while s < n:
        quads.append(((r // (2 * s)) == (c // (2 * s))) & ((r // s) != (c // s)))
        s *= 2
    return eye, first, quads


def _unit_lower_inverse(a_list, masks):
    eye, first, quads = masks
    ds = [eye - jnp.where(first, a, 0.0) for a in a_list]
    for quad in quads:
        ms = [_bdot(d, jnp.where(quad, a, 0.0)) for d, a in zip(ds, a_list)]
        ds = [d - _bdot(m, d) for d, m in zip(ds, ms)]
    return ds


def _gdn_kernel(x_ref, sh_ref, sc_ref, gt_ref, ng_ref, wqkv_ref, wz_ref, wa_ref, wb_ref, cw_ref,
                alog_ref, dtb_ref, hg_ref, wout_ref, o_ref,
                s_ref, cb_ref, q_ref, k_ref, v_ref, z_ref, g_ref, bt_ref, ob_ref, *, tb):
    dqk = GDN_QK_HEADS * GDN_HEAD_DIM
    dvv = GDN_V_HEADS * GDN_HEAD_DIM
    hd_ = GDN_HEAD_DIM
    nc = tb // CHUNK
    rep = GDN_V_HEADS // GDN_QK_HEADS

    @pl.when(pl.program_id(1) == 0)
    def _():
        s_ref[...] = jnp.zeros_like(s_ref)
        cb_ref[0:SUBLANE, :] = jnp.zeros((SUBLANE, cb_ref.shape[1]), F32)

    x = x_ref[0]
    h = _norm_mod(x, ng_ref[...], sc_ref[0], sh_ref[0])
    hb = h.astype(BF16)
    cb_ref[SUBLANE:SUBLANE + tb, :] = jnp.dot(hb, wqkv_ref[...], preferred_element_type=F32)
    z_ref[...] = jnp.dot(hb, wz_ref[...], preferred_element_type=F32)
    a = jnp.dot(hb, wa_ref[...], preferred_element_type=F32)
    bb = jnp.dot(hb, wb_ref[...], preferred_element_type=F32)
    g_ref[...] = -jnp.exp(alog_ref[...]) * _softplus(a + dtb_ref[...])
    bt_ref[...] = _sigmoid(bb)

    off = SUBLANE - (GDN_CONV - 1)
    conv = cw_ref[0:1, :] * cb_ref[off:off + tb, :]
    for j in range(1, GDN_CONV):
        conv = conv + cw_ref[j:j + 1, :] * cb_ref[off + j:off + j + tb, :]
    cb_ref[0:SUBLANE, :] = cb_ref[tb:tb + SUBLANE, :]
    qkv = _silu(conv)
    qscale = hd_ ** -0.5
    for hq in range(GDN_QK_HEADS):
        sl = slice(hq * hd_, (hq + 1) * hd_)
        qh = qkv[:, hq * hd_:(hq + 1) * hd_]
        kh = qkv[:, dqk + hq * hd_:dqk + (hq + 1) * hd_]
        q_ref[:, sl] = qh * (lax.rsqrt(jnp.sum(qh * qh, axis=-1, keepdims=True) + 1e-6) * qscale)
        k_ref[:, sl] = kh * lax.rsqrt(jnp.sum(kh * kh, axis=-1, keepdims=True) + 1e-6)
    v_ref[...] = qkv[:, 2 * dqk:]

    tril = _tril(CHUNK).astype(F32)
    lower = _tril(CHUNK)
    strict = _tril(CHUNK, -1)
    inv_masks = _block_inverse_masks(CHUNK)

    for c in range(nc):
        rows = slice(c * CHUNK, (c + 1) * CHUNK)
        gc = _hdot(tril, g_ref[rows, :])
        gct = gc.T
        gl = gc[CHUNK - 1:CHUNK, :]
        eg = jnp.exp(gc)
        eendt = jnp.exp(gl - gc).T
        dec = jnp.exp(gl)
        bt = bt_ref[rows, :]
        beg = bt * eg
        hq_r = range(GDN_QK_HEADS)
        hv_r = range(GDN_V_HEADS)
        qs = [q_ref[rows, hq * hd_:(hq + 1) * hd_] for hq in hq_r]
        ks = [k_ref[rows, hq * hd_:(hq + 1) * hd_] for hq in hq_r]
        kks = [_bdot_nt(ks[hq], ks[hq]) for hq in hq_r]
        qk0s = [_bdot_nt(qs[hq], ks[hq]) for hq in hq_r]
        khts = [ks[hq].T for hq in hq_r]
        cols = [slice(hvi, hvi + 1) for hvi in hv_r]
        decays = []
        for hvi in hv_r:
            diff = gc[:, cols[hvi]] - gct[hvi:hvi + 1, :]
            decays.append(jnp.where(lower, jnp.exp(jnp.where(lower, diff, 0.0)), 0.0))
        amats = [jnp.where(strict, bt[:, cols[hvi]] * kks[hvi // rep] * decays[hvi], 0.0) for hvi in hv_r]
        tinvs = _unit_lower_inverse(amats, inv_masks)
        rhs = [jnp.concatenate([bt[:, cols[hvi]] * v_ref[rows, hvi * hd_:(hvi + 1) * hd_],
                                beg[:, cols[hvi]] * ks[hvi // rep]], axis=1) for hvi in hv_r]
        uws = [_bdot(tinvs[hvi], rhs[hvi]) for hvi in hv_r]
        ss = [s_ref[hvi] for hvi in hv_r]
        wss = [_bdot(uws[hvi][:, hd_:], ss[hvi]) for hvi in hv_r]
        qss = [_bdot(qs[hvi // rep] * eg[:, cols[hvi]], ss[hvi]) for hvi in hv_r]
        vns = [uws[hvi][:, :hd_] - wss[hvi] for hvi in hv_r]
        outs = [qss[hvi] + _bdot(qk0s[hvi // rep] * decays[hvi], vns[hvi]) for hvi in hv_r]
        upds = [_bdot(khts[hvi // rep] * eendt[hvi:hvi + 1, :], vns[hvi]) for hvi in hv_r]
        for hvi in hv_r:
            ob_ref[rows, hvi * hd_:(hvi + 1) * hd_] = outs[hvi]
            s_ref[hvi] = dec[:, cols[hvi]] * ss[hvi] + upds[hvi]

    hg = hg_ref[...]
    parts = []
    for hvi in range(GDN_V_HEADS):
        vs = slice(hvi * hd_, (hvi + 1) * hd_)
        oh = ob_ref[:, vs]
        ms = jnp.mean(oh * oh, axis=-1, keepdims=True)
        parts.append((oh * lax.rsqrt(ms + RMS_EPS) * hg * _silu(z_ref[:, vs])).astype(BF16))
    y = jnp.concatenate(parts, axis=-1)
    m = jnp.dot(y, wout_ref[...], preferred_element_type=F32)
    o_ref[0] = x + gt_ref[0] * m


def _gdn_call(x, sh, sc, gt, ng, w_in, conv_w, a_log, dt_bias, head_g, w_out, *, tb):
    b, s, d = x.shape
    dqk = GDN_QK_HEADS * GDN_HEAD_DIM
    dvv = GDN_V_HEADS * GDN_HEAD_DIM
    nqkv = 2 * dqk + dvv
    wqkv = w_in[:, :nqkv].astype(BF16)
    wz = w_in[:, nqkv:nqkv + dvv].astype(BF16)
    padh = ((0, 0), (0, LANE - GDN_V_HEADS))
    wa = jnp.pad(w_in[:, nqkv + dvv:nqkv + dvv + GDN_V_HEADS], padh).astype(BF16)
    wb = jnp.pad(w_in[:, nqkv + dvv + GDN_V_HEADS:], padh).astype(BF16)
    alog = jnp.pad(a_log.reshape(1, -1), padh)
    dtb = jnp.pad(dt_bias.reshape(1, -1), padh)
    const2 = lambda bi, i: (0, 0)
    vec = pl.BlockSpec((1, 1, d), lambda bi, i: (bi, 0, 0))
    kern = functools.partial(_gdn_kernel, tb=tb)
    return pl.pallas_call(
        kern,
        grid=(b, s // tb),
        in_specs=[
            pl.BlockSpec((1, tb, d), lambda bi, i: (bi, i, 0)),
            vec, vec, vec,
            pl.BlockSpec((1, d), const2),
            pl.BlockSpec(wqkv.shape, const2),
            pl.BlockSpec(wz.shape, const2),
            pl.BlockSpec(wa.shape, const2),
            pl.BlockSpec(wb.shape, const2),
            pl.BlockSpec(conv_w.shape, const2),
            pl.BlockSpec((1, LANE), const2),
            pl.BlockSpec((1, LANE), const2),
            pl.BlockSpec((1, GDN_HEAD_DIM), const2),
            pl.BlockSpec((dvv, d), const2),
        ],
        out_specs=pl.BlockSpec((1, tb, d), lambda bi, i: (bi, i, 0)),
        out_shape=jax.ShapeDtypeStruct(x.shape, F32),
        scratch_shapes=[
            pltpu.VMEM((GDN_V_HEADS, GDN_HEAD_DIM, GDN_HEAD_DIM), F32),
            pltpu.VMEM((tb + SUBLANE, nqkv), F32),
            pltpu.VMEM((tb, dqk), F32),
            pltpu.VMEM((tb, dqk), F32),
            pltpu.VMEM((tb, dvv), F32),
            pltpu.VMEM((tb, dvv), F32),
            pltpu.VMEM((tb, LANE), F32),
            pltpu.VMEM((tb, LANE), F32),
            pltpu.VMEM((tb, dvv), F32),
        ],
        compiler_params=pltpu.CompilerParams(
            dimension_semantics=("arbitrary", "arbitrary"), vmem_limit_bytes=VMEM_LIMIT),
        name="gdn_layer",
    )(x, sh, sc, gt, ng.reshape(1, d), wqkv, wz, wa, wb, conv_w, alog, dtb,
      head_g.reshape(1, GDN_HEAD_DIM), w_out.astype(BF16))


def _top_values(x, n):
    vals = []
    for r in range(n):
        m = jnp.max(x, axis=0, keepdims=True)
        vals.append(m)
        if r + 1 < n:
            x = jnp.where(x == m, -jnp.inf, x)
    return vals


def _stack_rows(rows):
    l = rows[0].shape[1]
    sub = lax.broadcasted_iota(jnp.int32, (SUBLANE, l), 0)
    out = jnp.broadcast_to(rows[0], (SUBLANE, l))
    for i in range(1, len(rows)):
        out = jnp.where(sub == i, rows[i], out)
    return out


def _peer_kernel(x_ref, sh_ref, sc_ref, gt_ref, ng_ref, wq_ref, k1_ref, k2_ref, u_ref, vt_ref, o_ref,
                 tt_ref, s1_ref, w1_ref, s2_ref, e2_ref, c2_ref, yt_ref, *half_refs, tt, eb, ne):
    g = pl.program_id(2)
    nh = tt // PEER_NCOLS
    ht_refs = half_refs[:nh]
    at_refs = half_refs[nh:]
    nj = tt // LANE
    ab = eb // PEER_NKEYS
    k = PEER_TOPK

    @pl.when(g == 0)
    def _():
        x = x_ref[0]
        t = _norm_mod(x, ng_ref[...], sc_ref[0], sh_ref[0])
        tb = t.astype(BF16)
        tt_ref[...] = t.T.astype(BF16)
        q = jnp.dot(tb, wq_ref[...], preferred_element_type=F32)
        for h in range(PEER_HEADS):
            q1 = q[:, (2 * h) * PEER_HALF:(2 * h + 1) * PEER_HALF]
            q2 = q[:, (2 * h + 1) * PEER_HALF:(2 * h + 2) * PEER_HALF]
            s1_ref[h] = lax.dot_general(k1_ref[...], q1, NT, precision=HI, preferred_element_type=F32)
            s2_ref[h] = lax.dot_general(k2_ref[...], q2, NT, precision=HI, preferred_element_type=F32)
        yt_ref[...] = jnp.zeros_like(yt_ref)
        for ht_ref in ht_refs:
            ht_ref[...] = jnp.zeros_like(ht_ref)

        def route(idx, carry):
            h = idx // nj
            c0 = pl.multiple_of((idx % nj) * LANE, LANE)
            cols = pl.ds(c0, LANE)
            x1 = s1_ref[h, :, cols]
            x2 = s2_ref[h, :, cols]
            v1 = _top_values(x1, k + 1)
            v2 = _top_values(x2, k + 1)
            v2lo = _stack_rows(v2[:SUBLANE])
            v2hi = _stack_rows(v2[SUBLANE:2 * SUBLANE])
            v1hi = _stack_rows(v1[SUBLANE:2 * SUBLANE])
            sub = lax.broadcasted_iota(jnp.int32, (SUBLANE, LANE), 0)
            last = jnp.where(sub == 0, v1[0] + v2[k], jnp.where(sub == 1, v1[k] + v2[0], -jnp.inf))
            tiles = [v1[0] + v2lo, v1[0] + v2hi, v1hi + v2[0], last]
            tiles += [v1[i] + v2lo for i in range(1, SUBLANE)]
            cand = jnp.concatenate(tiles, axis=0)
            tops = _top_values(cand, k + 1)
            kth = tops[k - 1]
            cut = 0.5 * (kth + tops[k])
            top = v1[0] + v2[0]
            z = jnp.sum(jnp.where(cand >= kth, jnp.exp(cand - top), 0.0), axis=0, keepdims=True)
            w1_ref[h, :, cols] = jnp.exp(x1 - v1[0]) / z
            e2_ref[h, :, cols] = jnp.exp(x2 - v2[0])
            c2_ref[h, :, cols] = cut - x2
            return carry

        lax.fori_loop(0, PEER_HEADS * nj, route, 0)

    em = jnp.clip(g - 1, 0, ne - 1)
    arows = pl.ds(pl.multiple_of(em * ab, SUBLANE), SUBLANE)
    bh = PEER_NKEYS // 2
    d = yt_ref.shape[0]
    ngroup = PEER_NCOLS // LANE

    def gate_block(n, jj, a2, hb):
        cols = slice(n * PEER_NCOLS + jj * LANE, n * PEER_NCOLS + (jj + 1) * LANE)
        lcols = slice(jj * LANE, (jj + 1) * LANE)
        brows = slice(hb * bh, (hb + 1) * bh)
        als = (2 * a2, 2 * a2 + 1)
        s1t = [s1_ref[h, arows, cols] for h in range(PEER_HEADS)]
        w1t = [w1_ref[h, arows, cols] for h in range(PEER_HEADS)]
        accs = [jnp.zeros((bh, LANE), F32) for _ in als]
        for h in range(PEER_HEADS):
            c2 = c2_ref[h, brows, cols]
            e2 = e2_ref[h, brows, cols]
            for i, al in enumerate(als):
                sel = s1t[h][al:al + 1, :] >= c2
                accs[i] = accs[i] + jnp.where(sel, e2 * w1t[h][al:al + 1, :], 0.0)
        fold = None
        for i, al in enumerate(als):
            rows = slice(al * PEER_NKEYS + hb * bh, al * PEER_NKEYS + (hb + 1) * bh)
            out = accs[i] * _gelu_tanh(ht_refs[n][rows, lcols])
            at_refs[n][rows, lcols] = out.astype(BF16)
            bits = lax.bitcast_convert_type(out, jnp.int32)
            fold = bits if fold is None else fold | bits
        while fold.shape[0] > SUBLANE:
            half = fold.shape[0] // 2
            fold = fold[:half] | fold[half:]
        return fold & jnp.minimum(g, 0)

    def after(zero, rhs):
        head = pltpu.bitcast(rhs[:PEER_NCOLS], jnp.int32)
        head = head | jnp.tile(zero, (head.shape[0] // SUBLANE, head.shape[1] // LANE))
        return jnp.concatenate([pltpu.bitcast(head, BF16), rhs[PEER_NCOLS:]], axis=0)

    def matmul_pieces(n):
        ncols = slice(n * PEER_NCOLS, (n + 1) * PEER_NCOLS)
        pieces = []
        for m in range(d // PEER_MROWS):
            mrows = slice(m * PEER_MROWS, (m + 1) * PEER_MROWS)

            def second(zero, mrows=mrows):
                yt_ref[mrows, ncols] += jnp.dot(vt_ref[0, mrows, :], after(zero, at_refs[n][...]),
                                                preferred_element_type=F32)

            def first(zero, mrows=mrows):
                ht_refs[n][mrows, :] = jnp.dot(u_ref[mrows, :], after(zero, tt_ref[:, ncols]),
                                               preferred_element_type=F32)
            pieces += [second, first]
        return pieces

    pending = []
    for n in range(tt // PEER_NCOLS):
        blk = 0
        for jj in range(ngroup):
            for a2 in range(ab // 2):
                for hb in range(2):
                    zero = gate_block(n, jj, a2, hb)
                    if pending and blk % PEER_GATE_PER_PIECE == 1:
                        pending.pop(0)(zero)
                    blk += 1
        for piece in pending:
            piece(zero)
        pending = matmul_pieces(n)
    for piece in pending:
        piece(zero)

    @pl.when(g == ne)
    def _():
        o_ref[0] = x_ref[0] + gt_ref[0] * yt_ref[...].T


def _peer_call(x, sh, sc, gt, ng, wq, k1, k2, u, v, *, tt, eb):
    b, s, d = x.shape
    assert eb == SUBLANE * PEER_NKEYS and u.shape[0] % eb == 0
    ne = u.shape[0] // eb
    ub = u.astype(BF16)
    vtb = v.astype(BF16).reshape(ne, eb, d).transpose(0, 2, 1)
    const3 = lambda bi, i, e: (0, 0)
    vec = pl.BlockSpec((1, 1, d), lambda bi, i, e: (bi, 0, 0))
    kern = functools.partial(_peer_kernel, tt=tt, eb=eb, ne=ne)
    sshape = (PEER_HEADS, PEER_NKEYS, tt)
    return pl.pallas_call(
        kern,
        grid=(b, s // tt, ne + 1),
        in_specs=[
            pl.BlockSpec((1, tt, d), lambda bi, i, e: (bi, i, 0)),
            vec, vec, vec,
            pl.BlockSpec((1, d), const3),
            pl.BlockSpec(wq.shape, const3),
            pl.BlockSpec(k1.shape, const3),
            pl.BlockSpec(k2.shape, const3),
            pl.BlockSpec((eb, d), lambda bi, i, g: (jnp.minimum(g, ne - 1), 0)),
            pl.BlockSpec((1, d, eb), lambda bi, i, g: (jnp.maximum(g - 1, 0), 0, 0)),
        ],
        out_specs=pl.BlockSpec((1, tt, d), lambda bi, i, e: (bi, i, 0)),
        out_shape=jax.ShapeDtypeStruct(x.shape, F32),
        scratch_shapes=[
            pltpu.VMEM((d, tt), BF16),
            pltpu.VMEM(sshape, F32),
            pltpu.VMEM(sshape, F32),
            pltpu.VMEM(sshape, F32),
            pltpu.VMEM(sshape, F32),
            pltpu.VMEM(sshape, F32),
            pltpu.VMEM((d, tt), F32),
        ] + [pltpu.VMEM((eb, PEER_NCOLS), F32)] * (tt // PEER_NCOLS)
          + [pltpu.VMEM((eb, PEER_NCOLS), BF16)] * (tt // PEER_NCOLS),
        compiler_params=pltpu.CompilerParams(
            dimension_semantics=("arbitrary", "arbitrary", "arbitrary"), vmem_limit_bytes=VMEM_LIMIT),
        name="peer_layer",
    )(x, sh, sc, gt, ng.reshape(1, d), wq.astype(BF16), k1, k2, ub, vtb)


def _final_kernel(x_ref, g_ref, o_ref):
    x = x_ref[0]
    ms = jnp.mean(x * x, axis=-1, keepdims=True)
    o_ref[0] = x * lax.rsqrt(ms + RMS_EPS) * g_ref[...]


def _final_call(x, g, *, tb):
    b, s, d = x.shape
    return pl.pallas_call(
        _final_kernel,
        grid=(b, s // tb),
        in_specs=[pl.BlockSpec((1, tb, d), lambda bi, i: (bi, i, 0)),
                  pl.BlockSpec((1, d), lambda bi, i: (0, 0))],
        out_specs=pl.BlockSpec((1, tb, d), lambda bi, i: (bi, i, 0)),
        out_shape=jax.ShapeDtypeStruct(x.shape, F32),
        name="final_norm",
    )(x, g.reshape(1, d))


def _pick(n, pref):
    return pref if n % pref == 0 else n


def kernel(x, c, ada_w, ada_b, norm_mix_g, norm_ffn_g, gla_w_in, gla_w_g2, gla_b_g2, gla_norm_g,
           gla_w_out, gdn_w_in, gdn_conv_w, gdn_a_log, gdn_dt_bias, gdn_norm_g, gdn_w_out,
           peer_wq, peer_k1, peer_k2, peer_u, peer_v, final_norm_g):
    b, s, d = x.shape
    depth = ada_w.shape[0]
    c_pad = jnp.pad(c, ((0, SUBLANE - b % SUBLANE if b % SUBLANE else 0), (0, 0)))
    mod = _mod_call(c_pad, ada_w, ada_b)[:, :b, :].reshape(depth, b, 6, 1, d)
    tb_mix = _pick(s, 256)
    tt = _pick(s, 512)
    eb = 1024
    for i in range(depth):
        sh1, sc1, gt1, sh2, sc2, gt2 = (mod[i, :, j] for j in range(6))
        j = i // 2
        if i % 2 == 0:
            x = _gla_call(x, sh1, sc1, gt1, norm_mix_g[i], gla_w_in[j], gla_w_g2[j], gla_b_g2[j],
                          gla_norm_g[j], gla_w_out[j], tb=tb_mix)
        else:
            x = _gdn_call(x, sh1, sc1, gt1, norm_mix_g[i], gdn_w_in[j], gdn_conv_w[j], gdn_a_log[j],
                          gdn_dt_bias[j], gdn_norm_g[j], gdn_w_out[j], tb=tb_mix)
        x = _peer_call(x, sh2, sc2, gt2, norm_ffn_g[i], peer_wq[i], peer_k1[i], peer_k2[i],
                       peer_u[i], peer_v[i], tt=tt, eb=eb)
    return _final_call(x, final_norm_g, tb=_pick(s, 512))
```

```python
import functools

import jax
import jax.numpy as jnp
from jax import lax
from jax.experimental import pallas as pl
from jax.experimental.pallas import tpu as pltpu

F32 = jnp.float32
BF16 = jnp.bfloat16
HI = lax.Precision.HIGHEST

RMS_EPS = 1e-6
CHUNK = 64

GLA_HEADS = 4
GLA_GATE_RANK = 16
GLA_TAU = 16.0

GDN_QK_HEADS = 8
GDN_V_HEADS = 16
GDN_HEAD_DIM = 128
GDN_CONV = 4

PEER_HEADS = 8
PEER_TOPK = 16
PEER_NKEYS = 128
PEER_HALF = 128
PEER_NCOLS = 256
PEER_MROWS = 512
PEER_GATE_PER_PIECE = 4

LANE = 128
SUBLANE = 8
VMEM_LIMIT = 56 * 1024 * 1024

NT = (((1,), (1,)), ((), ()))


def _sigmoid(x):
    return 1.0 / (1.0 + jnp.exp(-x))


def _silu(x):
    return x * _sigmoid(x)


def _softplus(x):
    return jnp.maximum(x, 0.0) + jnp.log1p(jnp.exp(-jnp.abs(x)))


def _gelu_tanh(x):
    c = 0.7978845608028654
    hx = 0.5 * x
    return hx + hx * jnp.tanh(x * (c + (c * 0.044715) * (x * x)))


def _norm_mod(x, g, sc, sh):
    ms = jnp.mean(x * x, axis=-1, keepdims=True)
    return (x * lax.rsqrt(ms + RMS_EPS) * g) * (1.0 + sc) + sh


def _bdot(a, b):
    return jnp.dot(a.astype(BF16), b.astype(BF16), preferred_element_type=F32)


def _bdot_nt(a, b):
    return lax.dot_general(a.astype(BF16), b.astype(BF16), NT, preferred_element_type=F32)


def _hdot(a, b):
    return jnp.dot(a, b, precision=HI, preferred_element_type=F32)


def _tril(n, k=0):
    r = lax.broadcasted_iota(jnp.int32, (n, n), 0)
    c = lax.broadcasted_iota(jnp.int32, (n, n), 1)
    return (c - r) <= k


def _mod_kernel(c_ref, w_ref, b_ref, o_ref):
    c = c_ref[...]
    o_ref[0] = _hdot(_silu(c), w_ref[0]) + b_ref[0]


def _mod_call(c_pad, ada_w, ada_b):
    depth, d, d6 = ada_w.shape
    nb = d6 // d
    return pl.pallas_call(
        _mod_kernel,
        grid=(depth, nb),
        in_specs=[
            pl.BlockSpec(c_pad.shape, lambda i, j: (0, 0)),
            pl.BlockSpec((1, d, d), lambda i, j: (i, 0, j)),
            pl.BlockSpec((1, 1, d), lambda i, j: (i, 0, j)),
        ],
        out_specs=pl.BlockSpec((1, c_pad.shape[0], d), lambda i, j: (i, 0, j)),
        out_shape=jax.ShapeDtypeStruct((depth, c_pad.shape[0], d6), F32),
        name="adaln_mod",
    )(c_pad, ada_w, ada_b.reshape(depth, 1, d6))


def _gla_kernel(x_ref, sh_ref, sc_ref, gt_ref, ng_ref, wqkvr_ref, wg1_ref, wg2_ref, bg2_ref,
                hg_ref, wout_ref, o_ref, st_ref, p_ref, la_ref, ob_ref, *, tb, dk, dv):
    hk = dk // GLA_HEADS
    hv = dv // GLA_HEADS
    nc = tb // CHUNK

    @pl.when(pl.program_id(1) == 0)
    def _():
        st_ref[...] = jnp.zeros_like(st_ref)

    x = x_ref[0]
    h = _norm_mod(x, ng_ref[...], sc_ref[0], sh_ref[0])
    hb = h.astype(BF16)
    p_ref[...] = jnp.dot(hb, wqkvr_ref[...], preferred_element_type=F32)
    g1 = jnp.dot(hb, wg1_ref[...], preferred_element_type=F32)
    g = jnp.dot(g1.astype(BF16), wg2_ref[...], preferred_element_type=F32) + bg2_ref[...]
    la_ref[...] = (jnp.minimum(g, 0.0) - jnp.log1p(jnp.exp(-jnp.abs(g)))) * (1.0 / GLA_TAU)

    tril = _tril(CHUNK).astype(F32)
    causal = _tril(CHUNK)
    scale = hk ** -0.5

    for c in range(nc):
        rows = slice(c * CHUNK, (c + 1) * CHUNK)
        cum = _hdot(tril, la_ref[rows, :])
        cl = cum[CHUNK - 1:CHUNK, :]
        e_pos = jnp.exp(cum)
        e_neg = jnp.exp(-cum)
        e_end = jnp.exp(cl - cum)
        dec = jnp.exp(cl)
        hr = range(GLA_HEADS)
        ksl = [slice(hd * hk, (hd + 1) * hk) for hd in hr]
        qis = [p_ref[rows, hd * hk:(hd + 1) * hk] * (scale * e_pos[:, ksl[hd]]) for hd in hr]
        kraw = [p_ref[rows, dk + hd * hk:dk + (hd + 1) * hk] for hd in hr]
        vs = [p_ref[rows, 2 * dk + hd * hv:2 * dk + (hd + 1) * hv] for hd in hr]
        attns = [jnp.where(causal, _bdot_nt(qis[hd], kraw[hd] * e_neg[:, ksl[hd]]), 0.0) for hd in hr]
        sts = [st_ref[hd] for hd in hr]
        inter = [_bdot_nt(qis[hd], sts[hd]) for hd in hr]
        upds = [_bdot(vs[hd].T, kraw[hd] * e_end[:, ksl[hd]]) for hd in hr]
        intra = [_bdot(attns[hd], vs[hd]) for hd in hr]
        for hd in hr:
            ob_ref[rows, hd * hv:(hd + 1) * hv] = intra[hd] + inter[hd]
            st_ref[hd] = dec[:, ksl[hd]] * sts[hd] + upds[hd]

    hg = hg_ref[...]
    parts = []
    for hd in range(GLA_HEADS):
        oh = ob_ref[:, hd * hv:(hd + 1) * hv]
        ms = jnp.mean(oh * oh, axis=-1, keepdims=True)
        r = p_ref[:, 2 * dk + dv + hd * hv:2 * dk + dv + (hd + 1) * hv]
        parts.append((oh * lax.rsqrt(ms + RMS_EPS) * hg * _silu(r)).astype(BF16))
    y = jnp.concatenate(parts, axis=-1)
    m = jnp.dot(y, wout_ref[...], preferred_element_type=F32)
    o_ref[0] = x + gt_ref[0] * m


def _gla_call(x, sh, sc, gt, ng, w_in, w_g2, b_g2, head_g, w_out, *, tb):
    b, s, d = x.shape
    dk = w_g2.shape[1]
    dv = w_out.shape[0]
    wqkvr = w_in[:, :2 * dk + 2 * dv].astype(BF16)
    wg1 = jnp.pad(w_in[:, 2 * dk + 2 * dv:], ((0, 0), (0, LANE - GLA_GATE_RANK))).astype(BF16)
    wg2 = jnp.pad(w_g2, ((0, LANE - GLA_GATE_RANK), (0, 0))).astype(BF16)
    hv = dv // GLA_HEADS
    hk = dk // GLA_HEADS
    const2 = lambda bi, i: (0, 0)
    vec = pl.BlockSpec((1, 1, d), lambda bi, i: (bi, 0, 0))
    kern = functools.partial(_gla_kernel, tb=tb, dk=dk, dv=dv)
    return pl.pallas_call(
        kern,
        grid=(b, s // tb),
        in_specs=[
            pl.BlockSpec((1, tb, d), lambda bi, i: (bi, i, 0)),
            vec, vec, vec,
            pl.BlockSpec((1, d), const2),
            pl.BlockSpec(wqkvr.shape, const2),
            pl.BlockSpec(wg1.shape, const2),
            pl.BlockSpec(wg2.shape, const2),
            pl.BlockSpec((1, dk), const2),
            pl.BlockSpec((1, hv), const2),
            pl.BlockSpec((dv, d), const2),
        ],
        out_specs=pl.BlockSpec((1, tb, d), lambda bi, i: (bi, i, 0)),
        out_shape=jax.ShapeDtypeStruct(x.shape, F32),
        scratch_shapes=[
            pltpu.VMEM((GLA_HEADS, hv, hk), F32),
            pltpu.VMEM((tb, 2 * dk + 2 * dv), F32),
            pltpu.VMEM((tb, dk), F32),
            pltpu.VMEM((tb, dv), F32),
        ],
        compiler_params=pltpu.CompilerParams(
            dimension_semantics=("arbitrary", "arbitrary"), vmem_limit_bytes=VMEM_LIMIT),
        name="gla_layer",
    )(x, sh, sc, gt, ng.reshape(1, d), wqkvr, wg1, wg2, b_g2.reshape(1, dk),
      head_g.reshape(1, hv), w_out.astype(BF16))


def _block_inverse_masks(n):
    r = lax.broadcasted_iota(jnp.int32, (n, n), 0)
    c = lax.broadcasted_iota(jnp.int32, (n, n), 1)
    eye = (r == c).astype(F32)
    first = (r // 2) == (c // 2)
    quads = []
    s = 2
    while s < n:
        quads.append(((r // (2 * s)) == (c // (2 * s))) & ((r // s) != (c // s)))
        s *= 2
    return eye, first, quads


def _unit_lower_inverse(a_list, masks):
    eye, first, quads = masks
    ds = [eye - jnp.where(first, a, 0.0) for a in a_list]
    for quad in quads:
        ms = [_bdot(d, jnp.where(quad, a, 0.0)) for d, a in zip(ds, a_list)]
        ds = [d - _bdot(m, d) for d, m in zip(ds, ms)]
    return ds


def _gdn_kernel(x_ref, sh_ref, sc_ref, gt_ref, ng_ref, wqkv_ref, wz_ref, wa_ref, wb_ref, cw_ref,
                alog_ref, dtb_ref, hg_ref, wout_ref, o_ref,
                s_ref, cb_ref, q_ref, k_ref, v_ref, z_ref, g_ref, bt_ref, ob_ref, *, tb):
    dqk = GDN_QK_HEADS * GDN_HEAD_DIM
    dvv = GDN_V_HEADS * GDN_HEAD_DIM
    hd_ = GDN_HEAD_DIM
    nc = tb // CHUNK
    rep = GDN_V_HEADS // GDN_QK_HEADS

    @pl.when(pl.program_id(1) == 0)
    def _():
        s_ref[...] = jnp.zeros_like(s_ref)
        cb_ref[0:SUBLANE, :] = jnp.zeros((SUBLANE, cb_ref.shape[1]), F32)

    x = x_ref[0]
    h = _norm_mod(x, ng_ref[...], sc_ref[0], sh_ref[0])
    hb = h.astype(BF16)
    cb_ref[SUBLANE:SUBLANE + tb, :] = jnp.dot(hb, wqkv_ref[...], preferred_element_type=F32)
    z_ref[...] = jnp.dot(hb, wz_ref[...], preferred_element_type=F32)
    a = jnp.dot(hb, wa_ref[...], preferred_element_type=F32)
    bb = jnp.dot(hb, wb_ref[...], preferred_element_type=F32)
    g_ref[...] = -jnp.exp(alog_ref[...]) * _softplus(a + dtb_ref[...])
    bt_ref[...] = _sigmoid(bb)

    off = SUBLANE - (GDN_CONV - 1)
    conv = cw_ref[0:1, :] * cb_ref[off:off + tb, :]
    for j in range(1, GDN_CONV):
        conv = conv + cw_ref[j:j + 1, :] * cb_ref[off + j:off + j + tb, :]
    cb_ref[0:SUBLANE, :] = cb_ref[tb:tb + SUBLANE, :]
    qkv = _silu(conv)
    qscale = hd_ ** -0.5
    for hq in range(GDN_QK_HEADS):
        sl = slice(hq * hd_, (hq + 1) * hd_)
        qh = qkv[:, hq * hd_:(hq + 1) * hd_]
        kh = qkv[:, dqk + hq * hd_:dqk + (hq + 1) * hd_]
        q_ref[:, sl] = qh * (lax.rsqrt(jnp.sum(qh * qh, axis=-1, keepdims=True) + 1e-6) * qscale)
        k_ref[:, sl] = kh * lax.rsqrt(jnp.sum(kh * kh, axis=-1, keepdims=True) + 1e-6)
    v_ref[...] = qkv[:, 2 * dqk:]

    tril = _tril(CHUNK).astype(F32)
    lower = _tril(CHUNK)
    strict = _tril(CHUNK, -1)
    inv_masks = _block_inverse_masks(CHUNK)

    for c in range(nc):
        rows = slice(c * CHUNK, (c + 1) * CHUNK)
        gc = _hdot(tril, g_ref[rows, :])
        gct = gc.T
        gl = gc[CHUNK - 1:CHUNK, :]
        eg = jnp.exp(gc)
        eendt = jnp.exp(gl - gc).T
        dec = jnp.exp(gl)
        bt = bt_ref[rows, :]
        beg = bt * eg
        hq_r = range(GDN_QK_HEADS)
        hv_r = range(GDN_V_HEADS)
        qs = [q_ref[rows, hq * hd_:(hq + 1) * hd_] for hq in hq_r]
        ks = [k_ref[rows, hq * hd_:(hq + 1) * hd_] for hq in hq_r]
        kks = [_bdot_nt(ks[hq], ks[hq]) for hq in hq_r]
        qk0s = [_bdot_nt(qs[hq], ks[hq]) for hq in hq_r]
        khts = [ks[hq].T for hq in hq_r]
        cols = [slice(hvi, hvi + 1) for hvi in hv_r]
        decays = []
        for hvi in hv_r:
            diff = gc[:, cols[hvi]] - gct[hvi:hvi + 1, :]
            decays.append(jnp.where(lower, jnp.exp(jnp.where(lower, diff, 0.0)), 0.0))
        amats = [jnp.where(strict, bt[:, cols[hvi]] * kks[hvi // rep] * decays[hvi], 0.0) for hvi in hv_r]
        tinvs = _unit_lower_inverse(amats, inv_masks)
        rhs = [jnp.concatenate([bt[:, cols[hvi]] * v_ref[rows, hvi * hd_:(hvi + 1) * hd_],
                                beg[:, cols[hvi]] * ks[hvi // rep]], axis=1) for hvi in hv_r]
        uws = [_bdot(tinvs[hvi], rhs[hvi]) for hvi in hv_r]
        ss = [s_ref[hvi] for hvi in hv_r]
        wss = [_bdot(uws[hvi][:, hd_:], ss[hvi]) for hvi in hv_r]
        qss = [_bdot(qs[hvi // rep] * eg[:, cols[hvi]], ss[hvi]) for hvi in hv_r]
        vns = [uws[hvi][:, :hd_] - wss[hvi] for hvi in hv_r]
        outs = [qss[hvi] + _bdot(qk0s[hvi // rep] * decays[hvi], vns[hvi]) for hvi in hv_r]
        upds = [_bdot(khts[hvi // rep] * eendt[hvi:hvi + 1, :], vns[hvi]) for hvi in hv_r]
        for hvi in hv_r:
            ob_ref[rows, hvi * hd_:(hvi + 1) * hd_] = outs[hvi]
            s_ref[hvi] = dec[:, cols[hvi]] * ss[hvi] + upds[hvi]

    hg = hg_ref[...]
    parts = []
    for hvi in range(GDN_V_HEADS):
        vs = slice(hvi * hd_, (hvi + 1) * hd_)
        oh = ob_ref[:, vs]
        ms = jnp.mean(oh * oh, axis=-1, keepdims=True)
        parts.append((oh * lax.rsqrt(ms + RMS_EPS) * hg * _silu(z_ref[:, vs])).astype(BF16))
    y = jnp.concatenate(parts, axis=-1)
    m = jnp.dot(y, wout_ref[...], preferred_element_type=F32)
    o_ref[0] = x + gt_ref[0] * m


def _gdn_call(x, sh, sc, gt, ng, w_in, conv_w, a_log, dt_bias, head_g, w_out, *, tb):
    b, s, d = x.shape
    dqk = GDN_QK_HEADS * GDN_HEAD_DIM
    dvv = GDN_V_HEADS * GDN_HEAD_DIM
    nqkv = 2 * dqk + dvv
    wqkv = w_in[:, :nqkv].astype(BF16)
    wz = w_in[:, nqkv:nqkv + dvv].astype(BF16)
    padh = ((0, 0), (0, LANE - GDN_V_HEADS))
    wa = jnp.pad(w_in[:, nqkv + dvv:nqkv + dvv + GDN_V_HEADS], padh).astype(BF16)
    wb = jnp.pad(w_in[:, nqkv + dvv + GDN_V_HEADS:], padh).astype(BF16)
    alog = jnp.pad(a_log.reshape(1, -1), padh)
    dtb = jnp.pad(dt_bias.reshape(1, -1), padh)
    const2 = lambda bi, i: (0, 0)
    vec = pl.BlockSpec((1, 1, d), lambda bi, i: (bi, 0, 0))
    kern = functools.partial(_gdn_kernel, tb=tb)
    return pl.pallas_call(
        kern,
        grid=(b, s // tb),
        in_specs=[
            pl.BlockSpec((1, tb, d), lambda bi, i: (bi, i, 0)),
            vec, vec, vec,
            pl.BlockSpec((1, d), const2),
            pl.BlockSpec(wqkv.shape, const2),
            pl.BlockSpec(wz.shape, const2),
            pl.BlockSpec(wa.shape, const2),
            pl.BlockSpec(wb.shape, const2),
            pl.BlockSpec(conv_w.shape, const2),
            pl.BlockSpec((1, LANE), const2),
            pl.BlockSpec((1, LANE), const2),
            pl.BlockSpec((1, GDN_HEAD_DIM), const2),
            pl.BlockSpec((dvv, d), const2),
        ],
        out_specs=pl.BlockSpec((1, tb, d), lambda bi, i: (bi, i, 0)),
        out_shape=jax.ShapeDtypeStruct(x.shape, F32),
        scratch_shapes=[
            pltpu.VMEM((GDN_V_HEADS, GDN_HEAD_DIM, GDN_HEAD_DIM), F32),
            pltpu.VMEM((tb + SUBLANE, nqkv), F32),
            pltpu.VMEM((tb, dqk), F32),
            pltpu.VMEM((tb, dqk), F32),
            pltpu.VMEM((tb, dvv), F32),
            pltpu.VMEM((tb, dvv), F32),
            pltpu.VMEM((tb, LANE), F32),
            pltpu.VMEM((tb, LANE), F32),
            pltpu.VMEM((tb, dvv), F32),
        ],
        compiler_params=pltpu.CompilerParams(
            dimension_semantics=("arbitrary", "arbitrary"), vmem_limit_bytes=VMEM_LIMIT),
        name="gdn_layer",
    )(x, sh, sc, gt, ng.reshape(1, d), wqkv, wz, wa, wb, conv_w, alog, dtb,
      head_g.reshape(1, GDN_HEAD_DIM), w_out.astype(BF16))


def _top_values(x, n, with_rank=False):
    vals = []
    rank = jnp.full(x.shape, float(n), F32) if with_rank else None
    for r in range(n):
        m = jnp.max(x, axis=0, keepdims=True)
        vals.append(m)
        if with_rank:
            rank = jnp.where(x == m, float(r), rank)
        if r + 1 < n:
            x = jnp.where(x == m, -jnp.inf, x)
    return (vals, rank) if with_rank else vals


def _pair_pack(x):
    bits = lax.bitcast_convert_type(x.astype(BF16).astype(F32), jnp.int32)
    return bits | lax.shift_right_logical(bits, jnp.int32(16))


def _stack_rows(rows):
    l = rows[0].shape[1]
    sub = lax.broadcasted_iota(jnp.int32, (SUBLANE, l), 0)
    out = jnp.broadcast_to(rows[0], (SUBLANE, l))
    for i in range(1, len(rows)):
        out = jnp.where(sub == i, rows[i], out)
    return out


def _peer_kernel(x_ref, sh_ref, sc_ref, gt_ref, ng_ref, wq_ref, k1_ref, k2_ref, u_ref, vt_ref, o_ref,
                 tt_ref, s1_ref, s2_ref, r1_ref, w1_ref, jb_ref, e2_ref, yt_ref, *half_refs, tt, eb, ne):
    g = pl.program_id(2)
    nh = tt // PEER_NCOLS
    ht_refs = half_refs[:nh]
    at_refs = half_refs[nh:]
    nj = tt // LANE
    ab = eb // PEER_NKEYS
    k = PEER_TOPK

    @pl.when(g == 0)
    def _():
        x = x_ref[0]
        t = _norm_mod(x, ng_ref[...], sc_ref[0], sh_ref[0])
        tb = t.astype(BF16)
        tt_ref[...] = t.T.astype(BF16)
        q = jnp.dot(tb, wq_ref[...], preferred_element_type=F32)
        for h in range(PEER_HEADS):
            q1 = q[:, (2 * h) * PEER_HALF:(2 * h + 1) * PEER_HALF]
            q2 = q[:, (2 * h + 1) * PEER_HALF:(2 * h + 2) * PEER_HALF]
            s1_ref[h] = lax.dot_general(k1_ref[...], q1, NT, precision=HI, preferred_element_type=F32)
            s2_ref[h] = lax.dot_general(k2_ref[...], q2, NT, precision=HI, preferred_element_type=F32)
        yt_ref[...] = jnp.zeros_like(yt_ref)
        for ht_ref in ht_refs:
            ht_ref[...] = jnp.zeros_like(ht_ref)

        def route(idx, carry):
            h = idx // nj
            c0 = pl.multiple_of((idx % nj) * LANE, LANE)
            cols = pl.ds(c0, LANE)
            x1 = s1_ref[h, :, cols]
            x2 = s2_ref[h, :, cols]
            v1, rank1 = _top_values(x1, k + 1, with_rank=True)
            v2 = _top_values(x2, k + 1)
            v2lo = _stack_rows(v2[:SUBLANE])
            v2hi = _stack_rows(v2[SUBLANE:2 * SUBLANE])
            v1hi = _stack_rows(v1[SUBLANE:2 * SUBLANE])
            sub = lax.broadcasted_iota(jnp.int32, (SUBLANE, LANE), 0)
            last = jnp.where(sub == 0, v1[0] + v2[k], jnp.where(sub == 1, v1[k] + v2[0], -jnp.inf))
            tiles = [v1[0] + v2lo, v1[0] + v2hi, v1hi + v2[0], last]
            tiles += [v1[i] + v2lo for i in range(1, SUBLANE)]
            cand = jnp.concatenate(tiles, axis=0)
            tops = _top_values(cand, k + 1)
            kth = tops[k - 1]
            cut = 0.5 * (kth + tops[k])
            top = v1[0] + v2[0]
            z = jnp.sum(jnp.where(cand >= kth, jnp.exp(cand - top), 0.0), axis=0, keepdims=True)
            c2 = cut - x2
            passing = jnp.zeros_like(c2)
            for i in range(k + 1):
                passing = passing + jnp.where(v1[i] >= c2, 1.0, 0.0)
            r1_ref[h, :, cols] = _pair_pack(rank1)
            w1_ref[h, :, cols] = _pair_pack(jnp.exp(x1 - v1[0]) / z)
            jb_ref[h, :, cols] = passing.astype(BF16)
            e2_ref[h, :, cols] = jnp.exp(x2 - v2[0]).astype(BF16)
            return carry

        lax.fori_loop(0, PEER_HEADS * nj, route, 0)

    em = jnp.clip(g - 1, 0, ne - 1)
    arows = pl.ds(pl.multiple_of(em * ab, SUBLANE), SUBLANE)
    bh = PEER_NKEYS // 2
    d = yt_ref.shape[0]
    ngroup = PEER_NCOLS // LANE

    def gate_block(n, jj, a2, hb):
        cols = slice(n * PEER_NCOLS + jj * LANE, n * PEER_NCOLS + (jj + 1) * LANE)
        lcols = slice(jj * LANE, (jj + 1) * LANE)
        brows = slice(hb * bh, (hb + 1) * bh)
        als = (2 * a2, 2 * a2 + 1)
        r1t = [r1_ref[h, arows, cols] for h in range(PEER_HEADS)]
        w1t = [w1_ref[h, arows, cols] for h in range(PEER_HEADS)]

        def row(tile, al):
            return pltpu.bitcast(jnp.broadcast_to(tile[al:al + 1, :], (bh // 2, LANE)), BF16)

        accs = [jnp.zeros((bh, LANE), BF16) for _ in als]
        for h in range(PEER_HEADS):
            jb = jb_ref[h, brows, cols]
            e2 = e2_ref[h, brows, cols]
            for i, al in enumerate(als):
                sel = row(r1t[h], al) < jb
                accs[i] = accs[i] + jnp.where(sel, e2 * row(w1t[h], al), jnp.zeros_like(e2))
        fold = None
        for i, al in enumerate(als):
            rows = slice(al * PEER_NKEYS + hb * bh, al * PEER_NKEYS + (hb + 1) * bh)
            out = accs[i] * _gelu_tanh(ht_refs[n][rows, lcols]).astype(BF16)
            at_refs[n][rows, lcols] = out
            bits = pltpu.bitcast(out, jnp.int32)
            fold = bits if fold is None else fold | bits
        while fold.shape[0] > SUBLANE:
            half = fold.shape[0] // 2
            fold = fold[:half] | fold[half:]
        return fold & jnp.minimum(g, 0)

    def after(zero, rhs):
        head = pltpu.bitcast(rhs[:PEER_NCOLS], jnp.int32)
        head = head | jnp.tile(zero, (head.shape[0] // SUBLANE, head.shape[1] // LANE))
        return jnp.concatenate([pltpu.bitcast(head, BF16), rhs[PEER_NCOLS:]], axis=0)

    def matmul_pieces(n):
        ncols = slice(n * PEER_NCOLS, (n + 1) * PEER_NCOLS)
        pieces = []
        for m in range(d // PEER_MROWS):
            mrows = slice(m * PEER_MROWS, (m + 1) * PEER_MROWS)

            def second(zero, mrows=mrows):
                yt_ref[mrows, ncols] += jnp.dot(vt_ref[0, mrows, :], after(zero, at_refs[n][...]),
                                                preferred_element_type=F32)

            def first(zero, mrows=mrows):
                ht_refs[n][mrows, :] = jnp.dot(u_ref[mrows, :], after(zero, tt_ref[:, ncols]),
                                               preferred_element_type=F32)
            pieces += [second, first]
        return pieces

    pending = []
    for n in range(tt // PEER_NCOLS):
        blk = 0
        for jj in range(ngroup):
            for a2 in range(ab // 2):
                for hb in range(2):
                    zero = gate_block(n, jj, a2, hb)
                    if pending and blk % PEER_GATE_PER_PIECE == 1:
                        pending.pop(0)(zero)
                    blk += 1
        for piece in pending:
            piece(zero)
        pending = matmul_pieces(n)
    for piece in pending:
        piece(zero)

    @pl.when(g == ne)
    def _():
        o_ref[0] = x_ref[0] + gt_ref[0] * yt_ref[...].T


def _peer_call(x, sh, sc, gt, ng, wq, k1, k2, u, v, *, tt, eb):
    b, s, d = x.shape
    assert eb == SUBLANE * PEER_NKEYS and u.shape[0] % eb == 0
    ne = u.shape[0] // eb
    ub = u.astype(BF16)
    vtb = v.astype(BF16).reshape(ne, eb, d).transpose(0, 2, 1)
    const3 = lambda bi, i, e: (0, 0)
    vec = pl.BlockSpec((1, 1, d), lambda bi, i, e: (bi, 0, 0))
    kern = functools.partial(_peer_kernel, tt=tt, eb=eb, ne=ne)
    sshape = (PEER_HEADS, PEER_NKEYS, tt)
    return pl.pallas_call(
        kern,
        grid=(b, s // tt, ne + 1),
        in_specs=[
            pl.BlockSpec((1, tt, d), lambda bi, i, e: (bi, i, 0)),
            vec, vec, vec,
            pl.BlockSpec((1, d), const3),
            pl.BlockSpec(wq.shape, const3),
            pl.BlockSpec(k1.shape, const3),
            pl.BlockSpec(k2.shape, const3),
            pl.BlockSpec((eb, d), lambda bi, i, g: (jnp.minimum(g, ne - 1), 0)),
            pl.BlockSpec((1, d, eb), lambda bi, i, g: (jnp.maximum(g - 1, 0), 0, 0)),
        ],
        out_specs=pl.BlockSpec((1, tt, d), lambda bi, i, e: (bi, i, 0)),
        out_shape=jax.ShapeDtypeStruct(x.shape, F32),
        scratch_shapes=[
            pltpu.VMEM((d, tt), BF16),
            pltpu.VMEM(sshape, F32),
            pltpu.VMEM(sshape, F32),
            pltpu.VMEM(sshape, jnp.int32),
            pltpu.VMEM(sshape, jnp.int32),
            pltpu.VMEM(sshape, BF16),
            pltpu.VMEM(sshape, BF16),
            pltpu.VMEM((d, tt), F32),
        ] + [pltpu.VMEM((eb, PEER_NCOLS), F32)] * (tt // PEER_NCOLS)
          + [pltpu.VMEM((eb, PEER_NCOLS), BF16)] * (tt // PEER_NCOLS),
        compiler_params=pltpu.CompilerParams(
            dimension_semantics=("arbitrary", "arbitrary", "arbitrary"), vmem_limit_bytes=VMEM_LIMIT),
        name="peer_layer",
    )(x, sh, sc, gt, ng.reshape(1, d), wq.astype(BF16), k1, k2, ub, vtb)


def _final_kernel(x_ref, g_ref, o_ref):
    x = x_ref[0]
    ms = jnp.mean(x * x, axis=-1, keepdims=True)
    o_ref[0] = x * lax.rsqrt(ms + RMS_EPS) * g_ref[...]


def _final_call(x, g, *, tb):
    b, s, d = x.shape
    return pl.pallas_call(
        _final_kernel,
        grid=(b, s // tb),
        in_specs=[pl.BlockSpec((1, tb, d), lambda bi, i: (bi, i, 0)),
                  pl.BlockSpec((1, d), lambda bi, i: (0, 0))],
        out_specs=pl.BlockSpec((1, tb, d), lambda bi, i: (bi, i, 0)),
        out_shape=jax.ShapeDtypeStruct(x.shape, F32),
        name="final_norm",
    )(x, g.reshape(1, d))


def _pick(n, pref):
    return pref if n % pref == 0 else n


def kernel(x, c, ada_w, ada_b, norm_mix_g, norm_ffn_g, gla_w_in, gla_w_g2, gla_b_g2, gla_norm_g,
           gla_w_out, gdn_w_in, gdn_conv_w, gdn_a_log, gdn_dt_bias, gdn_norm_g, gdn_w_out,
           peer_wq, peer_k1, peer_k2, peer_u, peer_v, final_norm_g):
    b, s, d = x.shape
    depth = ada_w.shape[0]
    c_pad = jnp.pad(c, ((0, SUBLANE - b % SUBLANE if b % SUBLANE else 0), (0, 0)))
    mod = _mod_call(c_pad, ada_w, ada_b)[:, :b, :].reshape(depth, b, 6, 1, d)
    tb_mix = _pick(s, 256)
    tt = _pick(s, 512)
    eb = 1024
    for i in range(depth):
        sh1, sc1, gt1, sh2, sc2, gt2 = (mod[i, :, j] for j in range(6))
        j = i // 2
        if i % 2 == 0:
            x = _gla_call(x, sh1, sc1, gt1, norm_mix_g[i], gla_w_in[j], gla_w_g2[j], gla_b_g2[j],
                          gla_norm_g[j], gla_w_out[j], tb=tb_mix)
        else:
            x = _gdn_call(x, sh1, sc1, gt1, norm_mix_g[i], gdn_w_in[j], gdn_conv_w[j], gdn_a_log[j],
                          gdn_dt_bias[j], gdn_norm_g[j], gdn_w_out[j], tb=tb_mix)
        x = _peer_call(x, sh2, sc2, gt2, norm_ffn_g[i], peer_wq[i], peer_k1[i], peer_k2[i],
                       peer_u[i], peer_v[i], tt=tt, eb=eb)
    return _final_call(x, final_norm_g, tb=_pick(s, 512))
```

```python
import functools

import jax
import jax.numpy as jnp
from jax import lax
from jax.experimental import pallas as pl
from jax.experimental.pallas import tpu as pltpu

F32 = jnp.float32
BF16 = jnp.bfloat16
HI = lax.Precision.HIGHEST

RMS_EPS = 1e-6
CHUNK = 64

GLA_HEADS = 4
GLA_GATE_RANK = 16
GLA_TAU = 16.0

GDN_QK_HEADS = 8
GDN_V_HEADS = 16
GDN_HEAD_DIM = 128
GDN_CONV = 4

PEER_HEADS = 8
PEER_TOPK = 16
PEER_NKEYS = 128
PEER_HALF = 128
PEER_NCOLS = 256
PEER_MROWS = 1024
PEER_AROWS = 4
PEER_GATE_PER_PIECE = 4

LANE = 128
SUBLANE = 8
VMEM_LIMIT = 56 * 1024 * 1024

NT = (((1,), (1,)), ((), ()))


def _sigmoid(x):
    return 1.0 / (1.0 + jnp.exp(-x))


def _silu(x):
    return x * _sigmoid(x)


def _softplus(x):
    return jnp.maximum(x, 0.0) + jnp.log1p(jnp.exp(-jnp.abs(x)))


def _gelu_tanh(x):
    c = 0.7978845608028654
    hx = 0.5 * x
    return hx + hx * jnp.tanh(x * (c + (c * 0.044715) * (x * x)))


def _norm_mod(x, g, sc, sh):
    ms = jnp.mean(x * x, axis=-1, keepdims=True)
    return (x * lax.rsqrt(ms + RMS_EPS) * g) * (1.0 + sc) + sh


def _bdot(a, b):
    return jnp.dot(a.astype(BF16), b.astype(BF16), preferred_element_type=F32)


def _bdot_nt(a, b):
    return lax.dot_general(a.astype(BF16), b.astype(BF16), NT, preferred_element_type=F32)


def _hdot(a, b):
    return jnp.dot(a, b, precision=HI, preferred_element_type=F32)


def _tril(n, k=0):
    r = lax.broadcasted_iota(jnp.int32, (n, n), 0)
    c = lax.broadcasted_iota(jnp.int32, (n, n), 1)
    return (c - r) <= k


def _mod_kernel(c_ref, w_ref, b_ref, o_ref):
    c = c_ref[...]
    o_ref[0] = _hdot(_silu(c), w_ref[0]) + b_ref[0]


def _mod_call(c_pad, ada_w, ada_b):
    depth, d, d6 = ada_w.shape
    nb = d6 // d
    return pl.pallas_call(
        _mod_kernel,
        grid=(depth, nb),
        in_specs=[
            pl.BlockSpec(c_pad.shape, lambda i, j: (0, 0)),
            pl.BlockSpec((1, d, d), lambda i, j: (i, 0, j)),
            pl.BlockSpec((1, 1, d), lambda i, j: (i, 0, j)),
        ],
        out_specs=pl.BlockSpec((1, c_pad.shape[0], d), lambda i, j: (i, 0, j)),
        out_shape=jax.ShapeDtypeStruct((depth, c_pad.shape[0], d6), F32),
        name="adaln_mod",
    )(c_pad, ada_w, ada_b.reshape(depth, 1, d6))


def _gla_kernel(x_ref, sh_ref, sc_ref, gt_ref, ng_ref, wqkvr_ref, wg1_ref, wg2_ref, bg2_ref,
                hg_ref, wout_ref, o_ref, st_ref, p_ref, la_ref, ob_ref, *, tb, dk, dv):
    hk = dk // GLA_HEADS
    hv = dv // GLA_HEADS
    nc = tb // CHUNK

    @pl.when(pl.program_id(1) == 0)
    def _():
        st_ref[...] = jnp.zeros_like(st_ref)

    x = x_ref[0]
    h = _norm_mod(x, ng_ref[...], sc_ref[0], sh_ref[0])
    hb = h.astype(BF16)
    p_ref[...] = jnp.dot(hb, wqkvr_ref[...], preferred_element_type=F32)
    g1 = jnp.dot(hb, wg1_ref[...], preferred_element_type=F32)
    g = jnp.dot(g1.astype(BF16), wg2_ref[...], preferred_element_type=F32) + bg2_ref[...]
    la_ref[...] = (jnp.minimum(g, 0.0) - jnp.log1p(jnp.exp(-jnp.abs(g)))) * (1.0 / GLA_TAU)

    tril = _tril(CHUNK).astype(F32)
    causal = _tril(CHUNK)
    scale = hk ** -0.5

    for c in range(nc):
        rows = slice(c * CHUNK, (c + 1) * CHUNK)
        cum = _hdot(tril, la_ref[rows, :])
        cl = cum[CHUNK - 1:CHUNK, :]
        e_pos = jnp.exp(cum)
        e_neg = jnp.exp(-cum)
        e_end = jnp.exp(cl - cum)
        dec = jnp.exp(cl)
        hr = range(GLA_HEADS)
        ksl = [slice(hd * hk, (hd + 1) * hk) for hd in hr]
        qis = [p_ref[rows, hd * hk:(hd + 1) * hk] * (scale * e_pos[:, ksl[hd]]) for hd in hr]
        kraw = [p_ref[rows, dk + hd * hk:dk + (hd + 1) * hk] for hd in hr]
        vs = [p_ref[rows, 2 * dk + hd * hv:2 * dk + (hd + 1) * hv] for hd in hr]
        attns = [jnp.where(causal, _bdot_nt(qis[hd], kraw[hd] * e_neg[:, ksl[hd]]), 0.0) for hd in hr]
        sts = [st_ref[hd] for hd in hr]
        inter = [_bdot_nt(qis[hd], sts[hd]) for hd in hr]
        upds = [_bdot(vs[hd].T, kraw[hd] * e_end[:, ksl[hd]]) for hd in hr]
        intra = [_bdot(attns[hd], vs[hd]) for hd in hr]
        for hd in hr:
            ob_ref[rows, hd * hv:(hd + 1) * hv] = intra[hd] + inter[hd]
            st_ref[hd] = dec[:, ksl[hd]] * sts[hd] + upds[hd]

    hg = hg_ref[...]
    parts = []
    for hd in range(GLA_HEADS):
        oh = ob_ref[:, hd * hv:(hd + 1) * hv]
        ms = jnp.mean(oh * oh, axis=-1, keepdims=True)
        r = p_ref[:, 2 * dk + dv + hd * hv:2 * dk + dv + (hd + 1) * hv]
        parts.append((oh * lax.rsqrt(ms + RMS_EPS) * hg * _silu(r)).astype(BF16))
    y = jnp.concatenate(parts, axis=-1)
    m = jnp.dot(y, wout_ref[...], preferred_element_type=F32)
    o_ref[0] = x + gt_ref[0] * m


def _gla_call(x, sh, sc, gt, ng, w_in, w_g2, b_g2, head_g, w_out, *, tb):
    b, s, d = x.shape
    dk = w_g2.shape[1]
    dv = w_out.shape[0]
    wqkvr = w_in[:, :2 * dk + 2 * dv].astype(BF16)
    wg1 = jnp.pad(w_in[:, 2 * dk + 2 * dv:], ((0, 0), (0, LANE - GLA_GATE_RANK))).astype(BF16)
    wg2 = jnp.pad(w_g2, ((0, LANE - GLA_GATE_RANK), (0, 0))).astype(BF16)
    hv = dv // GLA_HEADS
    hk = dk // GLA_HEADS
    const2 = lambda bi, i: (0, 0)
    vec = pl.BlockSpec((1, 1, d), lambda bi, i: (bi, 0, 0))
    kern = functools.partial(_gla_kernel, tb=tb, dk=dk, dv=dv)
    return pl.pallas_call(
        kern,
        grid=(b, s // tb),
        in_specs=[
            pl.BlockSpec((1, tb, d), lambda bi, i: (bi, i, 0)),
            vec, vec, vec,
            pl.BlockSpec((1, d), const2),
            pl.BlockSpec(wqkvr.shape, const2),
            pl.BlockSpec(wg1.shape, const2),
            pl.BlockSpec(wg2.shape, const2),
            pl.BlockSpec((1, dk), const2),
            pl.BlockSpec((1, hv), const2),
            pl.BlockSpec((dv, d), const2),
        ],
        out_specs=pl.BlockSpec((1, tb, d), lambda bi, i: (bi, i, 0)),
        out_shape=jax.ShapeDtypeStruct(x.shape, F32),
        scratch_shapes=[
            pltpu.VMEM((GLA_HEADS, hv, hk), F32),
            pltpu.VMEM((tb, 2 * dk + 2 * dv), F32),
            pltpu.VMEM((tb, dk), F32),
            pltpu.VMEM((tb, dv), F32),
        ],
        compiler_params=pltpu.CompilerParams(
            dimension_semantics=("arbitrary", "arbitrary"), vmem_limit_bytes=VMEM_LIMIT),
        name="gla_layer",
    )(x, sh, sc, gt, ng.reshape(1, d), wqkvr, wg1, wg2, b_g2.reshape(1, dk),
      head_g.reshape(1, hv), w_out.astype(BF16))


def _block_inverse_masks(n):
    r = lax.broadcasted_iota(jnp.int32, (n, n), 0)
    c = lax.broadcasted_iota(jnp.int32, (n, n), 1)
    eye = (r == c).astype(F32)
    first = (r // 2) == (c // 2)
    quads = []
    s = 2
    while s < n:
        quads.append(((r // (2 * s)) == (c // (2 * s))) & ((r // s) != (c // s)))
        s *= 2
    return eye, first, quads


def _unit_lower_inverse(a_list, masks):
    eye, first, quads = masks
    ds = [eye - jnp.where(first, a, 0.0) for a in a_list]
    for quad in quads:
        ms = [_bdot(d, jnp.where(quad, a, 0.0)) for d, a in zip(ds, a_list)]
        ds = [d - _bdot(m, d) for d, m in zip(ds, ms)]
    return ds


def _gdn_kernel(x_ref, sh_ref, sc_ref, gt_ref, ng_ref, wqkv_ref, wz_ref, wa_ref, wb_ref, cw_ref,
                alog_ref, dtb_ref, hg_ref, wout_ref, o_ref,
                s_ref, cb_ref, q_ref, k_ref, v_ref, z_ref, g_ref, bt_ref, ob_ref, *, tb):
    dqk = GDN_QK_HEADS * GDN_HEAD_DIM
    dvv = GDN_V_HEADS * GDN_HEAD_DIM
    hd_ = GDN_HEAD_DIM
    nc = tb // CHUNK
    rep = GDN_V_HEADS // GDN_QK_HEADS

    @pl.when(pl.program_id(1) == 0)
    def _():
        s_ref[...] = jnp.zeros_like(s_ref)
        cb_ref[0:SUBLANE, :] = jnp.zeros((SUBLANE, cb_ref.shape[1]), F32)

    x = x_ref[0]
    h = _norm_mod(x, ng_ref[...], sc_ref[0], sh_ref[0])
    hb = h.astype(BF16)
    cb_ref[SUBLANE:SUBLANE + tb, :] = jnp.dot(hb, wqkv_ref[...], preferred_element_type=F32)
    z_ref[...] = jnp.dot(hb, wz_ref[...], preferred_element_type=F32)
    a = jnp.dot(hb, wa_ref[...], preferred_element_type=F32)
    bb = jnp.dot(hb, wb_ref[...], preferred_element_type=F32)
    g_ref[...] = -jnp.exp(alog_ref[...]) * _softplus(a + dtb_ref[...])
    bt_ref[...] = _sigmoid(bb)

    off = SUBLANE - (GDN_CONV - 1)
    conv = cw_ref[0:1, :] * cb_ref[off:off + tb, :]
    for j in range(1, GDN_CONV):
        conv = conv + cw_ref[j:j + 1, :] * cb_ref[off + j:off + j + tb, :]
    cb_ref[0:SUBLANE, :] = cb_ref[tb:tb + SUBLANE, :]
    qkv = _silu(conv)
    qscale = hd_ ** -0.5
    for hq in range(GDN_QK_HEADS):
        sl = slice(hq * hd_, (hq + 1) * hd_)
        qh = qkv[:, hq * hd_:(hq + 1) * hd_]
        kh = qkv[:, dqk + hq * hd_:dqk + (hq + 1) * hd_]
        q_ref[:, sl] = qh * (lax.rsqrt(jnp.sum(qh * qh, axis=-1, keepdims=True) + 1e-6) * qscale)
        k_ref[:, sl] = kh * lax.rsqrt(jnp.sum(kh * kh, axis=-1, keepdims=True) + 1e-6)
    v_ref[...] = qkv[:, 2 * dqk:]

    tril = _tril(CHUNK).astype(F32)
    lower = _tril(CHUNK)
    strict = _tril(CHUNK, -1)
    inv_masks = _block_inverse_masks(CHUNK)

    for c in range(nc):
        rows = slice(c * CHUNK, (c + 1) * CHUNK)
        gc = _hdot(tril, g_ref[rows, :])
        gct = gc.T
        gl = gc[CHUNK - 1:CHUNK, :]
        eg = jnp.exp(gc)
        eendt = jnp.exp(gl - gc).T
        dec = jnp.exp(gl)
        bt = bt_ref[rows, :]
        beg = bt * eg
        hq_r = range(GDN_QK_HEADS)
        hv_r = range(GDN_V_HEADS)
        qs = [q_ref[rows, hq * hd_:(hq + 1) * hd_] for hq in hq_r]
        ks = [k_ref[rows, hq * hd_:(hq + 1) * hd_] for hq in hq_r]
        kks = [_bdot_nt(ks[hq], ks[hq]) for hq in hq_r]
        qk0s = [_bdot_nt(qs[hq], ks[hq]) for hq in hq_r]
        khts = [ks[hq].T for hq in hq_r]
        cols = [slice(hvi, hvi + 1) for hvi in hv_r]
        decays = []
        for hvi in hv_r:
            diff = gc[:, cols[hvi]] - gct[hvi:hvi + 1, :]
            decays.append(jnp.where(lower, jnp.exp(jnp.where(lower, diff, 0.0)), 0.0))
        amats = [jnp.where(strict, bt[:, cols[hvi]] * kks[hvi // rep] * decays[hvi], 0.0) for hvi in hv_r]
        tinvs = _unit_lower_inverse(amats, inv_masks)
        rhs = [jnp.concatenate([bt[:, cols[hvi]] * v_ref[rows, hvi * hd_:(hvi + 1) * hd_],
                                beg[:, cols[hvi]] * ks[hvi // rep]], axis=1) for hvi in hv_r]
        uws = [_bdot(tinvs[hvi], rhs[hvi]) for hvi in hv_r]
        ss = [s_ref[hvi] for hvi in hv_r]
        wss = [_bdot(uws[hvi][:, hd_:], ss[hvi]) for hvi in hv_r]
        qss = [_bdot(qs[hvi // rep] * eg[:, cols[hvi]], ss[hvi]) for hvi in hv_r]
        vns = [uws[hvi][:, :hd_] - wss[hvi] for hvi in hv_r]
        outs = [qss[hvi] + _bdot(qk0s[hvi // rep] * decays[hvi], vns[hvi]) for hvi in hv_r]
        upds = [_bdot(khts[hvi // rep] * eendt[hvi:hvi + 1, :], vns[hvi]) for hvi in hv_r]
        for hvi in hv_r:
            ob_ref[rows, hvi * hd_:(hvi + 1) * hd_] = outs[hvi]
            s_ref[hvi] = dec[:, cols[hvi]] * ss[hvi] + upds[hvi]

    hg = hg_ref[...]
    parts = []
    for hvi in range(GDN_V_HEADS):
        vs = slice(hvi * hd_, (hvi + 1) * hd_)
        oh = ob_ref[:, vs]
        ms = jnp.mean(oh * oh, axis=-1, keepdims=True)
        parts.append((oh * lax.rsqrt(ms + RMS_EPS) * hg * _silu(z_ref[:, vs])).astype(BF16))
    y = jnp.concatenate(parts, axis=-1)
    m = jnp.dot(y, wout_ref[...], preferred_element_type=F32)
    o_ref[0] = x + gt_ref[0] * m


def _gdn_call(x, sh, sc, gt, ng, w_in, conv_w, a_log, dt_bias, head_g, w_out, *, tb):
    b, s, d = x.shape
    dqk = GDN_QK_HEADS * GDN_HEAD_DIM
    dvv = GDN_V_HEADS * GDN_HEAD_DIM
    nqkv = 2 * dqk + dvv
    wqkv = w_in[:, :nqkv].astype(BF16)
    wz = w_in[:, nqkv:nqkv + dvv].astype(BF16)
    padh = ((0, 0), (0, LANE - GDN_V_HEADS))
    wa = jnp.pad(w_in[:, nqkv + dvv:nqkv + dvv + GDN_V_HEADS], padh).astype(BF16)
    wb = jnp.pad(w_in[:, nqkv + dvv + GDN_V_HEADS:], padh).astype(BF16)
    alog = jnp.pad(a_log.reshape(1, -1), padh)
    dtb = jnp.pad(dt_bias.reshape(1, -1), padh)
    const2 = lambda bi, i: (0, 0)
    vec = pl.BlockSpec((1, 1, d), lambda bi, i: (bi, 0, 0))
    kern = functools.partial(_gdn_kernel, tb=tb)
    return pl.pallas_call(
        kern,
        grid=(b, s // tb),
        in_specs=[
            pl.BlockSpec((1, tb, d), lambda bi, i: (bi, i, 0)),
            vec, vec, vec,
            pl.BlockSpec((1, d), const2),
            pl.BlockSpec(wqkv.shape, const2),
            pl.BlockSpec(wz.shape, const2),
            pl.BlockSpec(wa.shape, const2),
            pl.BlockSpec(wb.shape, const2),
            pl.BlockSpec(conv_w.shape, const2),
            pl.BlockSpec((1, LANE), const2),
            pl.BlockSpec((1, LANE), const2),
            pl.BlockSpec((1, GDN_HEAD_DIM), const2),
            pl.BlockSpec((dvv, d), const2),
        ],
        out_specs=pl.BlockSpec((1, tb, d), lambda bi, i: (bi, i, 0)),
        out_shape=jax.ShapeDtypeStruct(x.shape, F32),
        scratch_shapes=[
            pltpu.VMEM((GDN_V_HEADS, GDN_HEAD_DIM, GDN_HEAD_DIM), F32),
            pltpu.VMEM((tb + SUBLANE, nqkv), F32),
            pltpu.VMEM((tb, dqk), F32),
            pltpu.VMEM((tb, dqk), F32),
            pltpu.VMEM((tb, dvv), F32),
            pltpu.VMEM((tb, dvv), F32),
            pltpu.VMEM((tb, LANE), F32),
            pltpu.VMEM((tb, LANE), F32),
            pltpu.VMEM((tb, dvv), F32),
        ],
        compiler_params=pltpu.CompilerParams(
            dimension_semantics=("arbitrary", "arbitrary"), vmem_limit_bytes=VMEM_LIMIT),
        name="gdn_layer",
    )(x, sh, sc, gt, ng.reshape(1, d), wqkv, wz, wa, wb, conv_w, alog, dtb,
      head_g.reshape(1, GDN_HEAD_DIM), w_out.astype(BF16))


def _top_values(x, n, with_rank=False):
    vals = []
    rank = jnp.full(x.shape, float(n), F32) if with_rank else None
    for r in range(n):
        m = jnp.max(x, axis=0, keepdims=True)
        vals.append(m)
        if with_rank:
            rank = jnp.where(x == m, float(r), rank)
        if r + 1 < n:
            x = jnp.where(x == m, -jnp.inf, x)
    return (vals, rank) if with_rank else vals


def _pair_pack(x):
    bits = lax.bitcast_convert_type(x.astype(BF16).astype(F32), jnp.int32)
    return bits | lax.shift_right_logical(bits, jnp.int32(16))


def _stack_rows(rows):
    l = rows[0].shape[1]
    sub = lax.broadcasted_iota(jnp.int32, (SUBLANE, l), 0)
    out = jnp.broadcast_to(rows[0], (SUBLANE, l))
    for i in range(1, len(rows)):
        out = jnp.where(sub == i, rows[i], out)
    return out


def _peer_kernel(x_ref, sh_ref, sc_ref, gt_ref, ng_ref, wq_ref, k1_ref, k2_ref, u_ref, vt_ref, o_ref,
                 tt_ref, s1_ref, s2_ref, r1_ref, w1_ref, jb_ref, e2_ref, yt_ref, *half_refs, tt, eb, ne):
    g = pl.program_id(2)
    nh = tt // PEER_NCOLS
    ht_refs = half_refs[:nh]
    at_refs = half_refs[nh:]
    nj = tt // LANE
    ab = eb // PEER_NKEYS
    k = PEER_TOPK

    @pl.when(g == 0)
    def _():
        x = x_ref[0]
        t = _norm_mod(x, ng_ref[...], sc_ref[0], sh_ref[0])
        tb = t.astype(BF16)
        tt_ref[...] = t.T.astype(BF16)
        q = jnp.dot(tb, wq_ref[...], preferred_element_type=F32)
        for h in range(PEER_HEADS):
            q1 = q[:, (2 * h) * PEER_HALF:(2 * h + 1) * PEER_HALF]
            q2 = q[:, (2 * h + 1) * PEER_HALF:(2 * h + 2) * PEER_HALF]
            s1_ref[h] = lax.dot_general(k1_ref[...], q1, NT, precision=HI, preferred_element_type=F32)
            s2_ref[h] = lax.dot_general(k2_ref[...], q2, NT, precision=HI, preferred_element_type=F32)
        yt_ref[...] = jnp.zeros_like(yt_ref)
        for ht_ref in ht_refs:
            ht_ref[...] = jnp.zeros_like(ht_ref)

        def route_one(idx):
            h = idx // nj
            c0 = pl.multiple_of((idx % nj) * LANE, LANE)
            cols = pl.ds(c0, LANE)
            x1 = s1_ref[h, :, cols]
            x2 = s2_ref[h, :, cols]
            v1, rank1 = _top_values(x1, k + 1, with_rank=True)
            v2 = _top_values(x2, k + 1)
            v2lo = _stack_rows(v2[:SUBLANE])
            v2hi = _stack_rows(v2[SUBLANE:2 * SUBLANE])
            v1hi = _stack_rows(v1[SUBLANE:2 * SUBLANE])
            sub = lax.broadcasted_iota(jnp.int32, (SUBLANE, LANE), 0)
            last = jnp.where(sub == 0, v1[0] + v2[k], jnp.where(sub == 1, v1[k] + v2[0], -jnp.inf))
            tiles = [v1[0] + v2lo, v1[0] + v2hi, v1hi + v2[0], last]
            tiles += [v1[i] + v2lo for i in range(1, SUBLANE)]
            cand = jnp.concatenate(tiles, axis=0)
            tops = _top_values(cand, k + 1)
            kth = tops[k - 1]
            cut = 0.5 * (kth + tops[k])
            top = v1[0] + v2[0]
            z = jnp.sum(jnp.where(cand >= kth, jnp.exp(cand - top), 0.0), axis=0, keepdims=True)
            c2 = cut - x2
            passing = jnp.zeros_like(c2)
            for i in range(k + 1):
                passing = passing + jnp.where(v1[i] >= c2, 1.0, 0.0)
            r1_ref[h, :, cols] = _pair_pack(rank1)
            w1_ref[h, :, cols] = _pair_pack(jnp.exp(x1 - v1[0]) / z)
            jb_ref[h, :, cols] = passing.astype(BF16)
            e2_ref[h, :, cols] = jnp.exp(x2 - v2[0]).astype(BF16)

        def route(i, carry):
            route_one(2 * i)
            route_one(2 * i + 1)
            return carry

        lax.fori_loop(0, PEER_HEADS * nj // 2, route, 0)

    em = jnp.clip(g - 1, 0, ne - 1)
    arows = pl.ds(pl.multiple_of(em * ab, SUBLANE), SUBLANE)
    bh = PEER_NKEYS // 2
    d = yt_ref.shape[0]
    ngroup = PEER_NCOLS // LANE

    def gate_block(n, jj, ag, hb):
        cols = slice(n * PEER_NCOLS + jj * LANE, n * PEER_NCOLS + (jj + 1) * LANE)
        lcols = slice(jj * LANE, (jj + 1) * LANE)
        brows = slice(hb * bh, (hb + 1) * bh)
        als = tuple(range(ag * PEER_AROWS, (ag + 1) * PEER_AROWS))
        r1t = [r1_ref[h, arows, cols] for h in range(PEER_HEADS)]
        w1t = [w1_ref[h, arows, cols] for h in range(PEER_HEADS)]

        def row(tile, al):
            return pltpu.bitcast(jnp.broadcast_to(tile[al:al + 1, :], (bh // 2, LANE)), BF16)

        accs = [jnp.zeros((bh, LANE), BF16) for _ in als]
        for h in range(PEER_HEADS):
            jb = jb_ref[h, brows, cols]
            e2 = e2_ref[h, brows, cols]
            for i, al in enumerate(als):
                sel = row(r1t[h], al) < jb
                accs[i] = accs[i] + jnp.where(sel, e2 * row(w1t[h], al), jnp.zeros_like(e2))
        fold = None
        for i, al in enumerate(als):
            rows = slice(al * PEER_NKEYS + hb * bh, al * PEER_NKEYS + (hb + 1) * bh)
            out = accs[i] * _gelu_tanh(ht_refs[n][rows, lcols]).astype(BF16)
            at_refs[n][rows, lcols] = out
            bits = pltpu.bitcast(out, jnp.int32)
            fold = bits if fold is None else fold | bits
        while fold.shape[0] > SUBLANE:
            half = fold.shape[0] // 2
            fold = fold[:half] | fold[half:]
        return fold & jnp.minimum(g, 0)

    def after(zero, rhs):
        head = pltpu.bitcast(rhs[:PEER_NCOLS], jnp.int32)
        head = head | jnp.tile(zero, (head.shape[0] // SUBLANE, head.shape[1] // LANE))
        return jnp.concatenate([pltpu.bitcast(head, BF16), rhs[PEER_NCOLS:]], axis=0)

    def matmul_pieces(n):
        ncols = slice(n * PEER_NCOLS, (n + 1) * PEER_NCOLS)
        pieces = []
        for m in range(d // PEER_MROWS):
            mrows = slice(m * PEER_MROWS, (m + 1) * PEER_MROWS)

            def second(zero, mrows=mrows):
                yt_ref[mrows, ncols] += jnp.dot(vt_ref[0, mrows, :], after(zero, at_refs[n][...]),
                                                preferred_element_type=F32)

            def first(zero, mrows=mrows):
                ht_refs[n][mrows, :] = jnp.dot(u_ref[mrows, :], after(zero, tt_ref[:, ncols]),
                                               preferred_element_type=F32)
            pieces += [second, first]
        return pieces

    pending = []
    for n in range(tt // PEER_NCOLS):
        blk = 0
        for jj in range(ngroup):
            for ag in range(ab // PEER_AROWS):
                for hb in range(2):
                    zero = gate_block(n, jj, ag, hb)
                    if pending and blk % PEER_GATE_PER_PIECE == 1:
                        pending.pop(0)(zero)
                    blk += 1
        for piece in pending:
            piece(zero)
        pending = matmul_pieces(n)
    for piece in pending:
        piece(zero)

    @pl.when(g == ne)
    def _():
        o_ref[0] = x_ref[0] + gt_ref[0] * yt_ref[...].T


def _peer_call(x, sh, sc, gt, ng, wq, k1, k2, u, v, *, tt, eb):
    b, s, d = x.shape
    assert eb == SUBLANE * PEER_NKEYS and u.shape[0] % eb == 0
    ne = u.shape[0] // eb
    ub = u.astype(BF16)
    vtb = v.astype(BF16).reshape(ne, eb, d).transpose(0, 2, 1)
    const3 = lambda bi, i, e: (0, 0)
    vec = pl.BlockSpec((1, 1, d), lambda bi, i, e: (bi, 0, 0))
    kern = functools.partial(_peer_kernel, tt=tt, eb=eb, ne=ne)
    sshape = (PEER_HEADS, PEER_NKEYS, tt)
    return pl.pallas_call(
        kern,
        grid=(b, s // tt, ne + 1),
        in_specs=[
            pl.BlockSpec((1, tt, d), lambda bi, i, e: (bi, i, 0)),
            vec, vec, vec,
            pl.BlockSpec((1, d), const3),
            pl.BlockSpec(wq.shape, const3),
            pl.BlockSpec(k1.shape, const3),
            pl.BlockSpec(k2.shape, const3),
            pl.BlockSpec((eb, d), lambda bi, i, g: (jnp.minimum(g, ne - 1), 0)),
            pl.BlockSpec((1, d, eb), lambda bi, i, g: (jnp.maximum(g - 1, 0), 0, 0)),
        ],
        out_specs=pl.BlockSpec((1, tt, d), lambda bi, i, e: (bi, i, 0)),
        out_shape=jax.ShapeDtypeStruct(x.shape, F32),
        scratch_shapes=[
            pltpu.VMEM((d, tt), BF16),
            pltpu.VMEM(sshape, F32),
            pltpu.VMEM(sshape, F32),
            pltpu.VMEM(sshape, jnp.int32),
            pltpu.VMEM(sshape, jnp.int32),
            pltpu.VMEM(sshape, BF16),
            pltpu.VMEM(sshape, BF16),
            pltpu.VMEM((d, tt), F32),
        ] + [pltpu.VMEM((eb, PEER_NCOLS), F32)] * (tt // PEER_NCOLS)
          + [pltpu.VMEM((eb, PEER_NCOLS), BF16)] * (tt // PEER_NCOLS),
        compiler_params=pltpu.CompilerParams(
            dimension_semantics=("arbitrary", "arbitrary", "arbitrary"), vmem_limit_bytes=VMEM_LIMIT),
        name="peer_layer",
    )(x, sh, sc, gt, ng.reshape(1, d), wq.astype(BF16), k1, k2, ub, vtb)


def _final_kernel(x_ref, g_ref, o_ref):
    x = x_ref[0]
    ms = jnp.mean(x * x, axis=-1, keepdims=True)
    o_ref[0] = x * lax.rsqrt(ms + RMS_EPS) * g_ref[...]


def _final_call(x, g, *, tb):
    b, s, d = x.shape
    return pl.pallas_call(
        _final_kernel,
        grid=(b, s // tb),
        in_specs=[pl.BlockSpec((1, tb, d), lambda bi, i: (bi, i, 0)),
                  pl.BlockSpec((1, d), lambda bi, i: (0, 0))],
        out_specs=pl.BlockSpec((1, tb, d), lambda bi, i: (bi, i, 0)),
        out_shape=jax.ShapeDtypeStruct(x.shape, F32),
        name="final_norm",
    )(x, g.reshape(1, d))


def _pick(n, pref):
    return pref if n % pref == 0 else n


def kernel(x, c, ada_w, ada_b, norm_mix_g, norm_ffn_g, gla_w_in, gla_w_g2, gla_b_g2, gla_norm_g,
           gla_w_out, gdn_w_in, gdn_conv_w, gdn_a_log, gdn_dt_bias, gdn_norm_g, gdn_w_out,
           peer_wq, peer_k1, peer_k2, peer_u, peer_v, final_norm_g):
    b, s, d = x.shape
    depth = ada_w.shape[0]
    c_pad = jnp.pad(c, ((0, SUBLANE - b % SUBLANE if b % SUBLANE else 0), (0, 0)))
    mod = _mod_call(c_pad, ada_w, ada_b)[:, :b, :].reshape(depth, b, 6, 1, d)
    tb_mix = _pick(s, 256)
    tt = _pick(s, 512)
    eb = 1024
    for i in range(depth):
        sh1, sc1, gt1, sh2, sc2, gt2 = (mod[i, :, j] for j in range(6))
        j = i // 2
        if i % 2 == 0:
            x = _gla_call(x, sh1, sc1, gt1, norm_mix_g[i], gla_w_in[j], gla_w_g2[j], gla_b_g2[j],
                          gla_norm_g[j], gla_w_out[j], tb=tb_mix)
        else:
            x = _gdn_call(x, sh1, sc1, gt1, norm_mix_g[i], gdn_w_in[j], gdn_conv_w[j], gdn_a_log[j],
                          gdn_dt_bias[j], gdn_norm_g[j], gdn_w_out[j], tb=tb_mix)
        x = _peer_call(x, sh2, sc2, gt2, norm_ffn_g[i], peer_wq[i], peer_k1[i], peer_k2[i],
                       peer_u[i], peer_v[i], tt=tt, eb=eb)
    return _final_call(x, final_norm_g, tb=_pick(s, 512))
```

```python
import functools

import jax
import jax.numpy as jnp
from jax import lax
from jax.experimental import pallas as pl
from jax.experimental.pallas import tpu as pltpu

F32 = jnp.float32
BF16 = jnp.bfloat16
HI = lax.Precision.HIGHEST

RMS_EPS = 1e-6
CHUNK = 64

GLA_HEADS = 4
GLA_GATE_RANK = 16
GLA_TAU = 16.0

GDN_QK_HEADS = 8
GDN_V_HEADS = 16
GDN_HEAD_DIM = 128
GDN_CONV = 4

PEER_HEADS = 8
PEER_TOPK = 16
PEER_NKEYS = 128
PEER_HALF = 128
PEER_NCOLS = 512
PEER_KTILE = 256
PEER_MROWS = 1024
PEER_AROWS = 4
PEER_GATE_PER_PIECE = 4

LANE = 128
SUBLANE = 8
PACK = 16
VMEM_LIMIT = 56 * 1024 * 1024

NT = (((1,), (1,)), ((), ()))


def _sigmoid(x):
    return 1.0 / (1.0 + jnp.exp(-x))


def _silu(x):
    return x * _sigmoid(x)


def _softplus(x):
    return jnp.maximum(x, 0.0) + jnp.log1p(jnp.exp(-jnp.abs(x)))


def _gelu_tanh(x):
    c = 0.7978845608028654
    hx = 0.5 * x
    return hx + hx * jnp.tanh(x * (c + (c * 0.044715) * (x * x)))


def _norm_mod(x, g, sc, sh):
    ms = jnp.mean(x * x, axis=-1, keepdims=True)
    return (x * lax.rsqrt(ms + RMS_EPS) * g) * (1.0 + sc) + sh


def _bdot(a, b):
    return jnp.dot(a.astype(BF16), b.astype(BF16), preferred_element_type=F32)


def _bdot_nt(a, b):
    return lax.dot_general(a.astype(BF16), b.astype(BF16), NT, preferred_element_type=F32)


def _hdot(a, b):
    return jnp.dot(a, b, precision=HI, preferred_element_type=F32)


def _tril(n, k=0):
    r = lax.broadcasted_iota(jnp.int32, (n, n), 0)
    c = lax.broadcasted_iota(jnp.int32, (n, n), 1)
    return (c - r) <= k


def _mod_kernel(c_ref, w_ref, b_ref, o_ref):
    c = c_ref[...]
    o_ref[0] = _hdot(_silu(c), w_ref[0]) + b_ref[0]


def _mod_call(c_pad, ada_w, ada_b):
    depth, d, d6 = ada_w.shape
    nb = d6 // d
    return pl.pallas_call(
        _mod_kernel,
        grid=(depth, nb),
        in_specs=[
            pl.BlockSpec(c_pad.shape, lambda i, j: (0, 0)),
            pl.BlockSpec((1, d, d), lambda i, j: (i, 0, j)),
            pl.BlockSpec((1, 1, d), lambda i, j: (i, 0, j)),
        ],
        out_specs=pl.BlockSpec((1, c_pad.shape[0], d), lambda i, j: (i, 0, j)),
        out_shape=jax.ShapeDtypeStruct((depth, c_pad.shape[0], d6), F32),
        name="adaln_mod",
    )(c_pad, ada_w, ada_b.reshape(depth, 1, d6))


def _gla_kernel(x_ref, sh_ref, sc_ref, gt_ref, ng_ref, wqkvr_ref, wg1_ref, wg2_ref, bg2_ref,
                hg_ref, wout_ref, o_ref, st_ref, p_ref, la_ref, ob_ref, *, tb, dk, dv):
    hk = dk // GLA_HEADS
    hv = dv // GLA_HEADS
    nc = tb // CHUNK

    @pl.when(pl.program_id(1) == 0)
    def _():
        st_ref[...] = jnp.zeros_like(st_ref)

    x = x_ref[0]
    h = _norm_mod(x, ng_ref[...], sc_ref[0], sh_ref[0])
    hb = h.astype(BF16)
    p_ref[...] = jnp.dot(hb, wqkvr_ref[...], preferred_element_type=F32)
    g1 = jnp.dot(hb, wg1_ref[...], preferred_element_type=F32)
    g = jnp.dot(g1.astype(BF16), wg2_ref[...], preferred_element_type=F32) + bg2_ref[...]
    la_ref[...] = (jnp.minimum(g, 0.0) - jnp.log1p(jnp.exp(-jnp.abs(g)))) * (1.0 / GLA_TAU)

    tril = _tril(CHUNK).astype(F32)
    causal = _tril(CHUNK)
    scale = hk ** -0.5

    for c in range(nc):
        rows = slice(c * CHUNK, (c + 1) * CHUNK)
        cum = _hdot(tril, la_ref[rows, :])
        cl = cum[CHUNK - 1:CHUNK, :]
        e_pos = jnp.exp(cum)
        e_neg = jnp.exp(-cum)
        e_end = jnp.exp(cl - cum)
        dec = jnp.exp(cl)
        hr = range(GLA_HEADS)
        ksl = [slice(hd * hk, (hd + 1) * hk) for hd in hr]
        qis = [p_ref[rows, hd * hk:(hd + 1) * hk] * (scale * e_pos[:, ksl[hd]]) for hd in hr]
        kraw = [p_ref[rows, dk + hd * hk:dk + (hd + 1) * hk] for hd in hr]
        vs = [p_ref[rows, 2 * dk + hd * hv:2 * dk + (hd + 1) * hv] for hd in hr]
        attns = [jnp.where(causal, _bdot_nt(qis[hd], kraw[hd] * e_neg[:, ksl[hd]]), 0.0) for hd in hr]
        sts = [st_ref[hd] for hd in hr]
        inter = [_bdot_nt(qis[hd], sts[hd]) for hd in hr]
        upds = [_bdot(vs[hd].T, kraw[hd] * e_end[:, ksl[hd]]) for hd in hr]
        intra = [_bdot(attns[hd], vs[hd]) for hd in hr]
        for hd in hr:
            ob_ref[rows, hd * hv:(hd + 1) * hv] = intra[hd] + inter[hd]
            st_ref[hd] = dec[:, ksl[hd]] * sts[hd] + upds[hd]

    hg = hg_ref[...]
    parts = []
    for hd in range(GLA_HEADS):
        oh = ob_ref[:, hd * hv:(hd + 1) * hv]
        ms = jnp.mean(oh * oh, axis=-1, keepdims=True)
        r = p_ref[:, 2 * dk + dv + hd * hv:2 * dk + dv + (hd + 1) * hv]
        parts.append((oh * lax.rsqrt(ms + RMS_EPS) * hg * _silu(r)).astype(BF16))
    y = jnp.concatenate(parts, axis=-1)
    m = jnp.dot(y, wout_ref[...], preferred_element_type=F32)
    o_ref[0] = x + gt_ref[0] * m


def _gla_call(x, sh, sc, gt, ng, w_in, w_g2, b_g2, head_g, w_out, *, tb):
    b, s, d = x.shape
    dk = w_g2.shape[1]
    dv = w_out.shape[0]
    wqkvr = w_in[:, :2 * dk + 2 * dv].astype(BF16)
    wg1 = jnp.pad(w_in[:, 2 * dk + 2 * dv:], ((0, 0), (0, LANE - GLA_GATE_RANK))).astype(BF16)
    wg2 = jnp.pad(w_g2, ((0, LANE - GLA_GATE_RANK), (0, 0))).astype(BF16)
    hv = dv // GLA_HEADS
    hk = dk // GLA_HEADS
    const2 = lambda bi, i: (0, 0)
    vec = pl.BlockSpec((1, 1, d), lambda bi, i: (bi, 0, 0))
    kern = functools.partial(_gla_kernel, tb=tb, dk=dk, dv=dv)
    return pl.pallas_call(
        kern,
        grid=(b, s // tb),
        in_specs=[
            pl.BlockSpec((1, tb, d), lambda bi, i: (bi, i, 0)),
            vec, vec, vec,
            pl.BlockSpec((1, d), const2),
            pl.BlockSpec(wqkvr.shape, const2),
            pl.BlockSpec(wg1.shape, const2),
            pl.BlockSpec(wg2.shape, const2),
            pl.BlockSpec((1, dk), const2),
            pl.BlockSpec((1, hv), const2),
            pl.BlockSpec((dv, d), const2),
        ],
        out_specs=pl.BlockSpec((1, tb, d), lambda bi, i: (bi, i, 0)),
        out_shape=jax.ShapeDtypeStruct(x.shape, F32),
        scratch_shapes=[
            pltpu.VMEM((GLA_HEADS, hv, hk), F32),
            pltpu.VMEM((tb, 2 * dk + 2 * dv), F32),
            pltpu.VMEM((tb, dk), F32),
            pltpu.VMEM((tb, dv), F32),
        ],
        compiler_params=pltpu.CompilerParams(
            dimension_semantics=("arbitrary", "arbitrary"), vmem_limit_bytes=VMEM_LIMIT),
        name="gla_layer",
    )(x, sh, sc, gt, ng.reshape(1, d), wqkvr, wg1, wg2, b_g2.reshape(1, dk),
      head_g.reshape(1, hv), w_out.astype(BF16))


def _block_inverse_masks(n):
    r = lax.broadcasted_iota(jnp.int32, (n, n), 0)
    c = lax.broadcasted_iota(jnp.int32, (n, n), 1)
    eye = (r == c).astype(F32)
    first = (r // 2) == (c // 2)
    quads = []
    s = 2
    while s < n:
        quads.append(((r // (2 * s)) == (c // (2 * s))) & ((r // s) != (c // s)))
        s *= 2
    return eye, first, quads


def _unit_lower_inverse(a_list, masks):
    eye, first, quads = masks
    ds = [eye - jnp.where(first, a, 0.0) for a in a_list]
    for quad in quads:
        ms = [_bdot(d, jnp.where(quad, a, 0.0)) for d, a in zip(ds, a_list)]
        ds = [d - _bdot(m, d) for d, m in zip(ds, ms)]
    return ds


def _gdn_kernel(x_ref, sh_ref, sc_ref, gt_ref, ng_ref, wqkv_ref, wz_ref, wa_ref, wb_ref, cw_ref,
                alog_ref, dtb_ref, hg_ref, wout_ref, o_ref,
                s_ref, cb_ref, q_ref, k_ref, v_ref, z_ref, g_ref, bt_ref, ob_ref, *, tb):
    dqk = GDN_QK_HEADS * GDN_HEAD_DIM
    dvv = GDN_V_HEADS * GDN_HEAD_DIM
    hd_ = GDN_HEAD_DIM
    nc = tb // CHUNK
    rep = GDN_V_HEADS // GDN_QK_HEADS

    @pl.when(pl.program_id(1) == 0)
    def _():
        s_ref[...] = jnp.zeros_like(s_ref)
        cb_ref[0:SUBLANE, :] = jnp.zeros((SUBLANE, cb_ref.shape[1]), F32)

    x = x_ref[0]
    h = _norm_mod(x, ng_ref[...], sc_ref[0], sh_ref[0])
    hb = h.astype(BF16)
    cb_ref[SUBLANE:SUBLANE + tb, :] = jnp.dot(hb, wqkv_ref[...], preferred_element_type=F32)
    z_ref[...] = jnp.dot(hb, wz_ref[...], preferred_element_type=F32)
    a = jnp.dot(hb, wa_ref[...], preferred_element_type=F32)
    bb = jnp.dot(hb, wb_ref[...], preferred_element_type=F32)
    g_ref[...] = -jnp.exp(alog_ref[...]) * _softplus(a + dtb_ref[...])
    bt_ref[...] = _sigmoid(bb)

    off = SUBLANE - (GDN_CONV - 1)
    conv = cw_ref[0:1, :] * cb_ref[off:off + tb, :]
    for j in range(1, GDN_CONV):
        conv = conv + cw_ref[j:j + 1, :] * cb_ref[off + j:off + j + tb, :]
    cb_ref[0:SUBLANE, :] = cb_ref[tb:tb + SUBLANE, :]
    qkv = _silu(conv)
    qscale = hd_ ** -0.5
    for hq in range(GDN_QK_HEADS):
        sl = slice(hq * hd_, (hq + 1) * hd_)
        qh = qkv[:, hq * hd_:(hq + 1) * hd_]
        kh = qkv[:, dqk + hq * hd_:dqk + (hq + 1) * hd_]
        q_ref[:, sl] = qh * (lax.rsqrt(jnp.sum(qh * qh, axis=-1, keepdims=True) + 1e-6) * qscale)
        k_ref[:, sl] = kh * lax.rsqrt(jnp.sum(kh * kh, axis=-1, keepdims=True) + 1e-6)
    v_ref[...] = qkv[:, 2 * dqk:]

    tril = _tril(CHUNK).astype(F32)
    lower = _tril(CHUNK)
    strict = _tril(CHUNK, -1)
    inv_masks = _block_inverse_masks(CHUNK)

    for c in range(nc):
        rows = slice(c * CHUNK, (c + 1) * CHUNK)
        gc = _hdot(tril, g_ref[rows, :])
        gct = gc.T
        gl = gc[CHUNK - 1:CHUNK, :]
        eg = jnp.exp(gc)
        eendt = jnp.exp(gl - gc).T
        dec = jnp.exp(gl)
        bt = bt_ref[rows, :]
        beg = bt * eg
        hq_r = range(GDN_QK_HEADS)
        hv_r = range(GDN_V_HEADS)
        qs = [q_ref[rows, hq * hd_:(hq + 1) * hd_] for hq in hq_r]
        ks = [k_ref[rows, hq * hd_:(hq + 1) * hd_] for hq in hq_r]
        kks = [_bdot_nt(ks[hq], ks[hq]) for hq in hq_r]
        qk0s = [_bdot_nt(qs[hq], ks[hq]) for hq in hq_r]
        khts = [ks[hq].T for hq in hq_r]
        cols = [slice(hvi, hvi + 1) for hvi in hv_r]
        decays = []
        for hvi in hv_r:
            diff = gc[:, cols[hvi]] - gct[hvi:hvi + 1, :]
            decays.append(jnp.where(lower, jnp.exp(jnp.where(lower, diff, 0.0)), 0.0))
        amats = [jnp.where(strict, bt[:, cols[hvi]] * kks[hvi // rep] * decays[hvi], 0.0) for hvi in hv_r]
        tinvs = _unit_lower_inverse(amats, inv_masks)
        rhs = [jnp.concatenate([bt[:, cols[hvi]] * v_ref[rows, hvi * hd_:(hvi + 1) * hd_],
                                beg[:, cols[hvi]] * ks[hvi // rep]], axis=1) for hvi in hv_r]
        uws = [_bdot(tinvs[hvi], rhs[hvi]) for hvi in hv_r]
        ss = [s_ref[hvi] for hvi in hv_r]
        wss = [_bdot(uws[hvi][:, hd_:], ss[hvi]) for hvi in hv_r]
        qss = [_bdot(qs[hvi // rep] * eg[:, cols[hvi]], ss[hvi]) for hvi in hv_r]
        vns = [uws[hvi][:, :hd_] - wss[hvi] for hvi in hv_r]
        outs = [qss[hvi] + _bdot(qk0s[hvi // rep] * decays[hvi], vns[hvi]) for hvi in hv_r]
        upds = [_bdot(khts[hvi // rep] * eendt[hvi:hvi + 1, :], vns[hvi]) for hvi in hv_r]
        for hvi in hv_r:
            ob_ref[rows, hvi * hd_:(hvi + 1) * hd_] = outs[hvi]
            s_ref[hvi] = dec[:, cols[hvi]] * ss[hvi] + upds[hvi]

    hg = hg_ref[...]
    parts = []
    for hvi in range(GDN_V_HEADS):
        vs = slice(hvi * hd_, (hvi + 1) * hd_)
        oh = ob_ref[:, vs]
        ms = jnp.mean(oh * oh, axis=-1, keepdims=True)
        parts.append((oh * lax.rsqrt(ms + RMS_EPS) * hg * _silu(z_ref[:, vs])).astype(BF16))
    y = jnp.concatenate(parts, axis=-1)
    m = jnp.dot(y, wout_ref[...], preferred_element_type=F32)
    o_ref[0] = x + gt_ref[0] * m


def _gdn_call(x, sh, sc, gt, ng, w_in, conv_w, a_log, dt_bias, head_g, w_out, *, tb):
    b, s, d = x.shape
    dqk = GDN_QK_HEADS * GDN_HEAD_DIM
    dvv = GDN_V_HEADS * GDN_HEAD_DIM
    nqkv = 2 * dqk + dvv
    wqkv = w_in[:, :nqkv].astype(BF16)
    wz = w_in[:, nqkv:nqkv + dvv].astype(BF16)
    padh = ((0, 0), (0, LANE - GDN_V_HEADS))
    wa = jnp.pad(w_in[:, nqkv + dvv:nqkv + dvv + GDN_V_HEADS], padh).astype(BF16)
    wb = jnp.pad(w_in[:, nqkv + dvv + GDN_V_HEADS:], padh).astype(BF16)
    alog = jnp.pad(a_log.reshape(1, -1), padh)
    dtb = jnp.pad(dt_bias.reshape(1, -1), padh)
    const2 = lambda bi, i: (0, 0)
    vec = pl.BlockSpec((1, 1, d), lambda bi, i: (bi, 0, 0))
    kern = functools.partial(_gdn_kernel, tb=tb)
    return pl.pallas_call(
        kern,
        grid=(b, s // tb),
        in_specs=[
            pl.BlockSpec((1, tb, d), lambda bi, i: (bi, i, 0)),
            vec, vec, vec,
            pl.BlockSpec((1, d), const2),
            pl.BlockSpec(wqkv.shape, const2),
            pl.BlockSpec(wz.shape, const2),
            pl.BlockSpec(wa.shape, const2),
            pl.BlockSpec(wb.shape, const2),
            pl.BlockSpec(conv_w.shape, const2),
            pl.BlockSpec((1, LANE), const2),
            pl.BlockSpec((1, LANE), const2),
            pl.BlockSpec((1, GDN_HEAD_DIM), const2),
            pl.BlockSpec((dvv, d), const2),
        ],
        out_specs=pl.BlockSpec((1, tb, d), lambda bi, i: (bi, i, 0)),
        out_shape=jax.ShapeDtypeStruct(x.shape, F32),
        scratch_shapes=[
            pltpu.VMEM((GDN_V_HEADS, GDN_HEAD_DIM, GDN_HEAD_DIM), F32),
            pltpu.VMEM((tb + SUBLANE, nqkv), F32),
            pltpu.VMEM((tb, dqk), F32),
            pltpu.VMEM((tb, dqk), F32),
            pltpu.VMEM((tb, dvv), F32),
            pltpu.VMEM((tb, dvv), F32),
            pltpu.VMEM((tb, LANE), F32),
            pltpu.VMEM((tb, LANE), F32),
            pltpu.VMEM((tb, dvv), F32),
        ],
        compiler_params=pltpu.CompilerParams(
            dimension_semantics=("arbitrary", "arbitrary"), vmem_limit_bytes=VMEM_LIMIT),
        name="gdn_layer",
    )(x, sh, sc, gt, ng.reshape(1, d), wqkv, wz, wa, wb, conv_w, alog, dtb,
      head_g.reshape(1, GDN_HEAD_DIM), w_out.astype(BF16))


def _top_values(x, n, with_rank=False):
    vals = []
    rank = jnp.full(x.shape, float(n), F32) if with_rank else None
    for r in range(n):
        m = jnp.max(x, axis=0, keepdims=True)
        vals.append(m)
        if with_rank:
            rank = jnp.where(x == m, float(r), rank)
        if r + 1 < n:
            x = jnp.where(x == m, -jnp.inf, x)
    return (vals, rank) if with_rank else vals


def _pair_pack(x):
    bits = lax.bitcast_convert_type(x.astype(BF16).astype(F32), jnp.int32)
    return bits | lax.shift_right_logical(bits, jnp.int32(16))


def _stack_rows(rows):
    l = rows[0].shape[1]
    sub = lax.broadcasted_iota(jnp.int32, (SUBLANE, l), 0)
    out = jnp.broadcast_to(rows[0], (SUBLANE, l))
    for i in range(1, len(rows)):
        out = jnp.where(sub == i, rows[i], out)
    return out


def _peer_kernel(x_ref, sh_ref, sc_ref, gt_ref, ng_ref, wq_ref, k1_ref, k2_ref, u_ref, vt_ref, o_ref,
                 tt_ref, s1_ref, s2_ref, r1_ref, w1_ref, jb_ref, e2_ref, yt_ref, *half_refs, tt, eb, ne):
    g = pl.program_id(2)
    nh = tt // PEER_NCOLS
    ht_refs = half_refs[:nh]
    at_refs = half_refs[nh:]
    nj = tt // LANE
    ab = eb // PEER_NKEYS
    k = PEER_TOPK

    @pl.when(g == 0)
    def _():
        x = x_ref[0]
        t = _norm_mod(x, ng_ref[...], sc_ref[0], sh_ref[0])
        tb = t.astype(BF16)
        tt_ref[...] = t.T.astype(BF16)
        q = jnp.dot(tb, wq_ref[...], preferred_element_type=F32)
        for h in range(PEER_HEADS):
            q1 = q[:, (2 * h) * PEER_HALF:(2 * h + 1) * PEER_HALF]
            q2 = q[:, (2 * h + 1) * PEER_HALF:(2 * h + 2) * PEER_HALF]
            s1_ref[h] = lax.dot_general(k1_ref[...], q1, NT, precision=HI, preferred_element_type=F32)
            s2_ref[h] = lax.dot_general(k2_ref[...], q2, NT, precision=HI, preferred_element_type=F32)
        yt_ref[...] = jnp.zeros_like(yt_ref)
        for ht_ref in ht_refs:
            ht_ref[...] = jnp.zeros_like(ht_ref)

        def route_one(idx):
            h = idx // nj
            c0 = pl.multiple_of((idx % nj) * LANE, LANE)
            cols = pl.ds(c0, LANE)
            x1 = s1_ref[h, :, cols]
            x2 = s2_ref[h, :, cols]
            v1, rank1 = _top_values(x1, k + 1, with_rank=True)
            v2 = _top_values(x2, k + 1)
            v2lo = _stack_rows(v2[:SUBLANE])
            v2hi = _stack_rows(v2[SUBLANE:2 * SUBLANE])
            v1hi = _stack_rows(v1[SUBLANE:2 * SUBLANE])
            sub = lax.broadcasted_iota(jnp.int32, (SUBLANE, LANE), 0)
            last = jnp.where(sub == 0, v1[0] + v2[k], jnp.where(sub == 1, v1[k] + v2[0], -jnp.inf))
            tiles = [v1[0] + v2lo, v1[0] + v2hi, v1hi + v2[0], last]
            tiles += [v1[i] + v2lo for i in range(1, SUBLANE)]
            cand = jnp.concatenate(tiles, axis=0)
            tops = _top_values(cand, k + 1)
            kth = tops[k - 1]
            cut = 0.5 * (kth + tops[k])
            top = v1[0] + v2[0]
            z = jnp.sum(jnp.where(cand >= kth, jnp.exp(cand - top), 0.0), axis=0, keepdims=True)
            c2 = cut - x2
            passing = jnp.zeros_like(c2)
            for i in range(k + 1):
                passing = passing + jnp.where(v1[i] >= c2, 1.0, 0.0)
            r1_ref[h, :, cols] = _pair_pack(rank1)
            w1_ref[h, :, cols] = _pair_pack(jnp.exp(x1 - v1[0]) / z)
            jb_ref[h, :, cols] = passing.astype(BF16)
            e2_ref[h, :, cols] = jnp.exp(x2 - v2[0]).astype(BF16)

        def route(i, carry):
            route_one(2 * i)
            route_one(2 * i + 1)
            return carry

        lax.fori_loop(0, PEER_HEADS * nj // 2, route, 0)

    em = jnp.clip(g - 1, 0, ne - 1)
    arows = pl.ds(pl.multiple_of(em * ab, SUBLANE), SUBLANE)
    bh = PEER_NKEYS // 2
    d = yt_ref.shape[0]
    ngroup = PEER_NCOLS // LANE

    def gate_block(n, jj, ag, hb):
        cols = slice(n * PEER_NCOLS + jj * LANE, n * PEER_NCOLS + (jj + 1) * LANE)
        lcols = slice(jj * LANE, (jj + 1) * LANE)
        brows = slice(hb * bh, (hb + 1) * bh)
        als = tuple(range(ag * PEER_AROWS, (ag + 1) * PEER_AROWS))
        r1t = [r1_ref[h, arows, cols] for h in range(PEER_HEADS)]
        w1t = [w1_ref[h, arows, cols] for h in range(PEER_HEADS)]

        def row(tile, al):
            return pltpu.bitcast(jnp.broadcast_to(tile[al:al + 1, :], (SUBLANE, LANE)), BF16)

        nv = bh // PACK
        accs = [[jnp.zeros((PACK, LANE), BF16) for _ in range(nv)] for _ in als]
        for h in range(PEER_HEADS):
            jb = jb_ref[h, brows, cols]
            e2 = e2_ref[h, brows, cols]
            r1rows = [row(r1t[h], al) for al in als]
            w1rows = [row(w1t[h], al) for al in als]
            for v in range(nv):
                vr = slice(v * PACK, (v + 1) * PACK)
                jbv = jb[vr]
                e2v = e2[vr]
                for i in range(len(als)):
                    accs[i][v] = accs[i][v] + jnp.where(r1rows[i] < jbv, e2v * w1rows[i], jnp.zeros_like(e2v))
        fold = None
        for i, al in enumerate(als):
            for v in range(nv):
                r0 = al * PEER_NKEYS + hb * bh + v * PACK
                out = accs[i][v] * _gelu_tanh(ht_refs[n][r0:r0 + PACK, lcols]).astype(BF16)
                at_refs[n][r0:r0 + PACK, lcols] = out
                bits = pltpu.bitcast(out, jnp.int32)
                fold = bits if fold is None else fold | bits
        return fold & jnp.minimum(g, 0)

    def after(zero, rhs):
        head = pltpu.bitcast(rhs[:PEER_KTILE], jnp.int32)
        head = head | jnp.tile(zero, (head.shape[0] // SUBLANE, head.shape[1] // LANE))
        return jnp.concatenate([pltpu.bitcast(head, BF16), rhs[PEER_KTILE:]], axis=0)

    def matmul_pieces(n):
        ncols = slice(n * PEER_NCOLS, (n + 1) * PEER_NCOLS)
        pieces = []
        for m in range(d // PEER_MROWS):
            mrows = slice(m * PEER_MROWS, (m + 1) * PEER_MROWS)

            def second(zero, mrows=mrows):
                yt_ref[mrows, ncols] += jnp.dot(vt_ref[0, mrows, :], after(zero, at_refs[n][...]),
                                                preferred_element_type=F32)

            def first(zero, mrows=mrows):
                ht_refs[n][mrows, :] = jnp.dot(u_ref[mrows, :], after(zero, tt_ref[:, ncols]),
                                               preferred_element_type=F32)
            pieces += [second, first]
        return pieces

    pending = []
    for n in range(tt // PEER_NCOLS):
        blk = 0
        for jj in range(ngroup):
            for ag in range(ab // PEER_AROWS):
                for hb in range(2):
                    zero = gate_block(n, jj, ag, hb)
                    if pending and blk % PEER_GATE_PER_PIECE == 1:
                        pending.pop(0)(zero)
                    blk += 1
        for piece in pending:
            piece(zero)
        pending = matmul_pieces(n)
    for piece in pending:
        piece(zero)

    @pl.when(g == ne)
    def _():
        o_ref[0] = x_ref[0] + gt_ref[0] * yt_ref[...].T


def _peer_call(x, sh, sc, gt, ng, wq, k1, k2, u, v, *, tt, eb):
    b, s, d = x.shape
    assert eb == SUBLANE * PEER_NKEYS and u.shape[0] % eb == 0
    ne = u.shape[0] // eb
    ub = u.astype(BF16)
    vtb = v.astype(BF16).reshape(ne, eb, d).transpose(0, 2, 1)
    const3 = lambda bi, i, e: (0, 0)
    vec = pl.BlockSpec((1, 1, d), lambda bi, i, e: (bi, 0, 0))
    kern = functools.partial(_peer_kernel, tt=tt, eb=eb, ne=ne)
    sshape = (PEER_HEADS, PEER_NKEYS, tt)
    return pl.pallas_call(
        kern,
        grid=(b, s // tt, ne + 1),
        in_specs=[
            pl.BlockSpec((1, tt, d), lambda bi, i, e: (bi, i, 0)),
            vec, vec, vec,
            pl.BlockSpec((1, d), const3),
            pl.BlockSpec(wq.shape, const3),
            pl.BlockSpec(k1.shape, const3),
            pl.BlockSpec(k2.shape, const3),
            pl.BlockSpec((eb, d), lambda bi, i, g: (jnp.minimum(g, ne - 1), 0)),
            pl.BlockSpec((1, d, eb), lambda bi, i, g: (jnp.maximum(g - 1, 0), 0, 0)),
        ],
        out_specs=pl.BlockSpec((1, tt, d), lambda bi, i, e: (bi, i, 0)),
        out_shape=jax.ShapeDtypeStruct(x.shape, F32),
        scratch_shapes=[
            pltpu.VMEM((d, tt), BF16),
            pltpu.VMEM(sshape, F32),
            pltpu.VMEM(sshape, F32),
            pltpu.VMEM(sshape, jnp.int32),
            pltpu.VMEM(sshape, jnp.int32),
            pltpu.VMEM(sshape, BF16),
            pltpu.VMEM(sshape, BF16),
            pltpu.VMEM((d, tt), F32),
        ] + [pltpu.VMEM((eb, PEER_NCOLS), F32)] * (tt // PEER_NCOLS)
          + [pltpu.VMEM((eb, PEER_NCOLS), BF16)] * (tt // PEER_NCOLS),
        compiler_params=pltpu.CompilerParams(
            dimension_semantics=("arbitrary", "arbitrary", "arbitrary"), vmem_limit_bytes=VMEM_LIMIT),
        name="peer_layer",
    )(x, sh, sc, gt, ng.reshape(1, d), wq.astype(BF16), k1, k2, ub, vtb)


def _final_kernel(x_ref, g_ref, o_ref):
    x = x_ref[0]
    ms = jnp.mean(x * x, axis=-1, keepdims=True)
    o_ref[0] = x * lax.rsqrt(ms + RMS_EPS) * g_ref[...]


def _final_call(x, g, *, tb):
    b, s, d = x.shape
    return pl.pallas_call(
        _final_kernel,
        grid=(b, s // tb),
        in_specs=[pl.BlockSpec((1, tb, d), lambda bi, i: (bi, i, 0)),
                  pl.BlockSpec((1, d), lambda bi, i: (0, 0))],
        out_specs=pl.BlockSpec((1, tb, d), lambda bi, i: (bi, i, 0)),
        out_shape=jax.ShapeDtypeStruct(x.shape, F32),
        name="final_norm",
    )(x, g.reshape(1, d))


def _pick(n, pref):
    return pref if n % pref == 0 else n


def kernel(x, c, ada_w, ada_b, norm_mix_g, norm_ffn_g, gla_w_in, gla_w_g2, gla_b_g2, gla_norm_g,
           gla_w_out, gdn_w_in, gdn_conv_w, gdn_a_log, gdn_dt_bias, gdn_norm_g, gdn_w_out,
           peer_wq, peer_k1, peer_k2, peer_u, peer_v, final_norm_g):
    b, s, d = x.shape
    depth = ada_w.shape[0]
    c_pad = jnp.pad(c, ((0, SUBLANE - b % SUBLANE if b % SUBLANE else 0), (0, 0)))
    mod = _mod_call(c_pad, ada_w, ada_b)[:, :b, :].reshape(depth, b, 6, 1, d)
    tb_mix = _pick(s, 256)
    tt = _pick(s, 512)
    eb = 1024
    for i in range(depth):
        sh1, sc1, gt1, sh2, sc2, gt2 = (mod[i, :, j] for j in range(6))
        j = i // 2
        if i % 2 == 0:
            x = _gla_call(x, sh1, sc1, gt1, norm_mix_g[i], gla_w_in[j], gla_w_g2[j], gla_b_g2[j],
                          gla_norm_g[j], gla_w_out[j], tb=tb_mix)
        else:
            x = _gdn_call(x, sh1, sc1, gt1, norm_mix_g[i], gdn_w_in[j], gdn_conv_w[j], gdn_a_log[j],
                          gdn_dt_bias[j], gdn_norm_g[j], gdn_w_out[j], tb=tb_mix)
        x = _peer_call(x, sh2, sc2, gt2, norm_ffn_g[i], peer_wq[i], peer_k1[i], peer_k2[i],
                       peer_u[i], peer_v[i], tt=tt, eb=eb)
    return _final_call(x, final_norm_g, tb=_pick(s, 512))
```

```python
import functools

import jax
import jax.numpy as jnp
from jax import lax
from jax.experimental import pallas as pl
from jax.experimental.pallas import tpu as pltpu

F32 = jnp.float32
BF16 = jnp.bfloat16
HI = lax.Precision.HIGHEST

RMS_EPS = 1e-6
CHUNK = 64

GLA_HEADS = 4
GLA_GATE_RANK = 16
GLA_TAU = 16.0

GDN_QK_HEADS = 8
GDN_V_HEADS = 16
GDN_HEAD_DIM = 128
GDN_CONV = 4

PEER_HEADS = 8
PEER_TOPK = 16
PEER_NKEYS = 128
PEER_HALF = 128
PEER_NCOLS = 256
PEER_KTILE = 256
PEER_MROWS = 1024
PEER_AROWS = 4
PEER_GATE_PER_PIECE = 4

LANE = 128
SUBLANE = 8
PACK = 16
VMEM_LIMIT = 56 * 1024 * 1024

NT = (((1,), (1,)), ((), ()))


def _sigmoid(x):
    return 1.0 / (1.0 + jnp.exp(-x))


def _silu(x):
    return x * _sigmoid(x)


def _softplus(x):
    return jnp.maximum(x, 0.0) + jnp.log1p(jnp.exp(-jnp.abs(x)))


def _gelu_tanh(x):
    c = 0.7978845608028654
    hx = 0.5 * x
    return hx + hx * jnp.tanh(x * (c + (c * 0.044715) * (x * x)))


def _norm_mod(x, g, sc, sh):
    ms = jnp.mean(x * x, axis=-1, keepdims=True)
    return (x * lax.rsqrt(ms + RMS_EPS) * g) * (1.0 + sc) + sh


def _bdot(a, b):
    return jnp.dot(a.astype(BF16), b.astype(BF16), preferred_element_type=F32)


def _bdot_nt(a, b):
    return lax.dot_general(a.astype(BF16), b.astype(BF16), NT, preferred_element_type=F32)


def _hdot(a, b):
    return jnp.dot(a, b, precision=HI, preferred_element_type=F32)


def _tril(n, k=0):
    r = lax.broadcasted_iota(jnp.int32, (n, n), 0)
    c = lax.broadcasted_iota(jnp.int32, (n, n), 1)
    return (c - r) <= k


def _mod_kernel(c_ref, w_ref, b_ref, o_ref):
    c = c_ref[...]
    o_ref[0] = _hdot(_silu(c), w_ref[0]) + b_ref[0]


def _mod_call(c_pad, ada_w, ada_b):
    depth, d, d6 = ada_w.shape
    nb = d6 // d
    return pl.pallas_call(
        _mod_kernel,
        grid=(depth, nb),
        in_specs=[
            pl.BlockSpec(c_pad.shape, lambda i, j: (0, 0)),
            pl.BlockSpec((1, d, d), lambda i, j: (i, 0, j)),
            pl.BlockSpec((1, 1, d), lambda i, j: (i, 0, j)),
        ],
        out_specs=pl.BlockSpec((1, c_pad.shape[0], d), lambda i, j: (i, 0, j)),
        out_shape=jax.ShapeDtypeStruct((depth, c_pad.shape[0], d6), F32),
        name="adaln_mod",
    )(c_pad, ada_w, ada_b.reshape(depth, 1, d6))


def _gla_kernel(x_ref, sh_ref, sc_ref, gt_ref, ng_ref, wqkvr_ref, wg1_ref, wg2_ref, bg2_ref,
                hg_ref, wout_ref, o_ref, st_ref, p_ref, la_ref, ob_ref, *, tb, dk, dv):
    hk = dk // GLA_HEADS
    hv = dv // GLA_HEADS
    nc = tb // CHUNK

    @pl.when(pl.program_id(1) == 0)
    def _():
        st_ref[...] = jnp.zeros_like(st_ref)

    x = x_ref[0]
    h = _norm_mod(x, ng_ref[...], sc_ref[0], sh_ref[0])
    hb = h.astype(BF16)
    p_ref[...] = jnp.dot(hb, wqkvr_ref[...], preferred_element_type=F32)
    g1 = jnp.dot(hb, wg1_ref[...], preferred_element_type=F32)
    g = jnp.dot(g1.astype(BF16), wg2_ref[...], preferred_element_type=F32) + bg2_ref[...]
    la_ref[...] = (jnp.minimum(g, 0.0) - jnp.log1p(jnp.exp(-jnp.abs(g)))) * (1.0 / GLA_TAU)

    tril = _tril(CHUNK).astype(F32)
    causal = _tril(CHUNK)
    scale = hk ** -0.5

    for c in range(nc):
        rows = slice(c * CHUNK, (c + 1) * CHUNK)
        cum = _hdot(tril, la_ref[rows, :])
        cl = cum[CHUNK - 1:CHUNK, :]
        e_pos = jnp.exp(cum)
        e_neg = jnp.exp(-cum)
        e_end = jnp.exp(cl - cum)
        dec = jnp.exp(cl)
        hr = range(GLA_HEADS)
        ksl = [slice(hd * hk, (hd + 1) * hk) for hd in hr]
        qis = [p_ref[rows, hd * hk:(hd + 1) * hk] * (scale * e_pos[:, ksl[hd]]) for hd in hr]
        kraw = [p_ref[rows, dk + hd * hk:dk + (hd + 1) * hk] for hd in hr]
        vs = [p_ref[rows, 2 * dk + hd * hv:2 * dk + (hd + 1) * hv] for hd in hr]
        attns = [jnp.where(causal, _bdot_nt(qis[hd], kraw[hd] * e_neg[:, ksl[hd]]), 0.0) for hd in hr]
        sts = [st_ref[hd] for hd in hr]
        inter = [_bdot_nt(qis[hd], sts[hd]) for hd in hr]
        upds = [_bdot(vs[hd].T, kraw[hd] * e_end[:, ksl[hd]]) for hd in hr]
        intra = [_bdot(attns[hd], vs[hd]) for hd in hr]
        for hd in hr:
            ob_ref[rows, hd * hv:(hd + 1) * hv] = intra[hd] + inter[hd]
            st_ref[hd] = dec[:, ksl[hd]] * sts[hd] + upds[hd]

    hg = hg_ref[...]
    parts = []
    for hd in range(GLA_HEADS):
        oh = ob_ref[:, hd * hv:(hd + 1) * hv]
        ms = jnp.mean(oh * oh, axis=-1, keepdims=True)
        r = p_ref[:, 2 * dk + dv + hd * hv:2 * dk + dv + (hd + 1) * hv]
        parts.append((oh * lax.rsqrt(ms + RMS_EPS) * hg * _silu(r)).astype(BF16))
    y = jnp.concatenate(parts, axis=-1)
    m = jnp.dot(y, wout_ref[...], preferred_element_type=F32)
    o_ref[0] = x + gt_ref[0] * m


def _gla_call(x, sh, sc, gt, ng, w_in, w_g2, b_g2, head_g, w_out, *, tb):
    b, s, d = x.shape
    dk = w_g2.shape[1]
    dv = w_out.shape[0]
    wqkvr = w_in[:, :2 * dk + 2 * dv].astype(BF16)
    wg1 = jnp.pad(w_in[:, 2 * dk + 2 * dv:], ((0, 0), (0, LANE - GLA_GATE_RANK))).astype(BF16)
    wg2 = jnp.pad(w_g2, ((0, LANE - GLA_GATE_RANK), (0, 0))).astype(BF16)
    hv = dv // GLA_HEADS
    hk = dk // GLA_HEADS
    const2 = lambda bi, i: (0, 0)
    vec = pl.BlockSpec((1, 1, d), lambda bi, i: (bi, 0, 0))
    kern = functools.partial(_gla_kernel, tb=tb, dk=dk, dv=dv)
    return pl.pallas_call(
        kern,
        grid=(b, s // tb),
        in_specs=[
            pl.BlockSpec((1, tb, d), lambda bi, i: (bi, i, 0)),
            vec, vec, vec,
            pl.BlockSpec((1, d), const2),
            pl.BlockSpec(wqkvr.shape, const2),
            pl.BlockSpec(wg1.shape, const2),
            pl.BlockSpec(wg2.shape, const2),
            pl.BlockSpec((1, dk), const2),
            pl.BlockSpec((1, hv), const2),
            pl.BlockSpec((dv, d), const2),
        ],
        out_specs=pl.BlockSpec((1, tb, d), lambda bi, i: (bi, i, 0)),
        out_shape=jax.ShapeDtypeStruct(x.shape, F32),
        scratch_shapes=[
            pltpu.VMEM((GLA_HEADS, hv, hk), F32),
            pltpu.VMEM((tb, 2 * dk + 2 * dv), F32),
            pltpu.VMEM((tb, dk), F32),
            pltpu.VMEM((tb, dv), F32),
        ],
        compiler_params=pltpu.CompilerParams(
            dimension_semantics=("arbitrary", "arbitrary"), vmem_limit_bytes=VMEM_LIMIT),
        name="gla_layer",
    )(x, sh, sc, gt, ng.reshape(1, d), wqkvr, wg1, wg2, b_g2.reshape(1, dk),
      head_g.reshape(1, hv), w_out.astype(BF16))


def _block_inverse_masks(n):
    r = lax.broadcasted_iota(jnp.int32, (n, n), 0)
    c = lax.broadcasted_iota(jnp.int32, (n, n), 1)
    eye = (r == c).astype(F32)
    first = (r // 2) == (c // 2)
    quads = []
    s = 2
    while s < n:
        quads.append(((r // (2 * s)) == (c // (2 * s))) & ((r // s) != (c // s)))
        s *= 2
    return eye, first, quads


def _unit_lower_inverse(a_list, masks):
    eye, first, quads = masks
    ds = [eye - jnp.where(first, a, 0.0) for a in a_list]
    for quad in quads:
        ms = [_bdot(d, jnp.where(quad, a, 0.0)) for d, a in zip(ds, a_list)]
        ds = [d - _bdot(m, d) for d, m in zip(ds, ms)]
    return ds


def _gdn_kernel(x_ref, sh_ref, sc_ref, gt_ref, ng_ref, wqkv_ref, wz_ref, wa_ref, wb_ref, cw_ref,
                alog_ref, dtb_ref, hg_ref, wout_ref, o_ref,
                s_ref, cb_ref, q_ref, k_ref, v_ref, z_ref, g_ref, bt_ref, ob_ref, *, tb):
    dqk = GDN_QK_HEADS * GDN_HEAD_DIM
    dvv = GDN_V_HEADS * GDN_HEAD_DIM
    hd_ = GDN_HEAD_DIM
    nc = tb // CHUNK
    rep = GDN_V_HEADS // GDN_QK_HEADS

    @pl.when(pl.program_id(1) == 0)
    def _():
        s_ref[...] = jnp.zeros_like(s_ref)
        cb_ref[0:SUBLANE, :] = jnp.zeros((SUBLANE, cb_ref.shape[1]), F32)

    x = x_ref[0]
    h = _norm_mod(x, ng_ref[...], sc_ref[0], sh_ref[0])
    hb = h.astype(BF16)
    cb_ref[SUBLANE:SUBLANE + tb, :] = jnp.dot(hb, wqkv_ref[...], preferred_element_type=F32)
    z_ref[...] = jnp.dot(hb, wz_ref[...], preferred_element_type=F32)
    a = jnp.dot(hb, wa_ref[...], preferred_element_type=F32)
    bb = jnp.dot(hb, wb_ref[...], preferred_element_type=F32)
    g_ref[...] = -jnp.exp(alog_ref[...]) * _softplus(a + dtb_ref[...])
    bt_ref[...] = _sigmoid(bb)

    off = SUBLANE - (GDN_CONV - 1)
    conv = cw_ref[0:1, :] * cb_ref[off:off + tb, :]
    for j in range(1, GDN_CONV):
        conv = conv + cw_ref[j:j + 1, :] * cb_ref[off + j:off + j + tb, :]
    cb_ref[0:SUBLANE, :] = cb_ref[tb:tb + SUBLANE, :]
    qkv = _silu(conv)
    qscale = hd_ ** -0.5
    for hq in range(GDN_QK_HEADS):
        sl = slice(hq * hd_, (hq + 1) * hd_)
        qh = qkv[:, hq * hd_:(hq + 1) * hd_]
        kh = qkv[:, dqk + hq * hd_:dqk + (hq + 1) * hd_]
        q_ref[:, sl] = qh * (lax.rsqrt(jnp.sum(qh * qh, axis=-1, keepdims=True) + 1e-6) * qscale)
        k_ref[:, sl] = kh * lax.rsqrt(jnp.sum(kh * kh, axis=-1, keepdims=True) + 1e-6)
    v_ref[...] = qkv[:, 2 * dqk:]

    tril = _tril(CHUNK).astype(F32)
    lower = _tril(CHUNK)
    strict = _tril(CHUNK, -1)
    inv_masks = _block_inverse_masks(CHUNK)

    for c in range(nc):
        rows = slice(c * CHUNK, (c + 1) * CHUNK)
        gc = _hdot(tril, g_ref[rows, :])
        gct = gc.T
        gl = gc[CHUNK - 1:CHUNK, :]
        eg = jnp.exp(gc)
        eendt = jnp.exp(gl - gc).T
        dec = jnp.exp(gl)
        bt = bt_ref[rows, :]
        beg = bt * eg
        hq_r = range(GDN_QK_HEADS)
        hv_r = range(GDN_V_HEADS)
        qs = [q_ref[rows, hq * hd_:(hq + 1) * hd_] for hq in hq_r]
        ks = [k_ref[rows, hq * hd_:(hq + 1) * hd_] for hq in hq_r]
        kks = [_bdot_nt(ks[hq], ks[hq]) for hq in hq_r]
        qk0s = [_bdot_nt(qs[hq], ks[hq]) for hq in hq_r]
        khts = [ks[hq].T for hq in hq_r]
        cols = [slice(hvi, hvi + 1) for hvi in hv_r]
        decays = []
        for hvi in hv_r:
            diff = gc[:, cols[hvi]] - gct[hvi:hvi + 1, :]
            decays.append(jnp.where(lower, jnp.exp(jnp.where(lower, diff, 0.0)), 0.0))
        amats = [jnp.where(strict, bt[:, cols[hvi]] * kks[hvi // rep] * decays[hvi], 0.0) for hvi in hv_r]
        tinvs = _unit_lower_inverse(amats, inv_masks)
        rhs = [jnp.concatenate([bt[:, cols[hvi]] * v_ref[rows, hvi * hd_:(hvi + 1) * hd_],
                                beg[:, cols[hvi]] * ks[hvi // rep]], axis=1) for hvi in hv_r]
        uws = [_bdot(tinvs[hvi], rhs[hvi]) for hvi in hv_r]
        ss = [s_ref[hvi] for hvi in hv_r]
        wss = [_bdot(uws[hvi][:, hd_:], ss[hvi]) for hvi in hv_r]
        qss = [_bdot(qs[hvi // rep] * eg[:, cols[hvi]], ss[hvi]) for hvi in hv_r]
        vns = [uws[hvi][:, :hd_] - wss[hvi] for hvi in hv_r]
        outs = [qss[hvi] + _bdot(qk0s[hvi // rep] * decays[hvi], vns[hvi]) for hvi in hv_r]
        upds = [_bdot(khts[hvi // rep] * eendt[hvi:hvi + 1, :], vns[hvi]) for hvi in hv_r]
        for hvi in hv_r:
            ob_ref[rows, hvi * hd_:(hvi + 1) * hd_] = outs[hvi]
            s_ref[hvi] = dec[:, cols[hvi]] * ss[hvi] + upds[hvi]

    hg = hg_ref[...]
    parts = []
    for hvi in range(GDN_V_HEADS):
        vs = slice(hvi * hd_, (hvi + 1) * hd_)
        oh = ob_ref[:, vs]
        ms = jnp.mean(oh * oh, axis=-1, keepdims=True)
        parts.append((oh * lax.rsqrt(ms + RMS_EPS) * hg * _silu(z_ref[:, vs])).astype(BF16))
    y = jnp.concatenate(parts, axis=-1)
    m = jnp.dot(y, wout_ref[...], preferred_element_type=F32)
    o_ref[0] = x + gt_ref[0] * m


def _gdn_call(x, sh, sc, gt, ng, w_in, conv_w, a_log, dt_bias, head_g, w_out, *, tb):
    b, s, d = x.shape
    dqk = GDN_QK_HEADS * GDN_HEAD_DIM
    dvv = GDN_V_HEADS * GDN_HEAD_DIM
    nqkv = 2 * dqk + dvv
    wqkv = w_in[:, :nqkv].astype(BF16)
    wz = w_in[:, nqkv:nqkv + dvv].astype(BF16)
    padh = ((0, 0), (0, LANE - GDN_V_HEADS))
    wa = jnp.pad(w_in[:, nqkv + dvv:nqkv + dvv + GDN_V_HEADS], padh).astype(BF16)
    wb = jnp.pad(w_in[:, nqkv + dvv + GDN_V_HEADS:], padh).astype(BF16)
    alog = jnp.pad(a_log.reshape(1, -1), padh)
    dtb = jnp.pad(dt_bias.reshape(1, -1), padh)
    const2 = lambda bi, i: (0, 0)
    vec = pl.BlockSpec((1, 1, d), lambda bi, i: (bi, 0, 0))
    kern = functools.partial(_gdn_kernel, tb=tb)
    return pl.pallas_call(
        kern,
        grid=(b, s // tb),
        in_specs=[
            pl.BlockSpec((1, tb, d), lambda bi, i: (bi, i, 0)),
            vec, vec, vec,
            pl.BlockSpec((1, d), const2),
            pl.BlockSpec(wqkv.shape, const2),
            pl.BlockSpec(wz.shape, const2),
            pl.BlockSpec(wa.shape, const2),
            pl.BlockSpec(wb.shape, const2),
            pl.BlockSpec(conv_w.shape, const2),
            pl.BlockSpec((1, LANE), const2),
            pl.BlockSpec((1, LANE), const2),
            pl.BlockSpec((1, GDN_HEAD_DIM), const2),
            pl.BlockSpec((dvv, d), const2),
        ],
        out_specs=pl.BlockSpec((1, tb, d), lambda bi, i: (bi, i, 0)),
        out_shape=jax.ShapeDtypeStruct(x.shape, F32),
        scratch_shapes=[
            pltpu.VMEM((GDN_V_HEADS, GDN_HEAD_DIM, GDN_HEAD_DIM), F32),
            pltpu.VMEM((tb + SUBLANE, nqkv), F32),
            pltpu.VMEM((tb, dqk), F32),
            pltpu.VMEM((tb, dqk), F32),
            pltpu.VMEM((tb, dvv), F32),
            pltpu.VMEM((tb, dvv), F32),
            pltpu.VMEM((tb, LANE), F32),
            pltpu.VMEM((tb, LANE), F32),
            pltpu.VMEM((tb, dvv), F32),
        ],
        compiler_params=pltpu.CompilerParams(
            dimension_semantics=("arbitrary", "arbitrary"), vmem_limit_bytes=VMEM_LIMIT),
        name="gdn_layer",
    )(x, sh, sc, gt, ng.reshape(1, d), wqkv, wz, wa, wb, conv_w, alog, dtb,
      head_g.reshape(1, GDN_HEAD_DIM), w_out.astype(BF16))


def _top_values(x, n, with_rank=False):
    vals = []
    rank = jnp.full(x.shape, float(n), F32) if with_rank else None
    for r in range(n):
        m = jnp.max(x, axis=0, keepdims=True)
        vals.append(m)
        if with_rank:
            rank = jnp.where(x == m, float(r), rank)
        if r + 1 < n:
            x = jnp.where(x == m, -jnp.inf, x)
    return (vals, rank) if with_rank else vals


def _pair_pack(x):
    bits = lax.bitcast_convert_type(x.astype(BF16).astype(F32), jnp.int32)
    return bits | lax.shift_right_logical(bits, jnp.int32(16))


def _stack_rows(rows):
    l = rows[0].shape[1]
    sub = lax.broadcasted_iota(jnp.int32, (SUBLANE, l), 0)
    out = jnp.broadcast_to(rows[0], (SUBLANE, l))
    for i in range(1, len(rows)):
        out = jnp.where(sub == i, rows[i], out)
    return out


def _peer_kernel(x_ref, sh_ref, sc_ref, gt_ref, ng_ref, fg_ref, wq_ref, k1_ref, k2_ref, u_ref, vt_ref, o_ref,
                 tt_ref, s1_ref, s2_ref, r1_ref, w1_ref, jb_ref, e2_ref, yt_ref, *half_refs, tt, eb, ne, final):
    g = pl.program_id(2)
    nh = tt // PEER_NCOLS
    ht_refs = half_refs[:nh]
    at_refs = half_refs[nh:]
    nj = tt // LANE
    ab = eb // PEER_NKEYS
    k = PEER_TOPK

    @pl.when(g == 0)
    def _():
        x = x_ref[0]
        t = _norm_mod(x, ng_ref[...], sc_ref[0], sh_ref[0])
        tb = t.astype(BF16)
        tt_ref[...] = t.T.astype(BF16)
        q = jnp.dot(tb, wq_ref[...], preferred_element_type=F32)
        for h in range(PEER_HEADS):
            q1 = q[:, (2 * h) * PEER_HALF:(2 * h + 1) * PEER_HALF]
            q2 = q[:, (2 * h + 1) * PEER_HALF:(2 * h + 2) * PEER_HALF]
            s1_ref[h] = lax.dot_general(k1_ref[...], q1, NT, precision=HI, preferred_element_type=F32)
            s2_ref[h] = lax.dot_general(k2_ref[...], q2, NT, precision=HI, preferred_element_type=F32)
        yt_ref[...] = jnp.zeros_like(yt_ref)
        for ht_ref in ht_refs:
            ht_ref[...] = jnp.zeros_like(ht_ref)

        def route_one(idx):
            h = idx // nj
            c0 = pl.multiple_of((idx % nj) * LANE, LANE)
            cols = pl.ds(c0, LANE)
            x1 = s1_ref[h, :, cols]
            x2 = s2_ref[h, :, cols]
            v1, rank1 = _top_values(x1, k + 1, with_rank=True)
            v2 = _top_values(x2, k + 1)
            v2lo = _stack_rows(v2[:SUBLANE])
            v2hi = _stack_rows(v2[SUBLANE:2 * SUBLANE])
            v1hi = _stack_rows(v1[SUBLANE:2 * SUBLANE])
            sub = lax.broadcasted_iota(jnp.int32, (SUBLANE, LANE), 0)
            last = jnp.where(sub == 0, v1[0] + v2[k], jnp.where(sub == 1, v1[k] + v2[0], -jnp.inf))
            tiles = [v1[0] + v2lo, v1[0] + v2hi, v1hi + v2[0], last]
            tiles += [v1[i] + v2lo for i in range(1, SUBLANE)]
            cand = jnp.concatenate(tiles, axis=0)
            tops = _top_values(cand, k + 1)
            kth = tops[k - 1]
            cut = 0.5 * (kth + tops[k])
            top = v1[0] + v2[0]
            z = jnp.sum(jnp.where(cand >= kth, jnp.exp(cand - top), 0.0), axis=0, keepdims=True)
            c2 = cut - x2
            passing = jnp.zeros_like(c2)
            for i in range(k + 1):
                passing = passing + jnp.where(v1[i] >= c2, 1.0, 0.0)
            r1_ref[h, :, cols] = _pair_pack(rank1)
            w1_ref[h, :, cols] = _pair_pack(jnp.exp(x1 - v1[0]) / z)
            jb_ref[h, :, cols] = passing.astype(BF16)
            e2_ref[h, :, cols] = jnp.exp(x2 - v2[0]).astype(BF16)

        def route(i, carry):
            route_one(2 * i)
            route_one(2 * i + 1)
            return carry

        lax.fori_loop(0, PEER_HEADS * nj // 2, route, 0)

    em = jnp.clip(g - 1, 0, ne - 1)
    arows = pl.ds(pl.multiple_of(em * ab, SUBLANE), SUBLANE)
    bh = PEER_NKEYS // 2
    d = yt_ref.shape[0]
    ngroup = PEER_NCOLS // LANE

    def gate_block(n, jj, ag, hb):
        cols = slice(n * PEER_NCOLS + jj * LANE, n * PEER_NCOLS + (jj + 1) * LANE)
        lcols = slice(jj * LANE, (jj + 1) * LANE)
        brows = slice(hb * bh, (hb + 1) * bh)
        als = tuple(range(ag * PEER_AROWS, (ag + 1) * PEER_AROWS))
        r1t = [r1_ref[h, arows, cols] for h in range(PEER_HEADS)]
        w1t = [w1_ref[h, arows, cols] for h in range(PEER_HEADS)]

        def row(tile, al):
            return pltpu.bitcast(jnp.broadcast_to(tile[al:al + 1, :], (SUBLANE, LANE)), BF16)

        nv = bh // PACK
        accs = [[None] * nv for _ in als]
        for h in range(PEER_HEADS):
            jb = jb_ref[h, brows, cols]
            e2 = e2_ref[h, brows, cols]
            r1rows = [row(r1t[h], al) for al in als]
            w1rows = [row(w1t[h], al) for al in als]
            for v in range(nv):
                vr = slice(v * PACK, (v + 1) * PACK)
                jbv = jb[vr]
                e2v = e2[vr]
                for i in range(len(als)):
                    term = jnp.where(r1rows[i] < jbv, e2v * w1rows[i], jnp.zeros_like(e2v))
                    accs[i][v] = term if h == 0 else accs[i][v] + term
        fold = None
        for i, al in enumerate(als):
            for v in range(nv):
                r0 = al * PEER_NKEYS + hb * bh + v * PACK
                out = accs[i][v] * _gelu_tanh(ht_refs[n][r0:r0 + PACK, lcols]).astype(BF16)
                at_refs[n][r0:r0 + PACK, lcols] = out
                bits = pltpu.bitcast(out, jnp.int32)
                fold = bits if fold is None else fold | bits
        return fold & jnp.minimum(g, 0)

    def after(zero, rhs):
        head = pltpu.bitcast(rhs[:PEER_KTILE], jnp.int32)
        head = head | jnp.tile(zero, (head.shape[0] // SUBLANE, head.shape[1] // LANE))
        return jnp.concatenate([pltpu.bitcast(head, BF16), rhs[PEER_KTILE:]], axis=0)

    def matmul_pieces(n):
        ncols = slice(n * PEER_NCOLS, (n + 1) * PEER_NCOLS)
        pieces = []
        for m in range(d // PEER_MROWS):
            mrows = slice(m * PEER_MROWS, (m + 1) * PEER_MROWS)

            def second(zero, mrows=mrows):
                yt_ref[mrows, ncols] += jnp.dot(vt_ref[0, mrows, :], after(zero, at_refs[n][...]),
                                                preferred_element_type=F32)

            def first(zero, mrows=mrows):
                ht_refs[n][mrows, :] = jnp.dot(u_ref[mrows, :], after(zero, tt_ref[:, ncols]),
                                               preferred_element_type=F32)
            pieces += [second, first]
        return pieces

    pending = []
    for n in range(tt // PEER_NCOLS):
        blk = 0
        for jj in range(ngroup):
            for ag in range(ab // PEER_AROWS):
                for hb in range(2):
                    zero = gate_block(n, jj, ag, hb)
                    if pending and blk % PEER_GATE_PER_PIECE == 1:
                        pending.pop(0)(zero)
                    blk += 1
        for piece in pending:
            piece(zero)
        pending = matmul_pieces(n)
    for piece in pending:
        piece(zero)

    @pl.when(g == ne)
    def _():
        o = x_ref[0] + gt_ref[0] * yt_ref[...].T
        if final:
            o = o * lax.rsqrt(jnp.mean(o * o, axis=-1, keepdims=True) + RMS_EPS) * fg_ref[...]
        o_ref[0] = o


def _peer_call(x, sh, sc, gt, ng, fg, wq, k1, k2, u, v, *, tt, eb, final):
    b, s, d = x.shape
    assert eb == SUBLANE * PEER_NKEYS and u.shape[0] % eb == 0
    ne = u.shape[0] // eb
    ub = u.astype(BF16)
    vtb = v.astype(BF16).reshape(ne, eb, d).transpose(0, 2, 1)
    const3 = lambda bi, i, e: (0, 0)
    vec = pl.BlockSpec((1, 1, d), lambda bi, i, e: (bi, 0, 0))
    kern = functools.partial(_peer_kernel, tt=tt, eb=eb, ne=ne, final=final)
    sshape = (PEER_HEADS, PEER_NKEYS, tt)
    return pl.pallas_call(
        kern,
        grid=(b, s // tt, ne + 1),
        in_specs=[
            pl.BlockSpec((1, tt, d), lambda bi, i, e: (bi, i, 0)),
            vec, vec, vec,
            pl.BlockSpec((1, d), const3),
            pl.BlockSpec((1, d), const3),
            pl.BlockSpec(wq.shape, const3),
            pl.BlockSpec(k1.shape, const3),
            pl.BlockSpec(k2.shape, const3),
            pl.BlockSpec((eb, d), lambda bi, i, g: (jnp.minimum(g, ne - 1), 0)),
            pl.BlockSpec((1, d, eb), lambda bi, i, g: (jnp.maximum(g - 1, 0), 0, 0)),
        ],
        out_specs=pl.BlockSpec((1, tt, d), lambda bi, i, e: (bi, i, 0)),
        out_shape=jax.ShapeDtypeStruct(x.shape, F32),
        scratch_shapes=[
            pltpu.VMEM((d, tt), BF16),
            pltpu.VMEM(sshape, F32),
            pltpu.VMEM(sshape, F32),
            pltpu.VMEM(sshape, jnp.int32),
            pltpu.VMEM(sshape, jnp.int32),
            pltpu.VMEM(sshape, BF16),
            pltpu.VMEM(sshape, BF16),
            pltpu.VMEM((d, tt), F32),
        ] + [pltpu.VMEM((eb, PEER_NCOLS), F32)] * (tt // PEER_NCOLS)
          + [pltpu.VMEM((eb, PEER_NCOLS), BF16)] * (tt // PEER_NCOLS),
        compiler_params=pltpu.CompilerParams(
            dimension_semantics=("arbitrary", "arbitrary", "arbitrary"), vmem_limit_bytes=VMEM_LIMIT),
        name="peer_layer",
    )(x, sh, sc, gt, ng.reshape(1, d), fg.reshape(1, d), wq.astype(BF16), k1, k2, ub, vtb)


def _pick(n, pref):
    return pref if n % pref == 0 else n


def kernel(x, c, ada_w, ada_b, norm_mix_g, norm_ffn_g, gla_w_in, gla_w_g2, gla_b_g2, gla_norm_g,
           gla_w_out, gdn_w_in, gdn_conv_w, gdn_a_log, gdn_dt_bias, gdn_norm_g, gdn_w_out,
           peer_wq, peer_k1, peer_k2, peer_u, peer_v, final_norm_g):
    b, s, d = x.shape
    depth = ada_w.shape[0]
    c_pad = jnp.pad(c, ((0, SUBLANE - b % SUBLANE if b % SUBLANE else 0), (0, 0)))
    mod = _mod_call(c_pad, ada_w, ada_b)[:, :b, :].reshape(depth, b, 6, 1, d)
    tb_mix = _pick(s, 256)
    tt = _pick(s, 512)
    eb = 1024
    for i in range(depth):
        sh1, sc1, gt1, sh2, sc2, gt2 = (mod[i, :, j] for j in range(6))
        j = i // 2
        if i % 2 == 0:
            x = _gla_call(x, sh1, sc1, gt1, norm_mix_g[i], gla_w_in[j], gla_w_g2[j], gla_b_g2[j],
                          gla_norm_g[j], gla_w_out[j], tb=tb_mix)
        else:
            x = _gdn_call(x, sh1, sc1, gt1, norm_mix_g[i], gdn_w_in[j], gdn_conv_w[j], gdn_a_log[j],
                          gdn_dt_bias[j], gdn_norm_g[j], gdn_w_out[j], tb=tb_mix)
        x = _peer_call(x, sh2, sc2, gt2, norm_ffn_g[i], final_norm_g, peer_wq[i], peer_k1[i], peer_k2[i],
                       peer_u[i], peer_v[i], tt=tt, eb=eb, final=(i == depth - 1))
    return x
```

```python
import functools

import jax
import jax.numpy as jnp
from jax import lax
from jax.experimental import pallas as pl
from jax.experimental.pallas import tpu as pltpu

F32 = jnp.float32
BF16 = jnp.bfloat16
HI = lax.Precision.HIGHEST

RMS_EPS = 1e-6
CHUNK = 64

GLA_HEADS = 4
GLA_GATE_RANK = 16
GLA_TAU = 16.0

GDN_QK_HEADS = 8
GDN_V_HEADS = 16
GDN_HEAD_DIM = 128
GDN_CONV = 4

PEER_HEADS = 8
PEER_TOPK = 16
PEER_NKEYS = 128
PEER_HALF = 128
PEER_NCOLS = 256
PEER_KTILE = 256
PEER_MROWS = 1024
PEER_AROWS = 4
PEER_GATE_PER_PIECE = 4

LANE = 128
SUBLANE = 8
PACK = 16
VMEM_LIMIT = 56 * 1024 * 1024

NT = (((1,), (1,)), ((), ()))


def _sigmoid(x):
    return 1.0 / (1.0 + jnp.exp(-x))


def _silu(x):
    return x * _sigmoid(x)


def _softplus(x):
    return jnp.maximum(x, 0.0) + jnp.log1p(jnp.exp(-jnp.abs(x)))


def _gelu_tanh(x):
    c = 0.7978845608028654
    hx = 0.5 * x
    return hx + hx * jnp.tanh(x * (c + (c * 0.044715) * (x * x)))


def _norm_mod(x, g, sc, sh):
    ms = jnp.mean(x * x, axis=-1, keepdims=True)
    return (x * lax.rsqrt(ms + RMS_EPS) * g) * (1.0 + sc) + sh


def _bdot(a, b):
    return jnp.dot(a.astype(BF16), b.astype(BF16), preferred_element_type=F32)


def _bdot_nt(a, b):
    return lax.dot_general(a.astype(BF16), b.astype(BF16), NT, preferred_element_type=F32)


def _hdot(a, b):
    return jnp.dot(a, b, precision=HI, preferred_element_type=F32)


def _tril(n, k=0):
    r = lax.broadcasted_iota(jnp.int32, (n, n), 0)
    c = lax.broadcasted_iota(jnp.int32, (n, n), 1)
    return (c - r) <= k


def _mod_kernel(c_ref, w_ref, b_ref, o_ref):
    c = c_ref[...]
    o_ref[0] = _hdot(_silu(c), w_ref[0]) + b_ref[0]


def _mod_call(c_pad, ada_w, ada_b):
    depth, d, d6 = ada_w.shape
    nb = d6 // d
    return pl.pallas_call(
        _mod_kernel,
        grid=(depth, nb),
        in_specs=[
            pl.BlockSpec(c_pad.shape, lambda i, j: (0, 0)),
            pl.BlockSpec((1, d, d), lambda i, j: (i, 0, j)),
            pl.BlockSpec((1, 1, d), lambda i, j: (i, 0, j)),
        ],
        out_specs=pl.BlockSpec((1, c_pad.shape[0], d), lambda i, j: (i, 0, j)),
        out_shape=jax.ShapeDtypeStruct((depth, c_pad.shape[0], d6), F32),
        name="adaln_mod",
    )(c_pad, ada_w, ada_b.reshape(depth, 1, d6))


def _gla_kernel(x_ref, sh_ref, sc_ref, gt_ref, ng_ref, wqkvr_ref, wg1_ref, wg2_ref, bg2_ref,
                hg_ref, wout_ref, o_ref, st_ref, p_ref, la_ref, ob_ref, *, tb, dk, dv):
    hk = dk // GLA_HEADS
    hv = dv // GLA_HEADS
    nc = tb // CHUNK

    @pl.when(pl.program_id(1) == 0)
    def _():
        st_ref[...] = jnp.zeros_like(st_ref)

    x = x_ref[0]
    h = _norm_mod(x, ng_ref[...], sc_ref[0], sh_ref[0])
    hb = h.astype(BF16)
    p_ref[...] = jnp.dot(hb, wqkvr_ref[...], preferred_element_type=F32)
    g1 = jnp.dot(hb, wg1_ref[...], preferred_element_type=F32)
    g = jnp.dot(g1.astype(BF16), wg2_ref[...], preferred_element_type=F32) + bg2_ref[...]
    la_ref[...] = (jnp.minimum(g, 0.0) - jnp.log1p(jnp.exp(-jnp.abs(g)))) * (1.0 / GLA_TAU)

    tril = _tril(CHUNK).astype(F32)
    causal = _tril(CHUNK)
    scale = hk ** -0.5

    for c in range(nc):
        rows = slice(c * CHUNK, (c + 1) * CHUNK)
        cum = _hdot(tril, la_ref[rows, :])
        cl = cum[CHUNK - 1:CHUNK, :]
        e_pos = jnp.exp(cum)
        e_neg = jnp.exp(-cum)
        e_end = jnp.exp(cl - cum)
        dec = jnp.exp(cl)
        hr = range(GLA_HEADS)
        ksl = [slice(hd * hk, (hd + 1) * hk) for hd in hr]
        qis = [p_ref[rows, hd * hk:(hd + 1) * hk] * (scale * e_pos[:, ksl[hd]]) for hd in hr]
        kraw = [p_ref[rows, dk + hd * hk:dk + (hd + 1) * hk] for hd in hr]
        vs = [p_ref[rows, 2 * dk + hd * hv:2 * dk + (hd + 1) * hv] for hd in hr]
        attns = [jnp.where(causal, _bdot_nt(qis[hd], kraw[hd] * e_neg[:, ksl[hd]]), 0.0) for hd in hr]
        sts = [st_ref[hd] for hd in hr]
        inter = [_bdot_nt(qis[hd], sts[hd]) for hd in hr]
        upds = [_bdot(vs[hd].T, kraw[hd] * e_end[:, ksl[hd]]) for hd in hr]
        intra = [_bdot(attns[hd], vs[hd]) for hd in hr]
        for hd in hr:
            ob_ref[rows, hd * hv:(hd + 1) * hv] = intra[hd] + inter[hd]
            st_ref[hd] = dec[:, ksl[hd]] * sts[hd] + upds[hd]

    hg = hg_ref[...]
    parts = []
    for hd in range(GLA_HEADS):
        oh = ob_ref[:, hd * hv:(hd + 1) * hv]
        ms = jnp.mean(oh * oh, axis=-1, keepdims=True)
        r = p_ref[:, 2 * dk + dv + hd * hv:2 * dk + dv + (hd + 1) * hv]
        parts.append((oh * lax.rsqrt(ms + RMS_EPS) * hg * _silu(r)).astype(BF16))
    y = jnp.concatenate(parts, axis=-1)
    m = jnp.dot(y, wout_ref[...], preferred_element_type=F32)
    o_ref[0] = x + gt_ref[0] * m


def _gla_call(x, sh, sc, gt, ng, w_in, w_g2, b_g2, head_g, w_out, *, tb):
    b, s, d = x.shape
    dk = w_g2.shape[1]
    dv = w_out.shape[0]
    wqkvr = w_in[:, :2 * dk + 2 * dv].astype(BF16)
    wg1 = jnp.pad(w_in[:, 2 * dk + 2 * dv:], ((0, 0), (0, LANE - GLA_GATE_RANK))).astype(BF16)
    wg2 = jnp.pad(w_g2, ((0, LANE - GLA_GATE_RANK), (0, 0))).astype(BF16)
    hv = dv // GLA_HEADS
    hk = dk // GLA_HEADS
    const2 = lambda bi, i: (0, 0)
    vec = pl.BlockSpec((1, 1, d), lambda bi, i: (bi, 0, 0))
    kern = functools.partial(_gla_kernel, tb=tb, dk=dk, dv=dv)
    return pl.pallas_call(
        kern,
        grid=(b, s // tb),
        in_specs=[
            pl.BlockSpec((1, tb, d), lambda bi, i: (bi, i, 0)),
            vec, vec, vec,
            pl.BlockSpec((1, d), const2),
            pl.BlockSpec(wqkvr.shape, const2),
            pl.BlockSpec(wg1.shape, const2),
            pl.BlockSpec(wg2.shape, const2),
            pl.BlockSpec((1, dk), const2),
            pl.BlockSpec((1, hv), const2),
            pl.BlockSpec((dv, d), const2),
        ],
        out_specs=pl.BlockSpec((1, tb, d), lambda bi, i: (bi, i, 0)),
        out_shape=jax.ShapeDtypeStruct(x.shape, F32),
        scratch_shapes=[
            pltpu.VMEM((GLA_HEADS, hv, hk), F32),
            pltpu.VMEM((tb, 2 * dk + 2 * dv), F32),
            pltpu.VMEM((tb, dk), F32),
            pltpu.VMEM((tb, dv), F32),
        ],
        compiler_params=pltpu.CompilerParams(
            dimension_semantics=("arbitrary", "arbitrary"), vmem_limit_bytes=VMEM_LIMIT),
        name="gla_layer",
    )(x, sh, sc, gt, ng.reshape(1, d), wqkvr, wg1, wg2, b_g2.reshape(1, dk),
      head_g.reshape(1, hv), w_out.astype(BF16))


def _block_inverse_masks(n):
    r = lax.broadcasted_iota(jnp.int32, (n, n), 0)
    c = lax.broadcasted_iota(jnp.int32, (n, n), 1)
    eye = (r == c).astype(F32)
    first = (r // 2) == (c // 2)
    quads = []
    s = 2
    while s < n:
        quads.append(((r // (2 * s)) == (c // (2 * s))) & ((r // s) != (c // s)))
        s *= 2
    return eye, first, quads


def _unit_lower_inverse(a_list, masks):
    eye, first, quads = masks
    ds = [eye - jnp.where(first, a, 0.0) for a in a_list]
    for quad in quads:
        ms = [_bdot(d, jnp.where(quad, a, 0.0)) for d, a in zip(ds, a_list)]
        ds = [d - _bdot(m, d) for d, m in zip(ds, ms)]
    return ds


def _gdn_kernel(x_ref, sh_ref, sc_ref, gt_ref, ng_ref, wqkv_ref, wz_ref, wa_ref, wb_ref, cw_ref,
                alog_ref, dtb_ref, hg_ref, wout_ref, o_ref,
                s_ref, cb_ref, q_ref, k_ref, v_ref, z_ref, g_ref, bt_ref, ob_ref, *, tb):
    dqk = GDN_QK_HEADS * GDN_HEAD_DIM
    dvv = GDN_V_HEADS * GDN_HEAD_DIM
    hd_ = GDN_HEAD_DIM
    nc = tb // CHUNK
    rep = GDN_V_HEADS // GDN_QK_HEADS

    @pl.when(pl.program_id(1) == 0)
    def _():
        s_ref[...] = jnp.zeros_like(s_ref)
        cb_ref[0:SUBLANE, :] = jnp.zeros((SUBLANE, cb_ref.shape[1]), F32)

    x = x_ref[0]
    h = _norm_mod(x, ng_ref[...], sc_ref[0], sh_ref[0])
    hb = h.astype(BF16)
    cb_ref[SUBLANE:SUBLANE + tb, :] = jnp.dot(hb, wqkv_ref[...], preferred_element_type=F32)
    z_ref[...] = jnp.dot(hb, wz_ref[...], preferred_element_type=F32)
    a = jnp.dot(hb, wa_ref[...], preferred_element_type=F32)
    bb = jnp.dot(hb, wb_ref[...], preferred_element_type=F32)
    g_ref[...] = -jnp.exp(alog_ref[...]) * _softplus(a + dtb_ref[...])
    bt_ref[...] = _sigmoid(bb)

    off = SUBLANE - (GDN_CONV - 1)
    conv = cw_ref[0:1, :] * cb_ref[off:off + tb, :]
    for j in range(1, GDN_CONV):
        conv = conv + cw_ref[j:j + 1, :] * cb_ref[off + j:off + j + tb, :]
    cb_ref[0:SUBLANE, :] = cb_ref[tb:tb + SUBLANE, :]
    qkv = _silu(conv)
    qscale = hd_ ** -0.5
    for hq in range(GDN_QK_HEADS):
        sl = slice(hq * hd_, (hq + 1) * hd_)
        qh = qkv[:, hq * hd_:(hq + 1) * hd_]
        kh = qkv[:, dqk + hq * hd_:dqk + (hq + 1) * hd_]
        q_ref[:, sl] = qh * (lax.rsqrt(jnp.sum(qh * qh, axis=-1, keepdims=True) + 1e-6) * qscale)
        k_ref[:, sl] = kh * lax.rsqrt(jnp.sum(kh * kh, axis=-1, keepdims=True) + 1e-6)
    v_ref[...] = qkv[:, 2 * dqk:]

    tril = _tril(CHUNK).astype(F32)
    lower = _tril(CHUNK)
    strict = _tril(CHUNK, -1)
    inv_masks = _block_inverse_masks(CHUNK)

    for c in range(nc):
        rows = slice(c * CHUNK, (c + 1) * CHUNK)
        gc = _hdot(tril, g_ref[rows, :])
        gct = gc.T
        gl = gc[CHUNK - 1:CHUNK, :]
        eg = jnp.exp(gc)
        eendt = jnp.exp(gl - gc).T
        dec = jnp.exp(gl)
        bt = bt_ref[rows, :]
        beg = bt * eg
        hq_r = range(GDN_QK_HEADS)
        hv_r = range(GDN_V_HEADS)
        qs = [q_ref[rows, hq * hd_:(hq + 1) * hd_] for hq in hq_r]
        ks = [k_ref[rows, hq * hd_:(hq + 1) * hd_] for hq in hq_r]
        kks = [_bdot_nt(ks[hq], ks[hq]) for hq in hq_r]
        qk0s = [_bdot_nt(qs[hq], ks[hq]) for hq in hq_r]
        khts = [ks[hq].T for hq in hq_r]
        cols = [slice(hvi, hvi + 1) for hvi in hv_r]
        decays = []
        for hvi in hv_r:
            diff = gc[:, cols[hvi]] - gct[hvi:hvi + 1, :]
            decays.append(jnp.where(lower, jnp.exp(jnp.where(lower, diff, 0.0)), 0.0))
        amats = [jnp.where(strict, bt[:, cols[hvi]] * kks[hvi // rep] * decays[hvi], 0.0) for hvi in hv_r]
        tinvs = _unit_lower_inverse(amats, inv_masks)
        rhs = [jnp.concatenate([bt[:, cols[hvi]] * v_ref[rows, hvi * hd_:(hvi + 1) * hd_],
                                beg[:, cols[hvi]] * ks[hvi // rep]], axis=1) for hvi in hv_r]
        uws = [_bdot(tinvs[hvi], rhs[hvi]) for hvi in hv_r]
        ss = [s_ref[hvi] for hvi in hv_r]
        wss = [_bdot(uws[hvi][:, hd_:], ss[hvi]) for hvi in hv_r]
        qss = [_bdot(qs[hvi // rep] * eg[:, cols[hvi]], ss[hvi]) for hvi in hv_r]
        vns = [uws[hvi][:, :hd_] - wss[hvi] for hvi in hv_r]
        outs = [qss[hvi] + _bdot(qk0s[hvi // rep] * decays[hvi], vns[hvi]) for hvi in hv_r]
        upds = [_bdot(khts[hvi // rep] * eendt[hvi:hvi + 1, :], vns[hvi]) for hvi in hv_r]
        for hvi in hv_r:
            ob_ref[rows, hvi * hd_:(hvi + 1) * hd_] = outs[hvi]
            s_ref[hvi] = dec[:, cols[hvi]] * ss[hvi] + upds[hvi]

    hg = hg_ref[...]
    parts = []
    for hvi in range(GDN_V_HEADS):
        vs = slice(hvi * hd_, (hvi + 1) * hd_)
        oh = ob_ref[:, vs]
        ms = jnp.mean(oh * oh, axis=-1, keepdims=True)
        parts.append((oh * lax.rsqrt(ms + RMS_EPS) * hg * _silu(z_ref[:, vs])).astype(BF16))
    y = jnp.concatenate(parts, axis=-1)
    m = jnp.dot(y, wout_ref[...], preferred_element_type=F32)
    o_ref[0] = x + gt_ref[0] * m


def _gdn_call(x, sh, sc, gt, ng, w_in, conv_w, a_log, dt_bias, head_g, w_out, *, tb):
    b, s, d = x.shape
    dqk = GDN_QK_HEADS * GDN_HEAD_DIM
    dvv = GDN_V_HEADS * GDN_HEAD_DIM
    nqkv = 2 * dqk + dvv
    wqkv = w_in[:, :nqkv].astype(BF16)
    wz = w_in[:, nqkv:nqkv + dvv].astype(BF16)
    padh = ((0, 0), (0, LANE - GDN_V_HEADS))
    wa = jnp.pad(w_in[:, nqkv + dvv:nqkv + dvv + GDN_V_HEADS], padh).astype(BF16)
    wb = jnp.pad(w_in[:, nqkv + dvv + GDN_V_HEADS:], padh).astype(BF16)
    alog = jnp.pad(a_log.reshape(1, -1), padh)
    dtb = jnp.pad(dt_bias.reshape(1, -1), padh)
    const2 = lambda bi, i: (0, 0)
    vec = pl.BlockSpec((1, 1, d), lambda bi, i: (bi, 0, 0))
    kern = functools.partial(_gdn_kernel, tb=tb)
    return pl.pallas_call(
        kern,
        grid=(b, s // tb),
        in_specs=[
            pl.BlockSpec((1, tb, d), lambda bi, i: (bi, i, 0)),
            vec, vec, vec,
            pl.BlockSpec((1, d), const2),
            pl.BlockSpec(wqkv.shape, const2),
            pl.BlockSpec(wz.shape, const2),
            pl.BlockSpec(wa.shape, const2),
            pl.BlockSpec(wb.shape, const2),
            pl.BlockSpec(conv_w.shape, const2),
            pl.BlockSpec((1, LANE), const2),
            pl.BlockSpec((1, LANE), const2),
            pl.BlockSpec((1, GDN_HEAD_DIM), const2),
            pl.BlockSpec((dvv, d), const2),
        ],
        out_specs=pl.BlockSpec((1, tb, d), lambda bi, i: (bi, i, 0)),
        out_shape=jax.ShapeDtypeStruct(x.shape, F32),
        scratch_shapes=[
            pltpu.VMEM((GDN_V_HEADS, GDN_HEAD_DIM, GDN_HEAD_DIM), F32),
            pltpu.VMEM((tb + SUBLANE, nqkv), F32),
            pltpu.VMEM((tb, dqk), F32),
            pltpu.VMEM((tb, dqk), F32),
            pltpu.VMEM((tb, dvv), F32),
            pltpu.VMEM((tb, dvv), F32),
            pltpu.VMEM((tb, LANE), F32),
            pltpu.VMEM((tb, LANE), F32),
            pltpu.VMEM((tb, dvv), F32),
        ],
        compiler_params=pltpu.CompilerParams(
            dimension_semantics=("arbitrary", "arbitrary"), vmem_limit_bytes=VMEM_LIMIT),
        name="gdn_layer",
    )(x, sh, sc, gt, ng.reshape(1, d), wqkv, wz, wa, wb, conv_w, alog, dtb,
      head_g.reshape(1, GDN_HEAD_DIM), w_out.astype(BF16))


def _top_values(x, n, with_rank=False):
    vals = []
    rank = jnp.full(x.shape, float(n), F32) if with_rank else None
    for r in range(n):
        m = jnp.max(x, axis=0, keepdims=True)
        vals.append(m)
        if with_rank:
            rank = jnp.where(x == m, float(r), rank)
        if r + 1 < n:
            x = jnp.where(x == m, -jnp.inf, x)
    return (vals, rank) if with_rank else vals


def _pair_pack(x):
    bits = lax.bitcast_convert_type(x.astype(BF16).astype(F32), jnp.int32)
    return bits | lax.shift_right_logical(bits, jnp.int32(16))


def _stack_rows(rows):
    l = rows[0].shape[1]
    sub = lax.broadcasted_iota(jnp.int32, (SUBLANE, l), 0)
    out = jnp.broadcast_to(rows[0], (SUBLANE, l))
    for i in range(1, len(rows)):
        out = jnp.where(sub == i, rows[i], out)
    return out


def _peer_kernel(x_ref, sh_ref, sc_ref, gt_ref, ng_ref, fg_ref, wq_ref, k1_ref, k2_ref, u_ref, vt_ref, o_ref,
                 tt_ref, s1_ref, s2_ref, r1_ref, w1_ref, jb_ref, e2_ref, yt_ref, *half_refs, tt, eb, ne, final):
    g = pl.program_id(2)
    nh = tt // PEER_NCOLS
    ht_refs = half_refs[:nh]
    at_refs = half_refs[nh:]
    nj = tt // LANE
    ab = eb // PEER_NKEYS
    k = PEER_TOPK

    @pl.when(g == 0)
    def _():
        x = x_ref[0]
        t = _norm_mod(x, ng_ref[...], sc_ref[0], sh_ref[0])
        tb = t.astype(BF16)
        tt_ref[...] = t.T.astype(BF16)
        q = jnp.dot(tb, wq_ref[...], preferred_element_type=F32)
        for h in range(PEER_HEADS):
            q1 = q[:, (2 * h) * PEER_HALF:(2 * h + 1) * PEER_HALF]
            q2 = q[:, (2 * h + 1) * PEER_HALF:(2 * h + 2) * PEER_HALF]
            s1_ref[h] = lax.dot_general(k1_ref[...], q1, NT, precision=HI, preferred_element_type=F32)
            s2_ref[h] = lax.dot_general(k2_ref[...], q2, NT, precision=HI, preferred_element_type=F32)
        yt_ref[...] = jnp.zeros_like(yt_ref)
        for ht_ref in ht_refs:
            ht_ref[...] = jnp.zeros_like(ht_ref)

        def route_one(idx):
            h = idx // nj
            c0 = pl.multiple_of((idx % nj) * LANE, LANE)
            cols = pl.ds(c0, LANE)
            x1 = s1_ref[h, :, cols]
            x2 = s2_ref[h, :, cols]
            v1, rank1 = _top_values(x1, k + 1, with_rank=True)
            v2 = _top_values(x2, k + 1)
            v2lo = _stack_rows(v2[:SUBLANE])
            v2hi = _stack_rows(v2[SUBLANE:2 * SUBLANE])
            v1hi = _stack_rows(v1[SUBLANE:2 * SUBLANE])
            sub = lax.broadcasted_iota(jnp.int32, (SUBLANE, LANE), 0)
            last = jnp.where(sub == 0, v1[0] + v2[k], jnp.where(sub == 1, v1[k] + v2[0], -jnp.inf))
            tiles = [v1[0] + v2lo, v1[0] + v2hi, v1hi + v2[0], last]
            tiles += [v1[i] + v2lo for i in range(1, SUBLANE)]
            cand = jnp.concatenate(tiles, axis=0)
            tops = _top_values(cand, k + 1)
            kth = tops[k - 1]
            cut = 0.5 * (kth + tops[k])
            top = v1[0] + v2[0]
            z = jnp.sum(jnp.where(cand >= kth, jnp.exp(cand - top), 0.0), axis=0, keepdims=True)
            c2 = cut - x2
            passing = jnp.zeros_like(c2)
            for i in range(k + 1):
                passing = passing + jnp.where(v1[i] >= c2, 1.0, 0.0)
            r1_ref[h, :, cols] = _pair_pack(rank1)
            w1_ref[h, :, cols] = _pair_pack(jnp.exp(x1 - v1[0]) / z)
            jb_ref[h, :, cols] = passing.astype(BF16)
            e2_ref[h, :, cols] = jnp.exp(x2 - v2[0]).astype(BF16)

        def route(i, carry):
            route_one(2 * i)
            route_one(2 * i + 1)
            return carry

        lax.fori_loop(0, PEER_HEADS * nj // 2, route, 0)

    em = jnp.clip(g - 1, 0, ne - 1)
    arows = pl.ds(pl.multiple_of(em * ab, SUBLANE), SUBLANE)
    bh = PEER_NKEYS // 2
    d = yt_ref.shape[0]
    ngroup = PEER_NCOLS // LANE

    def gate_block(n, jj, ag, hb):
        cols = slice(n * PEER_NCOLS + jj * LANE, n * PEER_NCOLS + (jj + 1) * LANE)
        lcols = slice(jj * LANE, (jj + 1) * LANE)
        brows = slice(hb * bh, (hb + 1) * bh)
        als = tuple(range(ag * PEER_AROWS, (ag + 1) * PEER_AROWS))
        r1t = [r1_ref[h, arows, cols] for h in range(PEER_HEADS)]
        w1t = [w1_ref[h, arows, cols] for h in range(PEER_HEADS)]

        def row(tile, al):
            return pltpu.bitcast(jnp.broadcast_to(tile[al:al + 1, :], (SUBLANE, LANE)), BF16)

        nv = bh // PACK
        accs = [[None] * nv for _ in als]
        for h in range(PEER_HEADS):
            jb = jb_ref[h, brows, cols]
            e2 = e2_ref[h, brows, cols]
            r1rows = [row(r1t[h], al) for al in als]
            w1rows = [row(w1t[h], al) for al in als]
            for v in range(nv):
                vr = slice(v * PACK, (v + 1) * PACK)
                jbv = jb[vr]
                e2v = e2[vr]
                for i in range(len(als)):
                    term = jnp.where(r1rows[i] < jbv, e2v * w1rows[i], jnp.zeros_like(e2v))
                    accs[i][v] = term if h == 0 else accs[i][v] + term
        fold = None
        for i, al in enumerate(als):
            for v in range(nv):
                r0 = al * PEER_NKEYS + hb * bh + v * PACK
                out = accs[i][v] * _gelu_tanh(ht_refs[n][r0:r0 + PACK, lcols].astype(BF16))
                at_refs[n][r0:r0 + PACK, lcols] = out
                bits = pltpu.bitcast(out, jnp.int32)
                fold = bits if fold is None else fold | bits
        return fold & jnp.minimum(g, 0)

    def after(zero, rhs):
        head = pltpu.bitcast(rhs[:PEER_KTILE], jnp.int32)
        head = head | jnp.tile(zero, (head.shape[0] // SUBLANE, head.shape[1] // LANE))
        return jnp.concatenate([pltpu.bitcast(head, BF16), rhs[PEER_KTILE:]], axis=0)

    def matmul_pieces(n):
        ncols = slice(n * PEER_NCOLS, (n + 1) * PEER_NCOLS)
        pieces = []
        for m in range(d // PEER_MROWS):
            mrows = slice(m * PEER_MROWS, (m + 1) * PEER_MROWS)

            def second(zero, mrows=mrows):
                yt_ref[mrows, ncols] += jnp.dot(vt_ref[0, mrows, :], after(zero, at_refs[n][...]),
                                                preferred_element_type=F32)

            def first(zero, mrows=mrows):
                ht_refs[n][mrows, :] = jnp.dot(u_ref[mrows, :], after(zero, tt_ref[:, ncols]),
                                               preferred_element_type=F32)
            pieces += [second, first]
        return pieces

    pending = []
    for n in range(tt // PEER_NCOLS):
        blk = 0
        for jj in range(ngroup):
            for ag in range(ab // PEER_AROWS):
                for hb in range(2):
                    zero = gate_block(n, jj, ag, hb)
                    if pending and blk % PEER_GATE_PER_PIECE == 1:
                        pending.pop(0)(zero)
                    blk += 1
        for piece in pending:
            piece(zero)
        pending = matmul_pieces(n)
    for piece in pending:
        piece(zero)

    @pl.when(g == ne)
    def _():
        o = x_ref[0] + gt_ref[0] * yt_ref[...].T
        if final:
            o = o * lax.rsqrt(jnp.mean(o * o, axis=-1, keepdims=True) + RMS_EPS) * fg_ref[...]
        o_ref[0] = o


def _peer_call(x, sh, sc, gt, ng, fg, wq, k1, k2, u, v, *, tt, eb, final):
    b, s, d = x.shape
    assert eb == SUBLANE * PEER_NKEYS and u.shape[0] % eb == 0
    ne = u.shape[0] // eb
    ub = u.astype(BF16)
    vtb = v.astype(BF16).reshape(ne, eb, d).transpose(0, 2, 1)
    const3 = lambda bi, i, e: (0, 0)
    vec = pl.BlockSpec((1, 1, d), lambda bi, i, e: (bi, 0, 0))
    kern = functools.partial(_peer_kernel, tt=tt, eb=eb, ne=ne, final=final)
    sshape = (PEER_HEADS, PEER_NKEYS, tt)
    return pl.pallas_call(
        kern,
        grid=(b, s // tt, ne + 1),
        in_specs=[
            pl.BlockSpec((1, tt, d), lambda bi, i, e: (bi, i, 0)),
            vec, vec, vec,
            pl.BlockSpec((1, d), const3),
            pl.BlockSpec((1, d), const3),
            pl.BlockSpec(wq.shape, const3),
            pl.BlockSpec(k1.shape, const3),
            pl.BlockSpec(k2.shape, const3),
            pl.BlockSpec((eb, d), lambda bi, i, g: (jnp.minimum(g, ne - 1), 0)),
            pl.BlockSpec((1, d, eb), lambda bi, i, g: (jnp.maximum(g - 1, 0), 0, 0)),
        ],
        out_specs=pl.BlockSpec((1, tt, d), lambda bi, i, e: (bi, i, 0)),
        out_shape=jax.ShapeDtypeStruct(x.shape, F32),
        scratch_shapes=[
            pltpu.VMEM((d, tt), BF16),
            pltpu.VMEM(sshape, F32),
            pltpu.VMEM(sshape, F32),
            pltpu.VMEM(sshape, jnp.int32),
            pltpu.VMEM(sshape, jnp.int32),
            pltpu.VMEM(sshape, BF16),
            pltpu.VMEM(sshape, BF16),
            pltpu.VMEM((d, tt), F32),
        ] + [pltpu.VMEM((eb, PEER_NCOLS), F32)] * (tt // PEER_NCOLS)
          + [pltpu.VMEM((eb, PEER_NCOLS), BF16)] * (tt // PEER_NCOLS),
        compiler_params=pltpu.CompilerParams(
            dimension_semantics=("arbitrary", "arbitrary", "arbitrary"), vmem_limit_bytes=VMEM_LIMIT),
        name="peer_layer",
    )(x, sh, sc, gt, ng.reshape(1, d), fg.reshape(1, d), wq.astype(BF16), k1, k2, ub, vtb)


def _pick(n, pref):
    return pref if n % pref == 0 else n


def kernel(x, c, ada_w, ada_b, norm_mix_g, norm_ffn_g, gla_w_in, gla_w_g2, gla_b_g2, gla_norm_g,
           gla_w_out, gdn_w_in, gdn_conv_w, gdn_a_log, gdn_dt_bias, gdn_norm_g, gdn_w_out,
           peer_wq, peer_k1, peer_k2, peer_u, peer_v, final_norm_g):
    b, s, d = x.shape
    depth = ada_w.shape[0]
    c_pad = jnp.pad(c, ((0, SUBLANE - b % SUBLANE if b % SUBLANE else 0), (0, 0)))
    mod = _mod_call(c_pad, ada_w, ada_b)[:, :b, :].reshape(depth, b, 6, 1, d)
    tb_mix = _pick(s, 256)
    tt = _pick(s, 512)
    eb = 1024
    for i in range(depth):
        sh1, sc1, gt1, sh2, sc2, gt2 = (mod[i, :, j] for j in range(6))
        j = i // 2
        if i % 2 == 0:
            x = _gla_call(x, sh1, sc1, gt1, norm_mix_g[i], gla_w_in[j], gla_w_g2[j], gla_b_g2[j],
                          gla_norm_g[j], gla_w_out[j], tb=tb_mix)
        else:
            x = _gdn_call(x, sh1, sc1, gt1, norm_mix_g[i], gdn_w_in[j], gdn_conv_w[j], gdn_a_log[j],
                          gdn_dt_bias[j], gdn_norm_g[j], gdn_w_out[j], tb=tb_mix)
        x = _peer_call(x, sh2, sc2, gt2, norm_ffn_g[i], final_norm_g, peer_wq[i], peer_k1[i], peer_k2[i],
                       peer_u[i], peer_v[i], tt=tt, eb=eb, final=(i == depth - 1))
    return x
```

```python
import functools

import jax
import jax.numpy as jnp
from jax import lax
from jax.experimental import pallas as pl
from jax.experimental.pallas import tpu as pltpu

F32 = jnp.float32
BF16 = jnp.bfloat16
HI = lax.Precision.HIGHEST

RMS_EPS = 1e-6
CHUNK = 64

GLA_HEADS = 4
GLA_GATE_RANK = 16
GLA_TAU = 16.0

GDN_QK_HEADS = 8
GDN_V_HEADS = 16
GDN_HEAD_DIM = 128
GDN_CONV = 4

PEER_HEADS = 8
PEER_TOPK = 16
PEER_NKEYS = 128
PEER_HALF = 128
PEER_NCOLS = 256
PEER_KTILE = 256
PEER_MROWS = 1024
PEER_AROWS = 4
PEER_GATE_PER_PIECE = 4

LANE = 128
SUBLANE = 8
PACK = 16
VMEM_LIMIT = 56 * 1024 * 1024

NT = (((1,), (1,)), ((), ()))


def _sigmoid(x):
    return 1.0 / (1.0 + jnp.exp(-x))


def _silu(x):
    return x * _sigmoid(x)


def _softplus(x):
    return jnp.maximum(x, 0.0) + jnp.log1p(jnp.exp(-jnp.abs(x)))


def _gelu_tanh(x):
    c = 0.7978845608028654
    hx = 0.5 * x
    return hx + hx * jnp.tanh(x * (c + (c * 0.044715) * (x * x)))


def _norm_mod(x, g, sc, sh):
    ms = jnp.mean(x * x, axis=-1, keepdims=True)
    return (x * lax.rsqrt(ms + RMS_EPS) * g) * (1.0 + sc) + sh


def _bdot(a, b):
    return jnp.dot(a.astype(BF16), b.astype(BF16), preferred_element_type=F32)


def _bdot_nt(a, b):
    return lax.dot_general(a.astype(BF16), b.astype(BF16), NT, preferred_element_type=F32)


def _hdot(a, b):
    return jnp.dot(a, b, precision=HI, preferred_element_type=F32)


def _tril(n, k=0):
    r = lax.broadcasted_iota(jnp.int32, (n, n), 0)
    c = lax.broadcasted_iota(jnp.int32, (n, n), 1)
    return (c - r) <= k


def _mod_kernel(c_ref, w_ref, b_ref, o_ref):
    c = c_ref[...]
    o_ref[0] = _hdot(_silu(c), w_ref[0]) + b_ref[0]


def _mod_call(c_pad, ada_w, ada_b):
    depth, d, d6 = ada_w.shape
    nb = d6 // d
    return pl.pallas_call(
        _mod_kernel,
        grid=(depth, nb),
        in_specs=[
            pl.BlockSpec(c_pad.shape, lambda i, j: (0, 0)),
            pl.BlockSpec((1, d, d), lambda i, j: (i, 0, j)),
            pl.BlockSpec((1, 1, d), lambda i, j: (i, 0, j)),
        ],
        out_specs=pl.BlockSpec((1, c_pad.shape[0], d), lambda i, j: (i, 0, j)),
        out_shape=jax.ShapeDtypeStruct((depth, c_pad.shape[0], d6), F32),
        name="adaln_mod",
    )(c_pad, ada_w, ada_b.reshape(depth, 1, d6))


def _gla_kernel(x_ref, sh_ref, sc_ref, gt_ref, ng_ref, wqkvr_ref, wg1_ref, wg2_ref, bg2_ref,
                hg_ref, wout_ref, o_ref, st_ref, p_ref, la_ref, ob_ref, *, tb, dk, dv):
    hk = dk // GLA_HEADS
    hv = dv // GLA_HEADS
    nc = tb // CHUNK

    @pl.when(pl.program_id(1) == 0)
    def _():
        st_ref[...] = jnp.zeros_like(st_ref)

    x = x_ref[0]
    h = _norm_mod(x, ng_ref[...], sc_ref[0], sh_ref[0])
    hb = h.astype(BF16)
    p_ref[...] = jnp.dot(hb, wqkvr_ref[...], preferred_element_type=F32)
    g1 = jnp.dot(hb, wg1_ref[...], preferred_element_type=F32)
    g = jnp.dot(g1.astype(BF16), wg2_ref[...], preferred_element_type=F32) + bg2_ref[...]
    la_ref[...] = (jnp.minimum(g, 0.0) - jnp.log1p(jnp.exp(-jnp.abs(g)))) * (1.0 / GLA_TAU)

    tril = _tril(CHUNK).astype(F32)
    causal = _tril(CHUNK)
    scale = hk ** -0.5

    for c in range(nc):
        rows = slice(c * CHUNK, (c + 1) * CHUNK)
        cum = _hdot(tril, la_ref[rows, :])
        cl = cum[CHUNK - 1:CHUNK, :]
        e_pos = jnp.exp(cum)
        e_neg = jnp.exp(-cum)
        e_end = jnp.exp(cl - cum)
        dec = jnp.exp(cl)
        hr = range(GLA_HEADS)
        ksl = [slice(hd * hk, (hd + 1) * hk) for hd in hr]
        qis = [p_ref[rows, hd * hk:(hd + 1) * hk] * (scale * e_pos[:, ksl[hd]]) for hd in hr]
        kraw = [p_ref[rows, dk + hd * hk:dk + (hd + 1) * hk] for hd in hr]
        vs = [p_ref[rows, 2 * dk + hd * hv:2 * dk + (hd + 1) * hv] for hd in hr]
        attns = [jnp.where(causal, _bdot_nt(qis[hd], kraw[hd] * e_neg[:, ksl[hd]]), 0.0) for hd in hr]
        sts = [st_ref[hd] for hd in hr]
        inter = [_bdot_nt(qis[hd], sts[hd]) for hd in hr]
        upds = [_bdot(vs[hd].T, kraw[hd] * e_end[:, ksl[hd]]) for hd in hr]
        intra = [_bdot(attns[hd], vs[hd]) for hd in hr]
        for hd in hr:
            ob_ref[rows, hd * hv:(hd + 1) * hv] = intra[hd] + inter[hd]
            st_ref[hd] = dec[:, ksl[hd]] * sts[hd] + upds[hd]

    hg = hg_ref[...]
    parts = []
    for hd in range(GLA_HEADS):
        oh = ob_ref[:, hd * hv:(hd + 1) * hv]
        ms = jnp.mean(oh * oh, axis=-1, keepdims=True)
        r = p_ref[:, 2 * dk + dv + hd * hv:2 * dk + dv + (hd + 1) * hv]
        parts.append((oh * lax.rsqrt(ms + RMS_EPS) * hg * _silu(r)).astype(BF16))
    y = jnp.concatenate(parts, axis=-1)
    m = jnp.dot(y, wout_ref[...], preferred_element_type=F32)
    o_ref[0] = x + gt_ref[0] * m


def _gla_call(x, sh, sc, gt, ng, w_in, w_g2, b_g2, head_g, w_out, *, tb):
    b, s, d = x.shape
    dk = w_g2.shape[1]
    dv = w_out.shape[0]
    wqkvr = w_in[:, :2 * dk + 2 * dv].astype(BF16)
    wg1 = jnp.pad(w_in[:, 2 * dk + 2 * dv:], ((0, 0), (0, LANE - GLA_GATE_RANK))).astype(BF16)
    wg2 = jnp.pad(w_g2, ((0, LANE - GLA_GATE_RANK), (0, 0))).astype(BF16)
    hv = dv // GLA_HEADS
    hk = dk // GLA_HEADS
    const2 = lambda bi, i: (0, 0)
    vec = pl.BlockSpec((1, 1, d), lambda bi, i: (bi, 0, 0))
    kern = functools.partial(_gla_kernel, tb=tb, dk=dk, dv=dv)
    return pl.pallas_call(
        kern,
        grid=(b, s // tb),
        in_specs=[
            pl.BlockSpec((1, tb, d), lambda bi, i: (bi, i, 0)),
            vec, vec, vec,
            pl.BlockSpec((1, d), const2),
            pl.BlockSpec(wqkvr.shape, const2),
            pl.BlockSpec(wg1.shape, const2),
            pl.BlockSpec(wg2.shape, const2),
            pl.BlockSpec((1, dk), const2),
            pl.BlockSpec((1, hv), const2),
            pl.BlockSpec((dv, d), const2),
        ],
        out_specs=pl.BlockSpec((1, tb, d), lambda bi, i: (bi, i, 0)),
        out_shape=jax.ShapeDtypeStruct(x.shape, F32),
        scratch_shapes=[
            pltpu.VMEM((GLA_HEADS, hv, hk), F32),
            pltpu.VMEM((tb, 2 * dk + 2 * dv), F32),
            pltpu.VMEM((tb, dk), F32),
            pltpu.VMEM((tb, dv), F32),
        ],
        compiler_params=pltpu.CompilerParams(
            dimension_semantics=("arbitrary", "arbitrary"), vmem_limit_bytes=VMEM_LIMIT),
        name="gla_layer",
    )(x, sh, sc, gt, ng.reshape(1, d), wqkvr, wg1, wg2, b_g2.reshape(1, dk),
      head_g.reshape(1, hv), w_out.astype(BF16))


def _block_inverse_masks(n):
    r = lax.broadcasted_iota(jnp.int32, (n, n), 0)
    c = lax.broadcasted_iota(jnp.int32, (n, n), 1)
    eye = (r == c).astype(F32)
    first = (r // 2) == (c // 2)
    quads = []
    s = 2
    while s < n:
        quads.append(((r // (2 * s)) == (c // (2 * s))) & ((r // s) != (c // s)))
        s *= 2
    return eye, first, quads


def _unit_lower_inverse(a_list, masks):
    eye, first, quads = masks
    ds = [eye - jnp.where(first, a, 0.0) for a in a_list]
    for quad in quads:
        ms = [_bdot(d, jnp.where(quad, a, 0.0)) for d, a in zip(ds, a_list)]
        ds = [d - _bdot(m, d) for d, m in zip(ds, ms)]
    return ds


def _gdn_kernel(x_ref, sh_ref, sc_ref, gt_ref, ng_ref, wqkv_ref, wz_ref, wa_ref, wb_ref, cw_ref,
                alog_ref, dtb_ref, hg_ref, wout_ref, o_ref,
                s_ref, cb_ref, q_ref, k_ref, v_ref, z_ref, g_ref, bt_ref, ob_ref, *, tb):
    dqk = GDN_QK_HEADS * GDN_HEAD_DIM
    dvv = GDN_V_HEADS * GDN_HEAD_DIM
    hd_ = GDN_HEAD_DIM
    nc = tb // CHUNK
    rep = GDN_V_HEADS // GDN_QK_HEADS

    @pl.when(pl.program_id(1) == 0)
    def _():
        s_ref[...] = jnp.zeros_like(s_ref)
        cb_ref[0:SUBLANE, :] = jnp.zeros((SUBLANE, cb_ref.shape[1]), F32)

    x = x_ref[0]
    h = _norm_mod(x, ng_ref[...], sc_ref[0], sh_ref[0])
    hb = h.astype(BF16)
    cb_ref[SUBLANE:SUBLANE + tb, :] = jnp.dot(hb, wqkv_ref[...], preferred_element_type=F32)
    z_ref[...] = jnp.dot(hb, wz_ref[...], preferred_element_type=F32)
    a = jnp.dot(hb, wa_ref[...], preferred_element_type=F32)
    bb = jnp.dot(hb, wb_ref[...], preferred_element_type=F32)
    g_ref[...] = -jnp.exp(alog_ref[...]) * _softplus(a + dtb_ref[...])
    bt_ref[...] = _sigmoid(bb)

    off = SUBLANE - (GDN_CONV - 1)
    conv = cw_ref[0:1, :] * cb_ref[off:off + tb, :]
    for j in range(1, GDN_CONV):
        conv = conv + cw_ref[j:j + 1, :] * cb_ref[off + j:off + j + tb, :]
    cb_ref[0:SUBLANE, :] = cb_ref[tb:tb + SUBLANE, :]
    qkv = _silu(conv)
    qscale = hd_ ** -0.5
    for hq in range(GDN_QK_HEADS):
        sl = slice(hq * hd_, (hq + 1) * hd_)
        qh = qkv[:, hq * hd_:(hq + 1) * hd_]
        kh = qkv[:, dqk + hq * hd_:dqk + (hq + 1) * hd_]
        q_ref[:, sl] = qh * (lax.rsqrt(jnp.sum(qh * qh, axis=-1, keepdims=True) + 1e-6) * qscale)
        k_ref[:, sl] = kh * lax.rsqrt(jnp.sum(kh * kh, axis=-1, keepdims=True) + 1e-6)
    v_ref[...] = qkv[:, 2 * dqk:]

    tril = _tril(CHUNK).astype(F32)
    lower = _tril(CHUNK)
    strict = _tril(CHUNK, -1)
    inv_masks = _block_inverse_masks(CHUNK)

    for c in range(nc):
        rows = slice(c * CHUNK, (c + 1) * CHUNK)
        gc = _hdot(tril, g_ref[rows, :])
        gct = gc.T
        gl = gc[CHUNK - 1:CHUNK, :]
        eg = jnp.exp(gc)
        eendt = jnp.exp(gl - gc).T
        dec = jnp.exp(gl)
        bt = bt_ref[rows, :]
        beg = bt * eg
        hq_r = range(GDN_QK_HEADS)
        hv_r = range(GDN_V_HEADS)
        qs = [q_ref[rows, hq * hd_:(hq + 1) * hd_] for hq in hq_r]
        ks = [k_ref[rows, hq * hd_:(hq + 1) * hd_] for hq in hq_r]
        kks = [_bdot_nt(ks[hq], ks[hq]) for hq in hq_r]
        qk0s = [_bdot_nt(qs[hq], ks[hq]) for hq in hq_r]
        khts = [ks[hq].T for hq in hq_r]
        cols = [slice(hvi, hvi + 1) for hvi in hv_r]
        decays = []
        for hvi in hv_r:
            diff = gc[:, cols[hvi]] - gct[hvi:hvi + 1, :]
            decays.append(jnp.where(lower, jnp.exp(jnp.where(lower, diff, 0.0)), 0.0))
        amats = [jnp.where(strict, bt[:, cols[hvi]] * kks[hvi // rep] * decays[hvi], 0.0) for hvi in hv_r]
        tinvs = _unit_lower_inverse(amats, inv_masks)
        rhs = [jnp.concatenate([bt[:, cols[hvi]] * v_ref[rows, hvi * hd_:(hvi + 1) * hd_],
                                beg[:, cols[hvi]] * ks[hvi // rep]], axis=1) for hvi in hv_r]
        uws = [_bdot(tinvs[hvi], rhs[hvi]) for hvi in hv_r]
        ss = [s_ref[hvi] for hvi in hv_r]
        wss = [_bdot(uws[hvi][:, hd_:], ss[hvi]) for hvi in hv_r]
        qss = [_bdot(qs[hvi // rep] * eg[:, cols[hvi]], ss[hvi]) for hvi in hv_r]
        vns = [uws[hvi][:, :hd_] - wss[hvi] for hvi in hv_r]
        outs = [qss[hvi] + _bdot(qk0s[hvi // rep] * decays[hvi], vns[hvi]) for hvi in hv_r]
        upds = [_bdot(khts[hvi // rep] * eendt[hvi:hvi + 1, :], vns[hvi]) for hvi in hv_r]
        for hvi in hv_r:
            ob_ref[rows, hvi * hd_:(hvi + 1) * hd_] = outs[hvi]
            s_ref[hvi] = dec[:, cols[hvi]] * ss[hvi] + upds[hvi]

    hg = hg_ref[...]
    parts = []
    for hvi in range(GDN_V_HEADS):
        vs = slice(hvi * hd_, (hvi + 1) * hd_)
        oh = ob_ref[:, vs]
        ms = jnp.mean(oh * oh, axis=-1, keepdims=True)
        parts.append((oh * lax.rsqrt(ms + RMS_EPS) * hg * _silu(z_ref[:, vs])).astype(BF16))
    y = jnp.concatenate(parts, axis=-1)
    m = jnp.dot(y, wout_ref[...], preferred_element_type=F32)
    o_ref[0] = x + gt_ref[0] * m


def _gdn_call(x, sh, sc, gt, ng, w_in, conv_w, a_log, dt_bias, head_g, w_out, *, tb):
    b, s, d = x.shape
    dqk = GDN_QK_HEADS * GDN_HEAD_DIM
    dvv = GDN_V_HEADS * GDN_HEAD_DIM
    nqkv = 2 * dqk + dvv
    wqkv = w_in[:, :nqkv].astype(BF16)
    wz = w_in[:, nqkv:nqkv + dvv].astype(BF16)
    padh = ((0, 0), (0, LANE - GDN_V_HEADS))
    wa = jnp.pad(w_in[:, nqkv + dvv:nqkv + dvv + GDN_V_HEADS], padh).astype(BF16)
    wb = jnp.pad(w_in[:, nqkv + dvv + GDN_V_HEADS:], padh).astype(BF16)
    alog = jnp.pad(a_log.reshape(1, -1), padh)
    dtb = jnp.pad(dt_bias.reshape(1, -1), padh)
    const2 = lambda bi, i: (0, 0)
    vec = pl.BlockSpec((1, 1, d), lambda bi, i: (bi, 0, 0))
    kern = functools.partial(_gdn_kernel, tb=tb)
    return pl.pallas_call(
        kern,
        grid=(b, s // tb),
        in_specs=[
            pl.BlockSpec((1, tb, d), lambda bi, i: (bi, i, 0)),
            vec, vec, vec,
            pl.BlockSpec((1, d), const2),
            pl.BlockSpec(wqkv.shape, const2),
            pl.BlockSpec(wz.shape, const2),
            pl.BlockSpec(wa.shape, const2),
            pl.BlockSpec(wb.shape, const2),
            pl.BlockSpec(conv_w.shape, const2),
            pl.BlockSpec((1, LANE), const2),
            pl.BlockSpec((1, LANE), const2),
            pl.BlockSpec((1, GDN_HEAD_DIM), const2),
            pl.BlockSpec((dvv, d), const2),
        ],
        out_specs=pl.BlockSpec((1, tb, d), lambda bi, i: (bi, i, 0)),
        out_shape=jax.ShapeDtypeStruct(x.shape, F32),
        scratch_shapes=[
            pltpu.VMEM((GDN_V_HEADS, GDN_HEAD_DIM, GDN_HEAD_DIM), F32),
            pltpu.VMEM((tb + SUBLANE, nqkv), F32),
            pltpu.VMEM((tb, dqk), F32),
            pltpu.VMEM((tb, dqk), F32),
            pltpu.VMEM((tb, dvv), F32),
            pltpu.VMEM((tb, dvv), F32),
            pltpu.VMEM((tb, LANE), F32),
            pltpu.VMEM((tb, LANE), F32),
            pltpu.VMEM((tb, dvv), F32),
        ],
        compiler_params=pltpu.CompilerParams(
            dimension_semantics=("arbitrary", "arbitrary"), vmem_limit_bytes=VMEM_LIMIT),
        name="gdn_layer",
    )(x, sh, sc, gt, ng.reshape(1, d), wqkv, wz, wa, wb, conv_w, alog, dtb,
      head_g.reshape(1, GDN_HEAD_DIM), w_out.astype(BF16))


def _top_values(x, n, with_rank=False):
    vals = []
    rank = jnp.full(x.shape, float(n), F32) if with_rank else None
    for r in range(n):
        m = jnp.max(x, axis=0, keepdims=True)
        vals.append(m)
        if with_rank:
            rank = jnp.where(x == m, float(r), rank)
        if r + 1 < n:
            x = jnp.where(x == m, -jnp.inf, x)
    return (vals, rank) if with_rank else vals


def _pair_pack(x):
    bits = lax.bitcast_convert_type(x.astype(BF16).astype(F32), jnp.int32)
    return bits | lax.shift_right_logical(bits, jnp.int32(16))


def _stack_rows(rows):
    l = rows[0].shape[1]
    sub = lax.broadcasted_iota(jnp.int32, (SUBLANE, l), 0)
    out = jnp.broadcast_to(rows[0], (SUBLANE, l))
    for i in range(1, len(rows)):
        out = jnp.where(sub == i, rows[i], out)
    return out


def _peer_kernel(x_ref, sh_ref, sc_ref, gt_ref, ng_ref, fg_ref, wq_ref, k1_ref, k2_ref, u_ref, vt_ref, o_ref,
                 tt_ref, s1_ref, s2_ref, r1_ref, w1_ref, jb_ref, e2_ref, yt_ref, *half_refs, tt, eb, ne, final):
    g = pl.program_id(2)
    nh = tt // PEER_NCOLS
    ht_refs = half_refs[:nh]
    at_refs = half_refs[nh:]
    nj = tt // LANE
    ab = eb // PEER_NKEYS
    k = PEER_TOPK

    @pl.when(g == 0)
    def _():
        x = x_ref[0]
        t = _norm_mod(x, ng_ref[...], sc_ref[0], sh_ref[0])
        tb = t.astype(BF16)
        tt_ref[...] = t.T.astype(BF16)
        q = jnp.dot(tb, wq_ref[...], preferred_element_type=F32)
        for h in range(PEER_HEADS):
            q1 = q[:, (2 * h) * PEER_HALF:(2 * h + 1) * PEER_HALF]
            q2 = q[:, (2 * h + 1) * PEER_HALF:(2 * h + 2) * PEER_HALF]
            s1_ref[h] = lax.dot_general(k1_ref[...], q1, NT, precision=HI, preferred_element_type=F32)
            s2_ref[h] = lax.dot_general(k2_ref[...], q2, NT, precision=HI, preferred_element_type=F32)
        yt_ref[...] = jnp.zeros_like(yt_ref)

        def route_one(idx):
            h = idx // nj
            c0 = pl.multiple_of((idx % nj) * LANE, LANE)
            cols = pl.ds(c0, LANE)
            x1 = s1_ref[h, :, cols]
            x2 = s2_ref[h, :, cols]
            v1, rank1 = _top_values(x1, k + 1, with_rank=True)
            v2 = _top_values(x2, k + 1)
            v2lo = _stack_rows(v2[:SUBLANE])
            v2hi = _stack_rows(v2[SUBLANE:2 * SUBLANE])
            v1hi = _stack_rows(v1[SUBLANE:2 * SUBLANE])
            sub = lax.broadcasted_iota(jnp.int32, (SUBLANE, LANE), 0)
            last = jnp.where(sub == 0, v1[0] + v2[k], jnp.where(sub == 1, v1[k] + v2[0], -jnp.inf))
            tiles = [v1[0] + v2lo, v1[0] + v2hi, v1hi + v2[0], last]
            tiles += [v1[i] + v2lo for i in range(1, SUBLANE)]
            cand = jnp.concatenate(tiles, axis=0)
            tops = _top_values(cand, k + 1)
            kth = tops[k - 1]
            cut = 0.5 * (kth + tops[k])
            top = v1[0] + v2[0]
            z = jnp.sum(jnp.where(cand >= kth, jnp.exp(cand - top), 0.0), axis=0, keepdims=True)
            c2 = cut - x2
            passing = jnp.zeros_like(c2)
            for i in range(k + 1):
                passing = passing + jnp.where(v1[i] >= c2, 1.0, 0.0)
            r1_ref[h, :, cols] = _pair_pack(rank1)
            w1_ref[h, :, cols] = _pair_pack(jnp.exp(x1 - v1[0]) / z)
            jb_ref[h, :, cols] = passing.astype(BF16)
            e2_ref[h, :, cols] = jnp.exp(x2 - v2[0]).astype(BF16)

        def route(i, carry):
            route_one(2 * i)
            route_one(2 * i + 1)
            return carry

        lax.fori_loop(0, PEER_HEADS * nj // 2, route, 0)

    em = jnp.clip(g - 1, 0, ne - 1)
    arows = pl.ds(pl.multiple_of(em * ab, SUBLANE), SUBLANE)
    bh = PEER_NKEYS // 2
    d = yt_ref.shape[0]
    ngroup = PEER_NCOLS // LANE

    def gate_block(n, jj, ag, hb):
        cols = slice(n * PEER_NCOLS + jj * LANE, n * PEER_NCOLS + (jj + 1) * LANE)
        lcols = slice(jj * LANE, (jj + 1) * LANE)
        brows = slice(hb * bh, (hb + 1) * bh)
        als = tuple(range(ag * PEER_AROWS, (ag + 1) * PEER_AROWS))
        r1t = [r1_ref[h, arows, cols] for h in range(PEER_HEADS)]
        w1t = [w1_ref[h, arows, cols] for h in range(PEER_HEADS)]

        def row(tile, al):
            return pltpu.bitcast(jnp.broadcast_to(tile[al:al + 1, :], (SUBLANE, LANE)), BF16)

        nv = bh // PACK
        accs = [[None] * nv for _ in als]
        for h in range(PEER_HEADS):
            jb = jb_ref[h, brows, cols]
            e2 = e2_ref[h, brows, cols]
            r1rows = [row(r1t[h], al) for al in als]
            w1rows = [row(w1t[h], al) for al in als]
            for v in range(nv):
                vr = slice(v * PACK, (v + 1) * PACK)
                jbv = jb[vr]
                e2v = e2[vr]
                for i in range(len(als)):
                    term = jnp.where(r1rows[i] < jbv, e2v * w1rows[i], jnp.zeros_like(e2v))
                    accs[i][v] = term if h == 0 else accs[i][v] + term
        fold = None
        for i, al in enumerate(als):
            for v in range(nv):
                r0 = al * PEER_NKEYS + hb * bh + v * PACK
                out = accs[i][v] * _gelu_tanh(ht_refs[n][r0:r0 + PACK, lcols].astype(BF16))
                at_refs[n][r0:r0 + PACK, lcols] = out
                bits = pltpu.bitcast(out, jnp.int32)
                fold = bits if fold is None else fold | bits
        return fold & jnp.minimum(g, 0)

    def after(zero, rhs):
        head = pltpu.bitcast(rhs[:PEER_KTILE], jnp.int32)
        head = head | jnp.tile(zero, (head.shape[0] // SUBLANE, head.shape[1] // LANE))
        return jnp.concatenate([pltpu.bitcast(head, BF16), rhs[PEER_KTILE:]], axis=0)

    def matmul_pieces(n, second_mm, first_mm):
        ncols = slice(n * PEER_NCOLS, (n + 1) * PEER_NCOLS)
        pieces = []
        for m in range(d // PEER_MROWS):
            mrows = slice(m * PEER_MROWS, (m + 1) * PEER_MROWS)

            def second(zero, mrows=mrows):
                yt_ref[mrows, ncols] += jnp.dot(vt_ref[0, mrows, :], after(zero, at_refs[n][...]),
                                                preferred_element_type=F32)

            def first(zero, mrows=mrows):
                ht_refs[n][mrows, :] = jnp.dot(u_ref[mrows, :], after(zero, tt_ref[:, ncols]),
                                               preferred_element_type=F32)
            pieces += ([second] if second_mm else []) + ([first] if first_mm else [])
        return pieces

    def run_step(gate, second_mm, first_mm):
        pending = []
        zero = lax.broadcasted_iota(jnp.int32, (SUBLANE, LANE), 0) & jnp.minimum(g, 0)
        for n in range(tt // PEER_NCOLS):
            blk = 0
            if gate:
                for jj in range(ngroup):
                    for ag in range(ab // PEER_AROWS):
                        for hb in range(2):
                            zero = gate_block(n, jj, ag, hb)
                            if pending and blk % PEER_GATE_PER_PIECE == 1:
                                pending.pop(0)(zero)
                            blk += 1
            for piece in pending:
                piece(zero)
            pending = matmul_pieces(n, second_mm, first_mm)
        for piece in pending:
            piece(zero)

    @pl.when(g == 0)
    def _():
        run_step(False, False, True)

    @pl.when(jnp.logical_and(g > 0, g < ne))
    def _():
        run_step(True, True, True)

    @pl.when(g == ne)
    def _():
        run_step(True, True, False)
        o = x_ref[0] + gt_ref[0] * yt_ref[...].T
        if final:
            o = o * lax.rsqrt(jnp.mean(o * o, axis=-1, keepdims=True) + RMS_EPS) * fg_ref[...]
        o_ref[0] = o


def _peer_call(x, sh, sc, gt, ng, fg, wq, k1, k2, u, v, *, tt, eb, final):
    b, s, d = x.shape
    assert eb == SUBLANE * PEER_NKEYS and u.shape[0] % eb == 0
    ne = u.shape[0] // eb
    ub = u.astype(BF16)
    vtb = v.astype(BF16).reshape(ne, eb, d).transpose(0, 2, 1)
    const3 = lambda bi, i, e: (0, 0)
    vec = pl.BlockSpec((1, 1, d), lambda bi, i, e: (bi, 0, 0))
    kern = functools.partial(_peer_kernel, tt=tt, eb=eb, ne=ne, final=final)
    sshape = (PEER_HEADS, PEER_NKEYS, tt)
    return pl.pallas_call(
        kern,
        grid=(b, s // tt, ne + 1),
        in_specs=[
            pl.BlockSpec((1, tt, d), lambda bi, i, e: (bi, i, 0)),
            vec, vec, vec,
            pl.BlockSpec((1, d), const3),
            pl.BlockSpec((1, d), const3),
            pl.BlockSpec(wq.shape, const3),
            pl.BlockSpec(k1.shape, const3),
            pl.BlockSpec(k2.shape, const3),
            pl.BlockSpec((eb, d), lambda bi, i, g: (jnp.minimum(g, ne - 1), 0)),
            pl.BlockSpec((1, d, eb), lambda bi, i, g: (jnp.maximum(g - 1, 0), 0, 0)),
        ],
        out_specs=pl.BlockSpec((1, tt, d), lambda bi, i, e: (bi, i, 0)),
        out_shape=jax.ShapeDtypeStruct(x.shape, F32),
        scratch_shapes=[
            pltpu.VMEM((d, tt), BF16),
            pltpu.VMEM(sshape, F32),
            pltpu.VMEM(sshape, F32),
            pltpu.VMEM(sshape, jnp.int32),
            pltpu.VMEM(sshape, jnp.int32),
            pltpu.VMEM(sshape, BF16),
            pltpu.VMEM(sshape, BF16),
            pltpu.VMEM((d, tt), F32),
        ] + [pltpu.VMEM((eb, PEER_NCOLS), F32)] * (tt // PEER_NCOLS)
          + [pltpu.VMEM((eb, PEER_NCOLS), BF16)] * (tt // PEER_NCOLS),
        compiler_params=pltpu.CompilerParams(
            dimension_semantics=("arbitrary", "arbitrary", "arbitrary"), vmem_limit_bytes=VMEM_LIMIT),
        name="peer_layer",
    )(x, sh, sc, gt, ng.reshape(1, d), fg.reshape(1, d), wq.astype(BF16), k1, k2, ub, vtb)


def _pick(n, pref):
    return pref if n % pref == 0 else n


def kernel(x, c, ada_w, ada_b, norm_mix_g, norm_ffn_g, gla_w_in, gla_w_g2, gla_b_g2, gla_norm_g,
           gla_w_out, gdn_w_in, gdn_conv_w, gdn_a_log, gdn_dt_bias, gdn_norm_g, gdn_w_out,
           peer_wq, peer_k1, peer_k2, peer_u, peer_v, final_norm_g):
    b, s, d = x.shape
    depth = ada_w.shape[0]
    c_pad = jnp.pad(c, ((0, SUBLANE - b % SUBLANE if b % SUBLANE else 0), (0, 0)))
    mod = _mod_call(c_pad, ada_w, ada_b)[:, :b, :].reshape(depth, b, 6, 1, d)
    tb_mix = _pick(s, 256)
    tt = _pick(s, 512)
    eb = 1024
    for i in range(depth):
        sh1, sc1, gt1, sh2, sc2, gt2 = (mod[i, :, j] for j in range(6))
        j = i // 2
        if i % 2 == 0:
            x = _gla_call(x, sh1, sc1, gt1, norm_mix_g[i], gla_w_in[j], gla_w_g2[j], gla_b_g2[j],
                          gla_norm_g[j], gla_w_out[j], tb=tb_mix)
        else:
            x = _gdn_call(x, sh1, sc1, gt1, norm_mix_g[i], gdn_w_in[j], gdn_conv_w[j], gdn_a_log[j],
                          gdn_dt_bias[j], gdn_norm_g[j], gdn_w_out[j], tb=tb_mix)
        x = _peer_call(x, sh2, sc2, gt2, norm_ffn_g[i], final_norm_g, peer_wq[i], peer_k1[i], peer_k2[i],
                       peer_u[i], peer_v[i], tt=tt, eb=eb, final=(i == depth - 1))
    return x
```

```python
import functools

import jax
import jax.numpy as jnp
from jax import lax
from jax.experimental import pallas as pl
from jax.experimental.pallas import tpu as pltpu

F32 = jnp.float32
BF16 = jnp.bfloat16
HI = lax.Precision.HIGHEST

RMS_EPS = 1e-6
CHUNK = 64

GLA_HEADS = 4
GLA_GATE_RANK = 16
GLA_TAU = 16.0

GDN_QK_HEADS = 8
GDN_V_HEADS = 16
GDN_HEAD_DIM = 128
GDN_CONV = 4

PEER_HEADS = 8
PEER_TOPK = 16
PEER_NKEYS = 128
PEER_HALF = 128
PEER_NCOLS = 256
PEER_KTILE = 256
PEER_MROWS = 1024
PEER_AROWS = 4
PEER_GATE_PER_PIECE = 4

LANE = 128
SUBLANE = 8
PACK = 16
VMEM_LIMIT = 56 * 1024 * 1024

NT = (((1,), (1,)), ((), ()))


def _sigmoid(x):
    return 1.0 / (1.0 + jnp.exp(-x))


def _silu(x):
    return x * _sigmoid(x)


def _softplus(x):
    return jnp.maximum(x, 0.0) + jnp.log1p(jnp.exp(-jnp.abs(x)))


def _gelu_tanh(x):
    c = 0.7978845608028654
    hx = 0.5 * x
    return hx + hx * jnp.tanh(x * (c + (c * 0.044715) * (x * x)))


def _norm_mod(x, g, sc, sh):
    ms = jnp.mean(x * x, axis=-1, keepdims=True)
    return (x * lax.rsqrt(ms + RMS_EPS) * g) * (1.0 + sc) + sh


def _bdot(a, b):
    return jnp.dot(a.astype(BF16), b.astype(BF16), preferred_element_type=F32)


def _bdot_nt(a, b):
    return lax.dot_general(a.astype(BF16), b.astype(BF16), NT, preferred_element_type=F32)


def _hdot(a, b):
    return jnp.dot(a, b, precision=HI, preferred_element_type=F32)


def _split_bf16(x):
    hi = x.astype(BF16)
    return hi, (x - hi.astype(F32)).astype(BF16)


def _tril(n, k=0):
    r = lax.broadcasted_iota(jnp.int32, (n, n), 0)
    c = lax.broadcasted_iota(jnp.int32, (n, n), 1)
    return (c - r) <= k


def _mod_kernel(c_ref, w_ref, b_ref, o_ref):
    c = c_ref[...]
    o_ref[0] = _hdot(_silu(c), w_ref[0]) + b_ref[0]


def _mod_call(c_pad, ada_w, ada_b):
    depth, d, d6 = ada_w.shape
    nb = d6 // d
    return pl.pallas_call(
        _mod_kernel,
        grid=(depth, nb),
        in_specs=[
            pl.BlockSpec(c_pad.shape, lambda i, j: (0, 0)),
            pl.BlockSpec((1, d, d), lambda i, j: (i, 0, j)),
            pl.BlockSpec((1, 1, d), lambda i, j: (i, 0, j)),
        ],
        out_specs=pl.BlockSpec((1, c_pad.shape[0], d), lambda i, j: (i, 0, j)),
        out_shape=jax.ShapeDtypeStruct((depth, c_pad.shape[0], d6), F32),
        name="adaln_mod",
    )(c_pad, ada_w, ada_b.reshape(depth, 1, d6))


def _gla_kernel(x_ref, sh_ref, sc_ref, gt_ref, ng_ref, wqkvr_ref, wg1_ref, wg2_ref, bg2_ref,
                hg_ref, wout_ref, o_ref, st_ref, p_ref, la_ref, ob_ref, *, tb, dk, dv):
    hk = dk // GLA_HEADS
    hv = dv // GLA_HEADS
    nc = tb // CHUNK

    @pl.when(pl.program_id(1) == 0)
    def _():
        st_ref[...] = jnp.zeros_like(st_ref)

    x = x_ref[0]
    h = _norm_mod(x, ng_ref[...], sc_ref[0], sh_ref[0])
    hb = h.astype(BF16)
    p_ref[...] = jnp.dot(hb, wqkvr_ref[...], preferred_element_type=F32)
    g1 = jnp.dot(hb, wg1_ref[...], preferred_element_type=F32)
    g = jnp.dot(g1.astype(BF16), wg2_ref[...], preferred_element_type=F32) + bg2_ref[...]
    la_ref[...] = (jnp.minimum(g, 0.0) - jnp.log1p(jnp.exp(-jnp.abs(g)))) * (1.0 / GLA_TAU)

    tril = _tril(CHUNK).astype(F32)
    causal = _tril(CHUNK)
    scale = hk ** -0.5

    for c in range(nc):
        rows = slice(c * CHUNK, (c + 1) * CHUNK)
        cum = _hdot(tril, la_ref[rows, :])
        cl = cum[CHUNK - 1:CHUNK, :]
        e_pos = jnp.exp(cum)
        e_neg = jnp.exp(-cum)
        e_end = jnp.exp(cl - cum)
        dec = jnp.exp(cl)
        hr = range(GLA_HEADS)
        ksl = [slice(hd * hk, (hd + 1) * hk) for hd in hr]
        qis = [p_ref[rows, hd * hk:(hd + 1) * hk] * (scale * e_pos[:, ksl[hd]]) for hd in hr]
        kraw = [p_ref[rows, dk + hd * hk:dk + (hd + 1) * hk] for hd in hr]
        vs = [p_ref[rows, 2 * dk + hd * hv:2 * dk + (hd + 1) * hv] for hd in hr]
        attns = [jnp.where(causal, _bdot_nt(qis[hd], kraw[hd] * e_neg[:, ksl[hd]]), 0.0) for hd in hr]
        sts = [st_ref[hd] for hd in hr]
        inter = [_bdot_nt(qis[hd], sts[hd]) for hd in hr]
        upds = [_bdot(vs[hd].T, kraw[hd] * e_end[:, ksl[hd]]) for hd in hr]
        intra = [_bdot(attns[hd], vs[hd]) for hd in hr]
        for hd in hr:
            ob_ref[rows, hd * hv:(hd + 1) * hv] = intra[hd] + inter[hd]
            st_ref[hd] = dec[:, ksl[hd]] * sts[hd] + upds[hd]

    hg = hg_ref[...]
    parts = []
    for hd in range(GLA_HEADS):
        oh = ob_ref[:, hd * hv:(hd + 1) * hv]
        ms = jnp.mean(oh * oh, axis=-1, keepdims=True)
        r = p_ref[:, 2 * dk + dv + hd * hv:2 * dk + dv + (hd + 1) * hv]
        parts.append((oh * lax.rsqrt(ms + RMS_EPS) * hg * _silu(r)).astype(BF16))
    y = jnp.concatenate(parts, axis=-1)
    m = jnp.dot(y, wout_ref[...], preferred_element_type=F32)
    o_ref[0] = x + gt_ref[0] * m


def _gla_call(x, sh, sc, gt, ng, w_in, w_g2, b_g2, head_g, w_out, *, tb):
    b, s, d = x.shape
    dk = w_g2.shape[1]
    dv = w_out.shape[0]
    wqkvr = w_in[:, :2 * dk + 2 * dv].astype(BF16)
    wg1 = jnp.pad(w_in[:, 2 * dk + 2 * dv:], ((0, 0), (0, LANE - GLA_GATE_RANK))).astype(BF16)
    wg2 = jnp.pad(w_g2, ((0, LANE - GLA_GATE_RANK), (0, 0))).astype(BF16)
    hv = dv // GLA_HEADS
    hk = dk // GLA_HEADS
    const2 = lambda bi, i: (0, 0)
    vec = pl.BlockSpec((1, 1, d), lambda bi, i: (bi, 0, 0))
    kern = functools.partial(_gla_kernel, tb=tb, dk=dk, dv=dv)
    return pl.pallas_call(
        kern,
        grid=(b, s // tb),
        in_specs=[
            pl.BlockSpec((1, tb, d), lambda bi, i: (bi, i, 0)),
            vec, vec, vec,
            pl.BlockSpec((1, d), const2),
            pl.BlockSpec(wqkvr.shape, const2),
            pl.BlockSpec(wg1.shape, const2),
            pl.BlockSpec(wg2.shape, const2),
            pl.BlockSpec((1, dk), const2),
            pl.BlockSpec((1, hv), const2),
            pl.BlockSpec((dv, d), const2),
        ],
        out_specs=pl.BlockSpec((1, tb, d), lambda bi, i: (bi, i, 0)),
        out_shape=jax.ShapeDtypeStruct(x.shape, F32),
        scratch_shapes=[
            pltpu.VMEM((GLA_HEADS, hv, hk), F32),
            pltpu.VMEM((tb, 2 * dk + 2 * dv), F32),
            pltpu.VMEM((tb, dk), F32),
            pltpu.VMEM((tb, dv), F32),
        ],
        compiler_params=pltpu.CompilerParams(
            dimension_semantics=("arbitrary", "arbitrary"), vmem_limit_bytes=VMEM_LIMIT),
        name="gla_layer",
    )(x, sh, sc, gt, ng.reshape(1, d), wqkvr, wg1, wg2, b_g2.reshape(1, dk),
      head_g.reshape(1, hv), w_out.astype(BF16))


def _block_inverse_masks(n):
    r = lax.broadcasted_iota(jnp.int32, (n, n), 0)
    c = lax.broadcasted_iota(jnp.int32, (n, n), 1)
    eye = (r == c).astype(F32)
    first = (r // 2) == (c // 2)
    quads = []
    s = 2
    while s < n:
        quads.append(((r // (2 * s)) == (c // (2 * s))) & ((r // s) != (c // s)))
        s *= 2
    return eye, first, quads


def _unit_lower_inverse(a_list, masks):
    eye, first, quads = masks
    ds = [eye - jnp.where(first, a, 0.0) for a in a_list]
    for quad in quads:
        ms = [_bdot(d, jnp.where(quad, a, 0.0)) for d, a in zip(ds, a_list)]
        ds = [d - _bdot(m, d) for d, m in zip(ds, ms)]
    return ds


def _gdn_kernel(x_ref, sh_ref, sc_ref, gt_ref, ng_ref, wqkv_ref, wz_ref, wa_ref, wb_ref, cw_ref,
                alog_ref, dtb_ref, hg_ref, wout_ref, o_ref,
                s_ref, cb_ref, q_ref, k_ref, v_ref, z_ref, g_ref, bt_ref, ob_ref, *, tb):
    dqk = GDN_QK_HEADS * GDN_HEAD_DIM
    dvv = GDN_V_HEADS * GDN_HEAD_DIM
    hd_ = GDN_HEAD_DIM
    nc = tb // CHUNK
    rep = GDN_V_HEADS // GDN_QK_HEADS

    @pl.when(pl.program_id(1) == 0)
    def _():
        s_ref[...] = jnp.zeros_like(s_ref)
        cb_ref[0:SUBLANE, :] = jnp.zeros((SUBLANE, cb_ref.shape[1]), F32)

    x = x_ref[0]
    h = _norm_mod(x, ng_ref[...], sc_ref[0], sh_ref[0])
    hb = h.astype(BF16)
    cb_ref[SUBLANE:SUBLANE + tb, :] = jnp.dot(hb, wqkv_ref[...], preferred_element_type=F32)
    z_ref[...] = jnp.dot(hb, wz_ref[...], preferred_element_type=F32)
    a = jnp.dot(hb, wa_ref[...], preferred_element_type=F32)
    bb = jnp.dot(hb, wb_ref[...], preferred_element_type=F32)
    g_ref[...] = -jnp.exp(alog_ref[...]) * _softplus(a + dtb_ref[...])
    bt_ref[...] = _sigmoid(bb)

    off = SUBLANE - (GDN_CONV - 1)
    conv = cw_ref[0:1, :] * cb_ref[off:off + tb, :]
    for j in range(1, GDN_CONV):
        conv = conv + cw_ref[j:j + 1, :] * cb_ref[off + j:off + j + tb, :]
    cb_ref[0:SUBLANE, :] = cb_ref[tb:tb + SUBLANE, :]
    qkv = _silu(conv)
    qscale = hd_ ** -0.5
    for hq in range(GDN_QK_HEADS):
        sl = slice(hq * hd_, (hq + 1) * hd_)
        qh = qkv[:, hq * hd_:(hq + 1) * hd_]
        kh = qkv[:, dqk + hq * hd_:dqk + (hq + 1) * hd_]
        q_ref[:, sl] = qh * (lax.rsqrt(jnp.sum(qh * qh, axis=-1, keepdims=True) + 1e-6) * qscale)
        k_ref[:, sl] = kh * lax.rsqrt(jnp.sum(kh * kh, axis=-1, keepdims=True) + 1e-6)
    v_ref[...] = qkv[:, 2 * dqk:]

    tril = _tril(CHUNK).astype(F32)
    lower = _tril(CHUNK)
    strict = _tril(CHUNK, -1)
    inv_masks = _block_inverse_masks(CHUNK)

    for c in range(nc):
        rows = slice(c * CHUNK, (c + 1) * CHUNK)
        gc = _hdot(tril, g_ref[rows, :])
        gct = gc.T
        gl = gc[CHUNK - 1:CHUNK, :]
        eg = jnp.exp(gc)
        eendt = jnp.exp(gl - gc).T
        dec = jnp.exp(gl)
        bt = bt_ref[rows, :]
        beg = bt * eg
        hq_r = range(GDN_QK_HEADS)
        hv_r = range(GDN_V_HEADS)
        qs = [q_ref[rows, hq * hd_:(hq + 1) * hd_] for hq in hq_r]
        ks = [k_ref[rows, hq * hd_:(hq + 1) * hd_] for hq in hq_r]
        kks = [_bdot_nt(ks[hq], ks[hq]) for hq in hq_r]
        qk0s = [_bdot_nt(qs[hq], ks[hq]) for hq in hq_r]
        khts = [ks[hq].T for hq in hq_r]
        cols = [slice(hvi, hvi + 1) for hvi in hv_r]
        decays = []
        for hvi in hv_r:
            diff = gc[:, cols[hvi]] - gct[hvi:hvi + 1, :]
            decays.append(jnp.where(lower, jnp.exp(jnp.where(lower, diff, 0.0)), 0.0))
        amats = [jnp.where(strict, bt[:, cols[hvi]] * kks[hvi // rep] * decays[hvi], 0.0) for hvi in hv_r]
        tinvs = _unit_lower_inverse(amats, inv_masks)
        rhs = [jnp.concatenate([bt[:, cols[hvi]] * v_ref[rows, hvi * hd_:(hvi + 1) * hd_],
                                beg[:, cols[hvi]] * ks[hvi // rep]], axis=1) for hvi in hv_r]
        uws = [_bdot(tinvs[hvi], rhs[hvi]) for hvi in hv_r]
        ss = [s_ref[hvi] for hvi in hv_r]
        wss = [_bdot(uws[hvi][:, hd_:], ss[hvi]) for hvi in hv_r]
        qss = [_bdot(qs[hvi // rep] * eg[:, cols[hvi]], ss[hvi]) for hvi in hv_r]
        vns = [uws[hvi][:, :hd_] - wss[hvi] for hvi in hv_r]
        outs = [qss[hvi] + _bdot(qk0s[hvi // rep] * decays[hvi], vns[hvi]) for hvi in hv_r]
        upds = [_bdot(khts[hvi // rep] * eendt[hvi:hvi + 1, :], vns[hvi]) for hvi in hv_r]
        for hvi in hv_r:
            ob_ref[rows, hvi * hd_:(hvi + 1) * hd_] = outs[hvi]
            s_ref[hvi] = dec[:, cols[hvi]] * ss[hvi] + upds[hvi]

    hg = hg_ref[...]
    parts = []
    for hvi in range(GDN_V_HEADS):
        vs = slice(hvi * hd_, (hvi + 1) * hd_)
        oh = ob_ref[:, vs]
        ms = jnp.mean(oh * oh, axis=-1, keepdims=True)
        parts.append((oh * lax.rsqrt(ms + RMS_EPS) * hg * _silu(z_ref[:, vs])).astype(BF16))
    y = jnp.concatenate(parts, axis=-1)
    m = jnp.dot(y, wout_ref[...], preferred_element_type=F32)
    o_ref[0] = x + gt_ref[0] * m


def _gdn_call(x, sh, sc, gt, ng, w_in, conv_w, a_log, dt_bias, head_g, w_out, *, tb):
    b, s, d = x.shape
    dqk = GDN_QK_HEADS * GDN_HEAD_DIM
    dvv = GDN_V_HEADS * GDN_HEAD_DIM
    nqkv = 2 * dqk + dvv
    wqkv = w_in[:, :nqkv].astype(BF16)
    wz = w_in[:, nqkv:nqkv + dvv].astype(BF16)
    padh = ((0, 0), (0, LANE - GDN_V_HEADS))
    wa = jnp.pad(w_in[:, nqkv + dvv:nqkv + dvv + GDN_V_HEADS], padh).astype(BF16)
    wb = jnp.pad(w_in[:, nqkv + dvv + GDN_V_HEADS:], padh).astype(BF16)
    alog = jnp.pad(a_log.reshape(1, -1), padh)
    dtb = jnp.pad(dt_bias.reshape(1, -1), padh)
    const2 = lambda bi, i: (0, 0)
    vec = pl.BlockSpec((1, 1, d), lambda bi, i: (bi, 0, 0))
    kern = functools.partial(_gdn_kernel, tb=tb)
    return pl.pallas_call(
        kern,
        grid=(b, s // tb),
        in_specs=[
            pl.BlockSpec((1, tb, d), lambda bi, i: (bi, i, 0)),
            vec, vec, vec,
            pl.BlockSpec((1, d), const2),
            pl.BlockSpec(wqkv.shape, const2),
            pl.BlockSpec(wz.shape, const2),
            pl.BlockSpec(wa.shape, const2),
            pl.BlockSpec(wb.shape, const2),
            pl.BlockSpec(conv_w.shape, const2),
            pl.BlockSpec((1, LANE), const2),
            pl.BlockSpec((1, LANE), const2),
            pl.BlockSpec((1, GDN_HEAD_DIM), const2),
            pl.BlockSpec((dvv, d), const2),
        ],
        out_specs=pl.BlockSpec((1, tb, d), lambda bi, i: (bi, i, 0)),
        out_shape=jax.ShapeDtypeStruct(x.shape, F32),
        scratch_shapes=[
            pltpu.VMEM((GDN_V_HEADS, GDN_HEAD_DIM, GDN_HEAD_DIM), F32),
            pltpu.VMEM((tb + SUBLANE, nqkv), F32),
            pltpu.VMEM((tb, dqk), F32),
            pltpu.VMEM((tb, dqk), F32),
            pltpu.VMEM((tb, dvv), F32),
            pltpu.VMEM((tb, dvv), F32),
            pltpu.VMEM((tb, LANE), F32),
            pltpu.VMEM((tb, LANE), F32),
            pltpu.VMEM((tb, dvv), F32),
        ],
        compiler_params=pltpu.CompilerParams(
            dimension_semantics=("arbitrary", "arbitrary"), vmem_limit_bytes=VMEM_LIMIT),
        name="gdn_layer",
    )(x, sh, sc, gt, ng.reshape(1, d), wqkv, wz, wa, wb, conv_w, alog, dtb,
      head_g.reshape(1, GDN_HEAD_DIM), w_out.astype(BF16))


def _top_values(x, n, with_rank=False):
    vals = []
    rank = jnp.full(x.shape, float(n), F32) if with_rank else None
    for r in range(n):
        m = jnp.max(x, axis=0, keepdims=True)
        vals.append(m)
        if with_rank:
            rank = jnp.where(x == m, float(r), rank)
        if r + 1 < n:
            x = jnp.where(x == m, -jnp.inf, x)
    return (vals, rank) if with_rank else vals


def _pair_pack(x):
    bits = lax.bitcast_convert_type(x.astype(BF16).astype(F32), jnp.int32)
    return bits | lax.shift_right_logical(bits, jnp.int32(16))


def _stack_rows(rows):
    l = rows[0].shape[1]
    sub = lax.broadcasted_iota(jnp.int32, (SUBLANE, l), 0)
    out = jnp.broadcast_to(rows[0], (SUBLANE, l))
    for i in range(1, len(rows)):
        out = jnp.where(sub == i, rows[i], out)
    return out


def _peer_kernel(x_ref, sh_ref, sc_ref, gt_ref, ng_ref, fg_ref, wq_ref, k1_ref, k2_ref, u_ref, vt_ref, o_ref,
                 tt_ref, s1_ref, s2_ref, r1_ref, w1_ref, jb_ref, e2_ref, yt_ref, *half_refs, tt, eb, ne, final):
    g = pl.program_id(2)
    nh = tt // PEER_NCOLS
    ht_refs = half_refs[:nh]
    at_refs = half_refs[nh:]
    nj = tt // LANE
    ab = eb // PEER_NKEYS
    k = PEER_TOPK

    @pl.when(g == 0)
    def _():
        x = x_ref[0]
        t = _norm_mod(x, ng_ref[...], sc_ref[0], sh_ref[0])
        tb = t.astype(BF16)
        tt_ref[...] = t.T.astype(BF16)
        q = jnp.dot(tb, wq_ref[...], preferred_element_type=F32)
        q_hi, q_lo = _split_bf16(q)
        k_parts = [_split_bf16(k1_ref[...]), _split_bf16(k2_ref[...])]
        for h in range(PEER_HEADS):
            for side, s_ref in enumerate((s1_ref, s2_ref)):
                qs = slice((2 * h + side) * PEER_HALF, (2 * h + side + 1) * PEER_HALF)
                k_hi, k_lo = k_parts[side]
                s_ref[h] = (lax.dot_general(k_hi, q_hi[:, qs], NT, preferred_element_type=F32)
                            + lax.dot_general(k_hi, q_lo[:, qs], NT, preferred_element_type=F32)
                            + lax.dot_general(k_lo, q_hi[:, qs], NT, preferred_element_type=F32))
        yt_ref[...] = jnp.zeros_like(yt_ref)

        def route_one(idx):
            h = idx // nj
            c0 = pl.multiple_of((idx % nj) * LANE, LANE)
            cols = pl.ds(c0, LANE)
            x1 = s1_ref[h, :, cols]
            x2 = s2_ref[h, :, cols]
            v1, rank1 = _top_values(x1, k + 1, with_rank=True)
            v2 = _top_values(x2, k + 1)
            v2lo = _stack_rows(v2[:SUBLANE])
            v2hi = _stack_rows(v2[SUBLANE:2 * SUBLANE])
            v1hi = _stack_rows(v1[SUBLANE:2 * SUBLANE])
            sub = lax.broadcasted_iota(jnp.int32, (SUBLANE, LANE), 0)
            last = jnp.where(sub == 0, v1[0] + v2[k], jnp.where(sub == 1, v1[k] + v2[0], -jnp.inf))
            tiles = [v1[0] + v2lo, v1[0] + v2hi, v1hi + v2[0], last]
            tiles += [v1[i] + v2lo for i in range(1, SUBLANE)]
            cand = jnp.concatenate(tiles, axis=0)
            tops = _top_values(cand, k + 1)
            kth = tops[k - 1]
            cut = 0.5 * (kth + tops[k])
            top = v1[0] + v2[0]
            z = jnp.sum(jnp.where(cand >= kth, jnp.exp(cand - top), 0.0), axis=0, keepdims=True)
            c2 = cut - x2
            passing = jnp.zeros_like(c2)
            for i in range(k + 1):
                passing = passing + jnp.where(v1[i] >= c2, 1.0, 0.0)
            r1_ref[h, :, cols] = _pair_pack(rank1)
            w1_ref[h, :, cols] = _pair_pack(jnp.exp(x1 - v1[0]) / z)
            jb_ref[h, :, cols] = passing.astype(BF16)
            e2_ref[h, :, cols] = jnp.exp(x2 - v2[0]).astype(BF16)

        def route(i, carry):
            route_one(2 * i)
            route_one(2 * i + 1)
            return carry

        lax.fori_loop(0, PEER_HEADS * nj // 2, route, 0)

    em = jnp.clip(g - 1, 0, ne - 1)
    arows = pl.ds(pl.multiple_of(em * ab, SUBLANE), SUBLANE)
    bh = PEER_NKEYS // 2
    d = yt_ref.shape[0]
    ngroup = PEER_NCOLS // LANE

    def gate_block(n, jj, ag, hb):
        cols = slice(n * PEER_NCOLS + jj * LANE, n * PEER_NCOLS + (jj + 1) * LANE)
        lcols = slice(jj * LANE, (jj + 1) * LANE)
        brows = slice(hb * bh, (hb + 1) * bh)
        als = tuple(range(ag * PEER_AROWS, (ag + 1) * PEER_AROWS))
        r1t = [r1_ref[h, arows, cols] for h in range(PEER_HEADS)]
        w1t = [w1_ref[h, arows, cols] for h in range(PEER_HEADS)]

        def row(tile, al):
            return pltpu.bitcast(jnp.broadcast_to(tile[al:al + 1, :], (SUBLANE, LANE)), BF16)

        nv = bh // PACK
        accs = [[None] * nv for _ in als]
        for h in range(PEER_HEADS):
            jb = jb_ref[h, brows, cols]
            e2 = e2_ref[h, brows, cols]
            r1rows = [row(r1t[h], al) for al in als]
            w1rows = [row(w1t[h], al) for al in als]
            for v in range(nv):
                vr = slice(v * PACK, (v + 1) * PACK)
                jbv = jb[vr]
                e2v = e2[vr]
                for i in range(len(als)):
                    term = jnp.where(r1rows[i] < jbv, e2v * w1rows[i], jnp.zeros_like(e2v))
                    accs[i][v] = term if h == 0 else accs[i][v] + term
        fold = None
        for i, al in enumerate(als):
            for v in range(nv):
                r0 = al * PEER_NKEYS + hb * bh + v * PACK
                out = accs[i][v] * _gelu_tanh(ht_refs[n][r0:r0 + PACK, lcols].astype(BF16))
                at_refs[n][r0:r0 + PACK, lcols] = out
                bits = pltpu.bitcast(out, jnp.int32)
                fold = bits if fold is None else fold | bits
        return fold & jnp.minimum(g, 0)

    def after(zero, rhs):
        head = pltpu.bitcast(rhs[:PEER_KTILE], jnp.int32)
        head = head | jnp.tile(zero, (head.shape[0] // SUBLANE, head.shape[1] // LANE))
        return jnp.concatenate([pltpu.bitcast(head, BF16), rhs[PEER_KTILE:]], axis=0)

    def matmul_pieces(n, second_mm, first_mm):
        ncols = slice(n * PEER_NCOLS, (n + 1) * PEER_NCOLS)
        pieces = []
        for m in range(d // PEER_MROWS):
            mrows = slice(m * PEER_MROWS, (m + 1) * PEER_MROWS)

            def second(zero, mrows=mrows):
                yt_ref[mrows, ncols] += jnp.dot(vt_ref[0, mrows, :], after(zero, at_refs[n][...]),
                                                preferred_element_type=F32)

            def first(zero, mrows=mrows):
                ht_refs[n][mrows, :] = jnp.dot(u_ref[mrows, :], after(zero, tt_ref[:, ncols]),
                                               preferred_element_type=F32)
            pieces += ([second] if second_mm else []) + ([first] if first_mm else [])
        return pieces

    def run_step(gate, second_mm, first_mm):
        pending = []
        zero = lax.broadcasted_iota(jnp.int32, (SUBLANE, LANE), 0) & jnp.minimum(g, 0)
        for n in range(tt // PEER_NCOLS):
            blk = 0
            if gate:
                for jj in range(ngroup):
                    for ag in range(ab // PEER_AROWS):
                        for hb in range(2):
                            zero = gate_block(n, jj, ag, hb)
                            if pending and blk % PEER_GATE_PER_PIECE == 1:
                                pending.pop(0)(zero)
                            blk += 1
            for piece in pending:
                piece(zero)
            pending = matmul_pieces(n, second_mm, first_mm)
        for piece in pending:
            piece(zero)

    @pl.when(g == 0)
    def _():
        run_step(False, False, True)

    @pl.when(jnp.logical_and(g > 0, g < ne))
    def _():
        run_step(True, True, True)

    @pl.when(g == ne)
    def _():
        run_step(True, True, False)
        o = x_ref[0] + gt_ref[0] * yt_ref[...].T
        if final:
            o = o * lax.rsqrt(jnp.mean(o * o, axis=-1, keepdims=True) + RMS_EPS) * fg_ref[...]
        o_ref[0] = o


def _peer_call(x, sh, sc, gt, ng, fg, wq, k1, k2, u, v, *, tt, eb, final):
    b, s, d = x.shape
    assert eb == SUBLANE * PEER_NKEYS and u.shape[0] % eb == 0
    ne = u.shape[0] // eb
    ub = u.astype(BF16)
    vtb = v.astype(BF16).reshape(ne, eb, d).transpose(0, 2, 1)
    const3 = lambda bi, i, e: (0, 0)
    vec = pl.BlockSpec((1, 1, d), lambda bi, i, e: (bi, 0, 0))
    kern = functools.partial(_peer_kernel, tt=tt, eb=eb, ne=ne, final=final)
    sshape = (PEER_HEADS, PEER_NKEYS, tt)
    return pl.pallas_call(
        kern,
        grid=(b, s // tt, ne + 1),
        in_specs=[
            pl.BlockSpec((1, tt, d), lambda bi, i, e: (bi, i, 0)),
            vec, vec, vec,
            pl.BlockSpec((1, d), const3),
            pl.BlockSpec((1, d), const3),
            pl.BlockSpec(wq.shape, const3),
            pl.BlockSpec(k1.shape, const3),
            pl.BlockSpec(k2.shape, const3),
            pl.BlockSpec((eb, d), lambda bi, i, g: (jnp.minimum(g, ne - 1), 0)),
            pl.BlockSpec((1, d, eb), lambda bi, i, g: (jnp.maximum(g - 1, 0), 0, 0)),
        ],
        out_specs=pl.BlockSpec((1, tt, d), lambda bi, i, e: (bi, i, 0)),
        out_shape=jax.ShapeDtypeStruct(x.shape, F32),
        scratch_shapes=[
            pltpu.VMEM((d, tt), BF16),
            pltpu.VMEM(sshape, F32),
            pltpu.VMEM(sshape, F32),
            pltpu.VMEM(sshape, jnp.int32),
            pltpu.VMEM(sshape, jnp.int32),
            pltpu.VMEM(sshape, BF16),
            pltpu.VMEM(sshape, BF16),
            pltpu.VMEM((d, tt), F32),
        ] + [pltpu.VMEM((eb, PEER_NCOLS), F32)] * (tt // PEER_NCOLS)
          + [pltpu.VMEM((eb, PEER_NCOLS), BF16)] * (tt // PEER_NCOLS),
        compiler_params=pltpu.CompilerParams(
            dimension_semantics=("arbitrary", "arbitrary", "arbitrary"), vmem_limit_bytes=VMEM_LIMIT),
        name="peer_layer",
    )(x, sh, sc, gt, ng.reshape(1, d), fg.reshape(1, d), wq.astype(BF16), k1, k2, ub, vtb)


def _pick(n, pref):
    return pref if n % pref == 0 else n


def kernel(x, c, ada_w, ada_b, norm_mix_g, norm_ffn_g, gla_w_in, gla_w_g2, gla_b_g2, gla_norm_g,
           gla_w_out, gdn_w_in, gdn_conv_w, gdn_a_log, gdn_dt_bias, gdn_norm_g, gdn_w_out,
           peer_wq, peer_k1, peer_k2, peer_u, peer_v, final_norm_g):
    b, s, d = x.shape
    depth = ada_w.shape[0]
    c_pad = jnp.pad(c, ((0, SUBLANE - b % SUBLANE if b % SUBLANE else 0), (0, 0)))
    mod = _mod_call(c_pad, ada_w, ada_b)[:, :b, :].reshape(depth, b, 6, 1, d)
    tb_gla = _pick(s, 512)
    tb_mix = _pick(s, 256)
    tt = _pick(s, 512)
    eb = 1024
    for i in range(depth):
        sh1, sc1, gt1, sh2, sc2, gt2 = (mod[i, :, j] for j in range(6))
        j = i // 2
        if i % 2 == 0:
            x = _gla_call(x, sh1, sc1, gt1, norm_mix_g[i], gla_w_in[j], gla_w_g2[j], gla_b_g2[j],
                          gla_norm_g[j], gla_w_out[j], tb=tb_gla)
        else:
            x = _gdn_call(x, sh1, sc1, gt1, norm_mix_g[i], gdn_w_in[j], gdn_conv_w[j], gdn_a_log[j],
                          gdn_dt_bias[j], gdn_norm_g[j], gdn_w_out[j], tb=tb_mix)
        x = _peer_call(x, sh2, sc2, gt2, norm_ffn_g[i], final_norm_g, peer_wq[i], peer_k1[i], peer_k2[i],
                       peer_u[i], peer_v[i], tt=tt, eb=eb, final=(i == depth - 1))
    return x
```

```python
import functools

import jax
import jax.numpy as jnp
from jax import lax
from jax.experimental import pallas as pl
from jax.experimental.pallas import tpu as pltpu

F32 = jnp.float32
BF16 = jnp.bfloat16
HI = lax.Precision.HIGHEST

RMS_EPS = 1e-6
CHUNK = 64

GLA_HEADS = 4
GLA_GATE_RANK = 16
GLA_TAU = 16.0

GDN_QK_HEADS = 8
GDN_V_HEADS = 16
GDN_HEAD_DIM = 128
GDN_CONV = 4

PEER_HEADS = 8
PEER_TOPK = 16
PEER_NKEYS = 128
PEER_HALF = 128
PEER_NCOLS = 256
PEER_KTILE = 256
PEER_MROWS = 1024
PEER_AROWS = 4
PEER_GATE_PER_PIECE = 4

LANE = 128
SUBLANE = 8
PACK = 16
VMEM_LIMIT = 56 * 1024 * 1024

NT = (((1,), (1,)), ((), ()))


def _sigmoid(x):
    return 1.0 / (1.0 + jnp.exp(-x))


def _silu(x):
    return x * _sigmoid(x)


def _softplus(x):
    return jnp.maximum(x, 0.0) + jnp.log1p(jnp.exp(-jnp.abs(x)))


def _gelu_tanh(x):
    c = 0.7978845608028654
    hx = 0.5 * x
    return hx + hx * jnp.tanh(x * (c + (c * 0.044715) * (x * x)))


def _norm_mod(x, g, sc, sh):
    ms = jnp.mean(x * x, axis=-1, keepdims=True)
    return (x * lax.rsqrt(ms + RMS_EPS) * g) * (1.0 + sc) + sh


def _bdot(a, b):
    return jnp.dot(a.astype(BF16), b.astype(BF16), preferred_element_type=F32)


def _bdot_nt(a, b):
    return lax.dot_general(a.astype(BF16), b.astype(BF16), NT, preferred_element_type=F32)


def _hdot(a, b):
    return jnp.dot(a, b, precision=HI, preferred_element_type=F32)


def _split_bf16(x):
    hi = x.astype(BF16)
    return hi, (x - hi.astype(F32)).astype(BF16)


def _tril(n, k=0):
    r = lax.broadcasted_iota(jnp.int32, (n, n), 0)
    c = lax.broadcasted_iota(jnp.int32, (n, n), 1)
    return (c - r) <= k


def _mod_kernel(c_ref, w_ref, b_ref, o_ref):
    c = c_ref[...]
    o_ref[0] = _hdot(_silu(c), w_ref[0]) + b_ref[0]


def _mod_call(c_pad, ada_w, ada_b):
    depth, d, d6 = ada_w.shape
    nb = d6 // d
    return pl.pallas_call(
        _mod_kernel,
        grid=(depth, nb),
        in_specs=[
            pl.BlockSpec(c_pad.shape, lambda i, j: (0, 0)),
            pl.BlockSpec((1, d, d), lambda i, j: (i, 0, j)),
            pl.BlockSpec((1, 1, d), lambda i, j: (i, 0, j)),
        ],
        out_specs=pl.BlockSpec((1, c_pad.shape[0], d), lambda i, j: (i, 0, j)),
        out_shape=jax.ShapeDtypeStruct((depth, c_pad.shape[0], d6), F32),
        name="adaln_mod",
    )(c_pad, ada_w, ada_b.reshape(depth, 1, d6))


def _gla_kernel(x_ref, sh_ref, sc_ref, gt_ref, ng_ref, wqkvr_ref, wg1_ref, wg2_ref, bg2_ref,
                hg_ref, wout_ref, o_ref, st_ref, p_ref, la_ref, ob_ref, *, tb, dk, dv):
    hk = dk // GLA_HEADS
    hv = dv // GLA_HEADS
    nc = tb // CHUNK

    @pl.when(pl.program_id(1) == 0)
    def _():
        st_ref[...] = jnp.zeros_like(st_ref)

    x = x_ref[0]
    h = _norm_mod(x, ng_ref[...], sc_ref[0], sh_ref[0])
    hb = h.astype(BF16)
    p_ref[...] = jnp.dot(hb, wqkvr_ref[...], preferred_element_type=F32)
    g1 = jnp.dot(hb, wg1_ref[...], preferred_element_type=F32)
    g = jnp.dot(g1.astype(BF16), wg2_ref[...], preferred_element_type=F32) + bg2_ref[...]
    la_ref[...] = (jnp.minimum(g, 0.0) - jnp.log1p(jnp.exp(-jnp.abs(g)))) * (1.0 / GLA_TAU)

    tril = _tril(CHUNK).astype(F32)
    causal = _tril(CHUNK)
    scale = hk ** -0.5

    for c in range(nc):
        rows = slice(c * CHUNK, (c + 1) * CHUNK)
        cum = _hdot(tril, la_ref[rows, :])
        cl = cum[CHUNK - 1:CHUNK, :]
        e_pos = jnp.exp(cum)
        e_neg = jnp.exp(-cum)
        e_end = jnp.exp(cl - cum)
        dec = jnp.exp(cl)
        hr = range(GLA_HEADS)
        ksl = [slice(hd * hk, (hd + 1) * hk) for hd in hr]
        qis = [p_ref[rows, hd * hk:(hd + 1) * hk] * (scale * e_pos[:, ksl[hd]]) for hd in hr]
        kraw = [p_ref[rows, dk + hd * hk:dk + (hd + 1) * hk] for hd in hr]
        vs = [p_ref[rows, 2 * dk + hd * hv:2 * dk + (hd + 1) * hv] for hd in hr]
        attns = [jnp.where(causal, _bdot_nt(qis[hd], kraw[hd] * e_neg[:, ksl[hd]]), 0.0) for hd in hr]
        sts = [st_ref[hd] for hd in hr]
        inter = [_bdot_nt(qis[hd], sts[hd]) for hd in hr]
        upds = [_bdot(vs[hd].T, kraw[hd] * e_end[:, ksl[hd]]) for hd in hr]
        intra = [_bdot(attns[hd], vs[hd]) for hd in hr]
        for hd in hr:
            ob_ref[rows, hd * hv:(hd + 1) * hv] = intra[hd] + inter[hd]
            st_ref[hd] = dec[:, ksl[hd]] * sts[hd] + upds[hd]

    hg = hg_ref[...]
    parts = []
    for hd in range(GLA_HEADS):
        oh = ob_ref[:, hd * hv:(hd + 1) * hv]
        ms = jnp.mean(oh * oh, axis=-1, keepdims=True)
        r = p_ref[:, 2 * dk + dv + hd * hv:2 * dk + dv + (hd + 1) * hv]
        parts.append((oh * lax.rsqrt(ms + RMS_EPS) * hg * _silu(r)).astype(BF16))
    y = jnp.concatenate(parts, axis=-1)
    m = jnp.dot(y, wout_ref[...], preferred_element_type=F32)
    o_ref[0] = x + gt_ref[0] * m


def _gla_call(x, sh, sc, gt, ng, w_in, w_g2, b_g2, head_g, w_out, *, tb):
    b, s, d = x.shape
    dk = w_g2.shape[1]
    dv = w_out.shape[0]
    wqkvr = w_in[:, :2 * dk + 2 * dv].astype(BF16)
    wg1 = jnp.pad(w_in[:, 2 * dk + 2 * dv:], ((0, 0), (0, LANE - GLA_GATE_RANK))).astype(BF16)
    wg2 = jnp.pad(w_g2, ((0, LANE - GLA_GATE_RANK), (0, 0))).astype(BF16)
    hv = dv // GLA_HEADS
    hk = dk // GLA_HEADS
    const2 = lambda bi, i: (0, 0)
    vec = pl.BlockSpec((1, 1, d), lambda bi, i: (bi, 0, 0))
    kern = functools.partial(_gla_kernel, tb=tb, dk=dk, dv=dv)
    return pl.pallas_call(
        kern,
        grid=(b, s // tb),
        in_specs=[
            pl.BlockSpec((1, tb, d), lambda bi, i: (bi, i, 0)),
            vec, vec, vec,
            pl.BlockSpec((1, d), const2),
            pl.BlockSpec(wqkvr.shape, const2),
            pl.BlockSpec(wg1.shape, const2),
            pl.BlockSpec(wg2.shape, const2),
            pl.BlockSpec((1, dk), const2),
            pl.BlockSpec((1, hv), const2),
            pl.BlockSpec((dv, d), const2),
        ],
        out_specs=pl.BlockSpec((1, tb, d), lambda bi, i: (bi, i, 0)),
        out_shape=jax.ShapeDtypeStruct(x.shape, F32),
        scratch_shapes=[
            pltpu.VMEM((GLA_HEADS, hv, hk), F32),
            pltpu.VMEM((tb, 2 * dk + 2 * dv), F32),
            pltpu.VMEM((tb, dk), F32),
            pltpu.VMEM((tb, dv), F32),
        ],
        compiler_params=pltpu.CompilerParams(
            dimension_semantics=("arbitrary", "arbitrary"), vmem_limit_bytes=VMEM_LIMIT),
        name="gla_layer",
    )(x, sh, sc, gt, ng.reshape(1, d), wqkvr, wg1, wg2, b_g2.reshape(1, dk),
      head_g.reshape(1, hv), w_out.astype(BF16))


def _block_inverse_masks(n):
    r = lax.broadcasted_iota(jnp.int32, (n, n), 0)
    c = lax.broadcasted_iota(jnp.int32, (n, n), 1)
    eye = (r == c).astype(F32)
    first = (r // 2) == (c // 2)
    quads = []
    s = 2
    while s < n:
        quads.append(((r // (2 * s)) == (c // (2 * s))) & ((r // s) != (c // s)))
        s *= 2
    return eye, first, quads


def _unit_lower_inverse(a_list, masks):
    eye, first, quads = masks
    ds = [eye - jnp.where(first, a, 0.0) for a in a_list]
    for quad in quads:
        ms = [_bdot(d, jnp.where(quad, a, 0.0)) for d, a in zip(ds, a_list)]
        ds = [d - _bdot(m, d) for d, m in zip(ds, ms)]
    return ds


def _gdn_kernel(x_ref, sh_ref, sc_ref, gt_ref, ng_ref, wqkv_ref, wz_ref, wa_ref, wb_ref, cw_ref,
                alog_ref, dtb_ref, hg_ref, wout_ref, o_ref,
                s_ref, cb_ref, q_ref, k_ref, v_ref, z_ref, g_ref, bt_ref, ob_ref, *, tb):
    dqk = GDN_QK_HEADS * GDN_HEAD_DIM
    dvv = GDN_V_HEADS * GDN_HEAD_DIM
    hd_ = GDN_HEAD_DIM
    nc = tb // CHUNK
    rep = GDN_V_HEADS // GDN_QK_HEADS

    @pl.when(pl.program_id(1) == 0)
    def _():
        s_ref[...] = jnp.zeros_like(s_ref)
        cb_ref[0:SUBLANE, :] = jnp.zeros((SUBLANE, cb_ref.shape[1]), F32)

    x = x_ref[0]
    h = _norm_mod(x, ng_ref[...], sc_ref[0], sh_ref[0])
    hb = h.astype(BF16)
    cb_ref[SUBLANE:SUBLANE + tb, :] = jnp.dot(hb, wqkv_ref[...], preferred_element_type=F32)
    z_ref[...] = jnp.dot(hb, wz_ref[...], preferred_element_type=F32)
    a = jnp.dot(hb, wa_ref[...], preferred_element_type=F32)
    bb = jnp.dot(hb, wb_ref[...], preferred_element_type=F32)
    g_ref[...] = -jnp.exp(alog_ref[...]) * _softplus(a + dtb_ref[...])
    bt_ref[...] = _sigmoid(bb)

    off = SUBLANE - (GDN_CONV - 1)
    conv = cw_ref[0:1, :] * cb_ref[off:off + tb, :]
    for j in range(1, GDN_CONV):
        conv = conv + cw_ref[j:j + 1, :] * cb_ref[off + j:off + j + tb, :]
    cb_ref[0:SUBLANE, :] = cb_ref[tb:tb + SUBLANE, :]
    qkv = _silu(conv)
    qscale = hd_ ** -0.5
    for hq in range(GDN_QK_HEADS):
        sl = slice(hq * hd_, (hq + 1) * hd_)
        qh = qkv[:, hq * hd_:(hq + 1) * hd_]
        kh = qkv[:, dqk + hq * hd_:dqk + (hq + 1) * hd_]
        q_ref[:, sl] = qh * (lax.rsqrt(jnp.sum(qh * qh, axis=-1, keepdims=True) + 1e-6) * qscale)
        k_ref[:, sl] = kh * lax.rsqrt(jnp.sum(kh * kh, axis=-1, keepdims=True) + 1e-6)
    v_ref[...] = qkv[:, 2 * dqk:]

    tril = _tril(CHUNK).astype(F32)
    lower = _tril(CHUNK)
    strict = _tril(CHUNK, -1)
    inv_masks = _block_inverse_masks(CHUNK)

    for c in range(nc):
        rows = slice(c * CHUNK, (c + 1) * CHUNK)
        gc = _hdot(tril, g_ref[rows, :])
        gct = gc.T
        gl = gc[CHUNK - 1:CHUNK, :]
        eg = jnp.exp(gc)
        eendt = jnp.exp(gl - gc).T
        dec = jnp.exp(gl)
        bt = bt_ref[rows, :]
        beg = bt * eg
        hq_r = range(GDN_QK_HEADS)
        hv_r = range(GDN_V_HEADS)
        qs = [q_ref[rows, hq * hd_:(hq + 1) * hd_] for hq in hq_r]
        ks = [k_ref[rows, hq * hd_:(hq + 1) * hd_] for hq in hq_r]
        kks = [_bdot_nt(ks[hq], ks[hq]) for hq in hq_r]
        qk0s = [_bdot_nt(qs[hq], ks[hq]) for hq in hq_r]
        khts = [ks[hq].T for hq in hq_r]
        cols = [slice(hvi, hvi + 1) for hvi in hv_r]
        decays = []
        for hvi in hv_r:
            diff = gc[:, cols[hvi]] - gct[hvi:hvi + 1, :]
            decays.append(jnp.where(lower, jnp.exp(jnp.where(lower, diff, 0.0)), 0.0))
        amats = [jnp.where(strict, bt[:, cols[hvi]] * kks[hvi // rep] * decays[hvi], 0.0) for hvi in hv_r]
        tinvs = _unit_lower_inverse(amats, inv_masks)
        rhs = [jnp.concatenate([bt[:, cols[hvi]] * v_ref[rows, hvi * hd_:(hvi + 1) * hd_],
                                beg[:, cols[hvi]] * ks[hvi // rep]], axis=1) for hvi in hv_r]
        uws = [_bdot(tinvs[hvi], rhs[hvi]) for hvi in hv_r]
        ss = [s_ref[hvi] for hvi in hv_r]
        wss = [_bdot(uws[hvi][:, hd_:], ss[hvi]) for hvi in hv_r]
        qss = [_bdot(qs[hvi // rep] * eg[:, cols[hvi]], ss[hvi]) for hvi in hv_r]
        vns = [uws[hvi][:, :hd_] - wss[hvi] for hvi in hv_r]
        outs = [qss[hvi] + _bdot(qk0s[hvi // rep] * decays[hvi], vns[hvi]) for hvi in hv_r]
        upds = [_bdot(khts[hvi // rep] * eendt[hvi:hvi + 1, :], vns[hvi]) for hvi in hv_r]
        for hvi in hv_r:
            ob_ref[rows, hvi * hd_:(hvi + 1) * hd_] = outs[hvi]
            s_ref[hvi] = dec[:, cols[hvi]] * ss[hvi] + upds[hvi]

    hg = hg_ref[...]
    parts = []
    for hvi in range(GDN_V_HEADS):
        vs = slice(hvi * hd_, (hvi + 1) * hd_)
        oh = ob_ref[:, vs]
        ms = jnp.mean(oh * oh, axis=-1, keepdims=True)
        parts.append((oh * lax.rsqrt(ms + RMS_EPS) * hg * _silu(z_ref[:, vs])).astype(BF16))
    y = jnp.concatenate(parts, axis=-1)
    m = jnp.dot(y, wout_ref[...], preferred_element_type=F32)
    o_ref[0] = x + gt_ref[0] * m


def _gdn_call(x, sh, sc, gt, ng, w_in, conv_w, a_log, dt_bias, head_g, w_out, *, tb):
    b, s, d = x.shape
    dqk = GDN_QK_HEADS * GDN_HEAD_DIM
    dvv = GDN_V_HEADS * GDN_HEAD_DIM
    nqkv = 2 * dqk + dvv
    wqkv = w_in[:, :nqkv].astype(BF16)
    wz = w_in[:, nqkv:nqkv + dvv].astype(BF16)
    padh = ((0, 0), (0, LANE - GDN_V_HEADS))
    wa = jnp.pad(w_in[:, nqkv + dvv:nqkv + dvv + GDN_V_HEADS], padh).astype(BF16)
    wb = jnp.pad(w_in[:, nqkv + dvv + GDN_V_HEADS:], padh).astype(BF16)
    alog = jnp.pad(a_log.reshape(1, -1), padh)
    dtb = jnp.pad(dt_bias.reshape(1, -1), padh)
    const2 = lambda bi, i: (0, 0)
    vec = pl.BlockSpec((1, 1, d), lambda bi, i: (bi, 0, 0))
    kern = functools.partial(_gdn_kernel, tb=tb)
    return pl.pallas_call(
        kern,
        grid=(b, s // tb),
        in_specs=[
            pl.BlockSpec((1, tb, d), lambda bi, i: (bi, i, 0)),
            vec, vec, vec,
            pl.BlockSpec((1, d), const2),
            pl.BlockSpec(wqkv.shape, const2),
            pl.BlockSpec(wz.shape, const2),
            pl.BlockSpec(wa.shape, const2),
            pl.BlockSpec(wb.shape, const2),
            pl.BlockSpec(conv_w.shape, const2),
            pl.BlockSpec((1, LANE), const2),
            pl.BlockSpec((1, LANE), const2),
            pl.BlockSpec((1, GDN_HEAD_DIM), const2),
            pl.BlockSpec((dvv, d), const2),
        ],
        out_specs=pl.BlockSpec((1, tb, d), lambda bi, i: (bi, i, 0)),
        out_shape=jax.ShapeDtypeStruct(x.shape, F32),
        scratch_shapes=[
            pltpu.VMEM((GDN_V_HEADS, GDN_HEAD_DIM, GDN_HEAD_DIM), F32),
            pltpu.VMEM((tb + SUBLANE, nqkv), F32),
            pltpu.VMEM((tb, dqk), F32),
            pltpu.VMEM((tb, dqk), F32),
            pltpu.VMEM((tb, dvv), F32),
            pltpu.VMEM((tb, dvv), F32),
            pltpu.VMEM((tb, LANE), F32),
            pltpu.VMEM((tb, LANE), F32),
            pltpu.VMEM((tb, dvv), F32),
        ],
        compiler_params=pltpu.CompilerParams(
            dimension_semantics=("arbitrary", "arbitrary"), vmem_limit_bytes=VMEM_LIMIT),
        name="gdn_layer",
    )(x, sh, sc, gt, ng.reshape(1, d), wqkv, wz, wa, wb, conv_w, alog, dtb,
      head_g.reshape(1, GDN_HEAD_DIM), w_out.astype(BF16))


def _top_values(x, n, with_rank=False):
    vals = []
    rank = jnp.full(x.shape, float(n), F32) if with_rank else None
    for r in range(n):
        m = jnp.max(x, axis=0, keepdims=True)
        vals.append(m)
        if with_rank:
            rank = jnp.where(x == m, float(r), rank)
        if r + 1 < n:
            x = jnp.where(x == m, -jnp.inf, x)
    return (vals, rank) if with_rank else vals


def _pair_pack(x):
    bits = lax.bitcast_convert_type(x.astype(BF16).astype(F32), jnp.int32)
    return bits | lax.shift_right_logical(bits, jnp.int32(16))


def _stack_rows(rows):
    l = rows[0].shape[1]
    sub = lax.broadcasted_iota(jnp.int32, (SUBLANE, l), 0)
    out = jnp.broadcast_to(rows[0], (SUBLANE, l))
    for i in range(1, len(rows)):
        out = jnp.where(sub == i, rows[i], out)
    return out


def _peer_kernel(x_ref, sh_ref, sc_ref, gt_ref, ng_ref, fg_ref, wq_ref, k1_ref, k2_ref, u_ref, vt_ref, o_ref,
                 tt_ref, s1_ref, s2_ref, r1_ref, w1_ref, jb_ref, e2_ref, yt_ref, *half_refs, tt, eb, ne, final):
    g = pl.program_id(2)
    nh = tt // PEER_NCOLS
    ht_refs = half_refs[:nh]
    at_refs = half_refs[nh:]
    nj = tt // LANE
    ab = eb // PEER_NKEYS
    k = PEER_TOPK

    @pl.when(g == 0)
    def _():
        x = x_ref[0]
        t = _norm_mod(x, ng_ref[...], sc_ref[0], sh_ref[0])
        tb = t.astype(BF16)
        tt_ref[...] = t.T.astype(BF16)
        q = jnp.dot(tb, wq_ref[...], preferred_element_type=F32)
        q_hi, q_lo = _split_bf16(q)
        k_parts = [_split_bf16(k1_ref[...]), _split_bf16(k2_ref[...])]
        for h in range(PEER_HEADS):
            for side, s_ref in enumerate((s1_ref, s2_ref)):
                qs = slice((2 * h + side) * PEER_HALF, (2 * h + side + 1) * PEER_HALF)
                k_hi, k_lo = k_parts[side]
                s_ref[h] = (lax.dot_general(k_hi, q_hi[:, qs], NT, preferred_element_type=F32)
                            + lax.dot_general(k_hi, q_lo[:, qs], NT, preferred_element_type=F32)
                            + lax.dot_general(k_lo, q_hi[:, qs], NT, preferred_element_type=F32))
        yt_ref[...] = jnp.zeros_like(yt_ref)

        def route_one(idx):
            h = idx // nj
            c0 = pl.multiple_of((idx % nj) * LANE, LANE)
            cols = pl.ds(c0, LANE)
            x1 = s1_ref[h, :, cols]
            x2 = s2_ref[h, :, cols]
            v1, rank1 = _top_values(x1, k + 1, with_rank=True)
            v2 = _top_values(x2, k + 1)
            v2lo = _stack_rows(v2[:SUBLANE])
            v2hi = _stack_rows(v2[SUBLANE:2 * SUBLANE])
            v1hi = _stack_rows(v1[SUBLANE:2 * SUBLANE])
            sub = lax.broadcasted_iota(jnp.int32, (SUBLANE, LANE), 0)
            last = jnp.where(sub == 0, v1[0] + v2[k], jnp.where(sub == 1, v1[k] + v2[0], -jnp.inf))
            tiles = [v1[0] + v2lo, v1[0] + v2hi, v1hi + v2[0], last]
            tiles += [v1[i] + v2lo for i in range(1, SUBLANE)]
            cand = jnp.concatenate(tiles, axis=0)
            tops = _top_values(cand, k + 1)
            kth = tops[k - 1]
            cut = 0.5 * (kth + tops[k])
            top = v1[0] + v2[0]
            z = jnp.sum(jnp.where(cand >= kth, jnp.exp(cand - top), 0.0), axis=0, keepdims=True)
            c2 = cut - x2
            passing = jnp.zeros_like(c2)
            for i in range(k + 1):
                passing = jnp.where(v1[i] >= c2, float(i + 1), passing)
            r1_ref[h, :, cols] = _pair_pack(rank1)
            w1_ref[h, :, cols] = _pair_pack(jnp.exp(x1 - v1[0]) / z)
            jb_ref[h, :, cols] = passing.astype(BF16)
            e2_ref[h, :, cols] = jnp.exp(x2 - v2[0]).astype(BF16)

        def route(i, carry):
            route_one(2 * i)
            route_one(2 * i + 1)
            return carry

        lax.fori_loop(0, PEER_HEADS * nj // 2, route, 0)

    em = jnp.clip(g - 1, 0, ne - 1)
    arows = pl.ds(pl.multiple_of(em * ab, SUBLANE), SUBLANE)
    bh = PEER_NKEYS // 2
    d = yt_ref.shape[0]
    ngroup = PEER_NCOLS // LANE

    def gate_block(n, jj, ag, hb):
        cols = slice(n * PEER_NCOLS + jj * LANE, n * PEER_NCOLS + (jj + 1) * LANE)
        lcols = slice(jj * LANE, (jj + 1) * LANE)
        brows = slice(hb * bh, (hb + 1) * bh)
        als = tuple(range(ag * PEER_AROWS, (ag + 1) * PEER_AROWS))
        r1t = [r1_ref[h, arows, cols] for h in range(PEER_HEADS)]
        w1t = [w1_ref[h, arows, cols] for h in range(PEER_HEADS)]

        def row(tile, al):
            return pltpu.bitcast(jnp.broadcast_to(tile[al:al + 1, :], (SUBLANE, LANE)), BF16)

        nv = bh // PACK
        accs = [[None] * nv for _ in als]
        for h in range(PEER_HEADS):
            jb = jb_ref[h, brows, cols]
            e2 = e2_ref[h, brows, cols]
            r1rows = [row(r1t[h], al) for al in als]
            w1rows = [row(w1t[h], al) for al in als]
            for v in range(nv):
                vr = slice(v * PACK, (v + 1) * PACK)
                jbv = jb[vr]
                e2v = e2[vr]
                for i in range(len(als)):
                    term = jnp.where(r1rows[i] < jbv, e2v * w1rows[i], jnp.zeros_like(e2v))
                    accs[i][v] = term if h == 0 else accs[i][v] + term
        fold = None
        for i, al in enumerate(als):
            for v in range(nv):
                r0 = al * PEER_NKEYS + hb * bh + v * PACK
                out = accs[i][v] * _gelu_tanh(ht_refs[n][r0:r0 + PACK, lcols].astype(BF16))
                at_refs[n][r0:r0 + PACK, lcols] = out
                bits = pltpu.bitcast(out, jnp.int32)
                fold = bits if fold is None else fold | bits
        return fold & jnp.minimum(g, 0)

    def after(zero, rhs):
        head = pltpu.bitcast(rhs[:PEER_KTILE], jnp.int32)
        head = head | jnp.tile(zero, (head.shape[0] // SUBLANE, head.shape[1] // LANE))
        return jnp.concatenate([pltpu.bitcast(head, BF16), rhs[PEER_KTILE:]], axis=0)

    def matmul_pieces(n, second_mm, first_mm):
        ncols = slice(n * PEER_NCOLS, (n + 1) * PEER_NCOLS)
        pieces = []
        for m in range(d // PEER_MROWS):
            mrows = slice(m * PEER_MROWS, (m + 1) * PEER_MROWS)

            def second(zero, mrows=mrows):
                yt_ref[mrows, ncols] += jnp.dot(vt_ref[0, mrows, :], after(zero, at_refs[n][...]),
                                                preferred_element_type=F32)

            def first(zero, mrows=mrows):
                ht_refs[n][mrows, :] = jnp.dot(u_ref[mrows, :], after(zero, tt_ref[:, ncols]),
                                               preferred_element_type=F32)
            pieces += ([second] if second_mm else []) + ([first] if first_mm else [])
        return pieces

    def run_step(gate, second_mm, first_mm):
        pending = []
        zero = lax.broadcasted_iota(jnp.int32, (SUBLANE, LANE), 0) & jnp.minimum(g, 0)
        for n in range(tt // PEER_NCOLS):
            blk = 0
            if gate:
                for jj in range(ngroup):
                    for ag in range(ab // PEER_AROWS):
                        for hb in range(2):
                            zero = gate_block(n, jj, ag, hb)
                            if pending and blk % PEER_GATE_PER_PIECE == 1:
                                pending.pop(0)(zero)
                            blk += 1
            for piece in pending:
                piece(zero)
            pending = matmul_pieces(n, second_mm, first_mm)
        for piece in pending:
            piece(zero)

    @pl.when(g == 0)
    def _():
        run_step(False, False, True)

    @pl.when(jnp.logical_and(g > 0, g < ne))
    def _():
        run_step(True, True, True)

    @pl.when(g == ne)
    def _():
        run_step(True, True, False)
        o = x_ref[0] + gt_ref[0] * yt_ref[...].T
        if final:
            o = o * lax.rsqrt(jnp.mean(o * o, axis=-1, keepdims=True) + RMS_EPS) * fg_ref[...]
        o_ref[0] = o


def _peer_call(x, sh, sc, gt, ng, fg, wq, k1, k2, u, v, *, tt, eb, final):
    b, s, d = x.shape
    assert eb == SUBLANE * PEER_NKEYS and u.shape[0] % eb == 0
    ne = u.shape[0] // eb
    ub = u.astype(BF16)
    vtb = v.astype(BF16).reshape(ne, eb, d).transpose(0, 2, 1)
    const3 = lambda bi, i, e: (0, 0)
    vec = pl.BlockSpec((1, 1, d), lambda bi, i, e: (bi, 0, 0))
    kern = functools.partial(_peer_kernel, tt=tt, eb=eb, ne=ne, final=final)
    sshape = (PEER_HEADS, PEER_NKEYS, tt)
    return pl.pallas_call(
        kern,
        grid=(b, s // tt, ne + 1),
        in_specs=[
            pl.BlockSpec((1, tt, d), lambda bi, i, e: (bi, i, 0)),
            vec, vec, vec,
            pl.BlockSpec((1, d), const3),
            pl.BlockSpec((1, d), const3),
            pl.BlockSpec(wq.shape, const3),
            pl.BlockSpec(k1.shape, const3),
            pl.BlockSpec(k2.shape, const3),
            pl.BlockSpec((eb, d), lambda bi, i, g: (jnp.minimum(g, ne - 1), 0)),
            pl.BlockSpec((1, d, eb), lambda bi, i, g: (jnp.maximum(g - 1, 0), 0, 0)),
        ],
        out_specs=pl.BlockSpec((1, tt, d), lambda bi, i, e: (bi, i, 0)),
        out_shape=jax.ShapeDtypeStruct(x.shape, F32),
        scratch_shapes=[
            pltpu.VMEM((d, tt), BF16),
            pltpu.VMEM(sshape, F32),
            pltpu.VMEM(sshape, F32),
            pltpu.VMEM(sshape, jnp.int32),
            pltpu.VMEM(sshape, jnp.int32),
            pltpu.VMEM(sshape, BF16),
            pltpu.VMEM(sshape, BF16),
            pltpu.VMEM((d, tt), F32),
        ] + [pltpu.VMEM((eb, PEER_NCOLS), F32)] * (tt // PEER_NCOLS)
          + [pltpu.VMEM((eb, PEER_NCOLS), BF16)] * (tt // PEER_NCOLS),
        compiler_params=pltpu.CompilerParams(
            dimension_semantics=("arbitrary", "arbitrary", "arbitrary"), vmem_limit_bytes=VMEM_LIMIT),
        name="peer_layer",
    )(x, sh, sc, gt, ng.reshape(1, d), fg.reshape(1, d), wq.astype(BF16), k1, k2, ub, vtb)


def _pick(n, pref):
    return pref if n % pref == 0 else n


def kernel(x, c, ada_w, ada_b, norm_mix_g, norm_ffn_g, gla_w_in, gla_w_g2, gla_b_g2, gla_norm_g,
           gla_w_out, gdn_w_in, gdn_conv_w, gdn_a_log, gdn_dt_bias, gdn_norm_g, gdn_w_out,
           peer_wq, peer_k1, peer_k2, peer_u, peer_v, final_norm_g):
    b, s, d = x.shape
    depth = ada_w.shape[0]
    c_pad = jnp.pad(c, ((0, SUBLANE - b % SUBLANE if b % SUBLANE else 0), (0, 0)))
    mod = _mod_call(c_pad, ada_w, ada_b)[:, :b, :].reshape(depth, b, 6, 1, d)
    tb_gla = _pick(s, 512)
    tb_mix = _pick(s, 256)
    tt = _pick(s, 512)
    eb = 1024
    for i in range(depth):
        sh1, sc1, gt1, sh2, sc2, gt2 = (mod[i, :, j] for j in range(6))
        j = i // 2
        if i % 2 == 0:
            x = _gla_call(x, sh1, sc1, gt1, norm_mix_g[i], gla_w_in[j], gla_w_g2[j], gla_b_g2[j],
                          gla_norm_g[j], gla_w_out[j], tb=tb_gla)
        else:
            x = _gdn_call(x, sh1, sc1, gt1, norm_mix_g[i], gdn_w_in[j], gdn_conv_w[j], gdn_a_log[j],
                          gdn_dt_bias[j], gdn_norm_g[j], gdn_w_out[j], tb=tb_mix)
        x = _peer_call(x, sh2, sc2, gt2, norm_ffn_g[i], final_norm_g, peer_wq[i], peer_k1[i], peer_k2[i],
                       peer_u[i], peer_v[i], tt=tt, eb=eb, final=(i == depth - 1))
    return x
```

```python
import functools

import jax
import jax.numpy as jnp
from jax import lax
from jax.experimental import pallas as pl
from jax.experimental.pallas import tpu as pltpu

F32 = jnp.float32
BF16 = jnp.bfloat16
HI = lax.Precision.HIGHEST

RMS_EPS = 1e-6
CHUNK = 64

GLA_HEADS = 4
GLA_GATE_RANK = 16
GLA_TAU = 16.0

GDN_QK_HEADS = 8
GDN_V_HEADS = 16
GDN_HEAD_DIM = 128
GDN_CONV = 4

PEER_HEADS = 8
PEER_TOPK = 16
PEER_NKEYS = 128
PEER_HALF = 128
PEER_NCOLS = 256
PEER_KTILE = 256
PEER_MROWS = 1024
PEER_AROWS = 4
PEER_GATE_PER_PIECE = 4
PEER_ROUTE_UNROLL = 4

LANE = 128
SUBLANE = 8
PACK = 16
VMEM_LIMIT = 56 * 1024 * 1024

NT = (((1,), (1,)), ((), ()))


def _sigmoid(x):
    return 1.0 / (1.0 + jnp.exp(-x))


def _silu(x):
    return x * _sigmoid(x)


def _softplus(x):
    return jnp.maximum(x, 0.0) + jnp.log1p(jnp.exp(-jnp.abs(x)))


def _gelu_tanh(x):
    c = 0.7978845608028654
    hx = 0.5 * x
    return hx + hx * jnp.tanh(x * (c + (c * 0.044715) * (x * x)))


def _norm_mod(x, g, sc, sh):
    ms = jnp.mean(x * x, axis=-1, keepdims=True)
    return (x * lax.rsqrt(ms + RMS_EPS) * g) * (1.0 + sc) + sh


def _bdot(a, b):
    return jnp.dot(a.astype(BF16), b.astype(BF16), preferred_element_type=F32)


def _bdot_nt(a, b):
    return lax.dot_general(a.astype(BF16), b.astype(BF16), NT, preferred_element_type=F32)


def _hdot(a, b):
    return jnp.dot(a, b, precision=HI, preferred_element_type=F32)


def _split_bf16(x):
    hi = x.astype(BF16)
    return hi, (x - hi.astype(F32)).astype(BF16)


def _tril(n, k=0):
    r = lax.broadcasted_iota(jnp.int32, (n, n), 0)
    c = lax.broadcasted_iota(jnp.int32, (n, n), 1)
    return (c - r) <= k


def _mod_kernel(c_ref, w_ref, b_ref, o_ref):
    c = c_ref[...]
    o_ref[0] = _hdot(_silu(c), w_ref[0]) + b_ref[0]


def _mod_call(c_pad, ada_w, ada_b):
    depth, d, d6 = ada_w.shape
    nb = d6 // d
    return pl.pallas_call(
        _mod_kernel,
        grid=(depth, nb),
        in_specs=[
            pl.BlockSpec(c_pad.shape, lambda i, j: (0, 0)),
            pl.BlockSpec((1, d, d), lambda i, j: (i, 0, j)),
            pl.BlockSpec((1, 1, d), lambda i, j: (i, 0, j)),
        ],
        out_specs=pl.BlockSpec((1, c_pad.shape[0], d), lambda i, j: (i, 0, j)),
        out_shape=jax.ShapeDtypeStruct((depth, c_pad.shape[0], d6), F32),
        name="adaln_mod",
    )(c_pad, ada_w, ada_b.reshape(depth, 1, d6))


def _gla_kernel(x_ref, sh_ref, sc_ref, gt_ref, ng_ref, wqkvr_ref, wg1_ref, wg2_ref, bg2_ref,
                hg_ref, wout_ref, o_ref, st_ref, p_ref, la_ref, ob_ref, *, tb, dk, dv):
    hk = dk // GLA_HEADS
    hv = dv // GLA_HEADS
    nc = tb // CHUNK

    @pl.when(pl.program_id(1) == 0)
    def _():
        st_ref[...] = jnp.zeros_like(st_ref)

    x = x_ref[0]
    h = _norm_mod(x, ng_ref[...], sc_ref[0], sh_ref[0])
    hb = h.astype(BF16)
    p_ref[...] = jnp.dot(hb, wqkvr_ref[...], preferred_element_type=F32)
    g1 = jnp.dot(hb, wg1_ref[...], preferred_element_type=F32)
    g = jnp.dot(g1.astype(BF16), wg2_ref[...], preferred_element_type=F32) + bg2_ref[...]
    la_ref[...] = (jnp.minimum(g, 0.0) - jnp.log1p(jnp.exp(-jnp.abs(g)))) * (1.0 / GLA_TAU)

    tril = _tril(CHUNK).astype(F32)
    causal = _tril(CHUNK)
    scale = hk ** -0.5

    for c in range(nc):
        rows = slice(c * CHUNK, (c + 1) * CHUNK)
        cum = _hdot(tril, la_ref[rows, :])
        cl = cum[CHUNK - 1:CHUNK, :]
        e_pos = jnp.exp(cum)
        e_neg = jnp.exp(-cum)
        e_end = jnp.exp(cl - cum)
        dec = jnp.exp(cl)
        hr = range(GLA_HEADS)
        ksl = [slice(hd * hk, (hd + 1) * hk) for hd in hr]
        qis = [p_ref[rows, hd * hk:(hd + 1) * hk] * (scale * e_pos[:, ksl[hd]]) for hd in hr]
        kraw = [p_ref[rows, dk + hd * hk:dk + (hd + 1) * hk] for hd in hr]
        vs = [p_ref[rows, 2 * dk + hd * hv:2 * dk + (hd + 1) * hv] for hd in hr]
        attns = [jnp.where(causal, _bdot_nt(qis[hd], kraw[hd] * e_neg[:, ksl[hd]]), 0.0) for hd in hr]
        sts = [st_ref[hd] for hd in hr]
        inter = [_bdot_nt(qis[hd], sts[hd]) for hd in hr]
        upds = [_bdot(vs[hd].T, kraw[hd] * e_end[:, ksl[hd]]) for hd in hr]
        intra = [_bdot(attns[hd], vs[hd]) for hd in hr]
        for hd in hr:
            ob_ref[rows, hd * hv:(hd + 1) * hv] = intra[hd] + inter[hd]
            st_ref[hd] = dec[:, ksl[hd]] * sts[hd] + upds[hd]

    hg = hg_ref[...]
    parts = []
    for hd in range(GLA_HEADS):
        oh = ob_ref[:, hd * hv:(hd + 1) * hv]
        ms = jnp.mean(oh * oh, axis=-1, keepdims=True)
        r = p_ref[:, 2 * dk + dv + hd * hv:2 * dk + dv + (hd + 1) * hv]
        parts.append((oh * lax.rsqrt(ms + RMS_EPS) * hg * _silu(r)).astype(BF16))
    y = jnp.concatenate(parts, axis=-1)
    m = jnp.dot(y, wout_ref[...], preferred_element_type=F32)
    o_ref[0] = x + gt_ref[0] * m


def _gla_call(x, sh, sc, gt, ng, w_in, w_g2, b_g2, head_g, w_out, *, tb):
    b, s, d = x.shape
    dk = w_g2.shape[1]
    dv = w_out.shape[0]
    wqkvr = w_in[:, :2 * dk + 2 * dv].astype(BF16)
    wg1 = jnp.pad(w_in[:, 2 * dk + 2 * dv:], ((0, 0), (0, LANE - GLA_GATE_RANK))).astype(BF16)
    wg2 = jnp.pad(w_g2, ((0, LANE - GLA_GATE_RANK), (0, 0))).astype(BF16)
    hv = dv // GLA_HEADS
    hk = dk // GLA_HEADS
    const2 = lambda bi, i: (0, 0)
    vec = pl.BlockSpec((1, 1, d), lambda bi, i: (bi, 0, 0))
    kern = functools.partial(_gla_kernel, tb=tb, dk=dk, dv=dv)
    return pl.pallas_call(
        kern,
        grid=(b, s // tb),
        in_specs=[
            pl.BlockSpec((1, tb, d), lambda bi, i: (bi, i, 0)),
            vec, vec, vec,
            pl.BlockSpec((1, d), const2),
            pl.BlockSpec(wqkvr.shape, const2),
            pl.BlockSpec(wg1.shape, const2),
            pl.BlockSpec(wg2.shape, const2),
            pl.BlockSpec((1, dk), const2),
            pl.BlockSpec((1, hv), const2),
            pl.BlockSpec((dv, d), const2),
        ],
        out_specs=pl.BlockSpec((1, tb, d), lambda bi, i: (bi, i, 0)),
        out_shape=jax.ShapeDtypeStruct(x.shape, F32),
        scratch_shapes=[
            pltpu.VMEM((GLA_HEADS, hv, hk), F32),
            pltpu.VMEM((tb, 2 * dk + 2 * dv), F32),
            pltpu.VMEM((tb, dk), F32),
            pltpu.VMEM((tb, dv), F32),
        ],
        compiler_params=pltpu.CompilerParams(
            dimension_semantics=("arbitrary", "arbitrary"), vmem_limit_bytes=VMEM_LIMIT),
        name="gla_layer",
    )(x, sh, sc, gt, ng.reshape(1, d), wqkvr, wg1, wg2, b_g2.reshape(1, dk),
      head_g.reshape(1, hv), w_out.astype(BF16))


def _block_inverse_masks(n):
    r = lax.broadcasted_iota(jnp.int32, (n, n), 0)
    c = lax.broadcasted_iota(jnp.int32, (n, n), 1)
    eye = (r == c).astype(F32)
    first = (r // 2) == (c // 2)
    quads = []
    s = 2
    while s < n:
        quads.append(((r // (2 * s)) == (c // (2 * s))) & ((r // s) != (c // s)))
        s *= 2
    return eye, first, quads


def _unit_lower_inverse(a_list, masks):
    eye, first, quads = masks
    ds = [eye - jnp.where(first, a, 0.0) for a in a_list]
    for quad in quads:
        ms = [_bdot(d, jnp.where(quad, a, 0.0)) for d, a in zip(ds, a_list)]
        ds = [d - _bdot(m, d) for d, m in zip(ds, ms)]
    return ds


def _gdn_kernel(x_ref, sh_ref, sc_ref, gt_ref, ng_ref, wqkv_ref, wz_ref, wa_ref, wb_ref, cw_ref,
                alog_ref, dtb_ref, hg_ref, wout_ref, o_ref,
                s_ref, cb_ref, q_ref, k_ref, v_ref, z_ref, g_ref, bt_ref, ob_ref, *, tb):
    dqk = GDN_QK_HEADS * GDN_HEAD_DIM
    dvv = GDN_V_HEADS * GDN_HEAD_DIM
    hd_ = GDN_HEAD_DIM
    nc = tb // CHUNK
    rep = GDN_V_HEADS // GDN_QK_HEADS

    @pl.when(pl.program_id(1) == 0)
    def _():
        s_ref[...] = jnp.zeros_like(s_ref)
        cb_ref[0:SUBLANE, :] = jnp.zeros((SUBLANE, cb_ref.shape[1]), F32)

    x = x_ref[0]
    h = _norm_mod(x, ng_ref[...], sc_ref[0], sh_ref[0])
    hb = h.astype(BF16)
    cb_ref[SUBLANE:SUBLANE + tb, :] = jnp.dot(hb, wqkv_ref[...], preferred_element_type=F32)
    z_ref[...] = jnp.dot(hb, wz_ref[...], preferred_element_type=F32)
    a = jnp.dot(hb, wa_ref[...], preferred_element_type=F32)
    bb = jnp.dot(hb, wb_ref[...], preferred_element_type=F32)
    g_ref[...] = -jnp.exp(alog_ref[...]) * _softplus(a + dtb_ref[...])
    bt_ref[...] = _sigmoid(bb)

    off = SUBLANE - (GDN_CONV - 1)
    conv = cw_ref[0:1, :] * cb_ref[off:off + tb, :]
    for j in range(1, GDN_CONV):
        conv = conv + cw_ref[j:j + 1, :] * cb_ref[off + j:off + j + tb, :]
    cb_ref[0:SUBLANE, :] = cb_ref[tb:tb + SUBLANE, :]
    qkv = _silu(conv)
    qscale = hd_ ** -0.5
    for hq in range(GDN_QK_HEADS):
        sl = slice(hq * hd_, (hq + 1) * hd_)
        qh = qkv[:, hq * hd_:(hq + 1) * hd_]
        kh = qkv[:, dqk + hq * hd_:dqk + (hq + 1) * hd_]
        q_ref[:, sl] = qh * (lax.rsqrt(jnp.sum(qh * qh, axis=-1, keepdims=True) + 1e-6) * qscale)
        k_ref[:, sl] = kh * lax.rsqrt(jnp.sum(kh * kh, axis=-1, keepdims=True) + 1e-6)
    v_ref[...] = qkv[:, 2 * dqk:]

    tril = _tril(CHUNK).astype(F32)
    lower = _tril(CHUNK)
    strict = _tril(CHUNK, -1)
    inv_masks = _block_inverse_masks(CHUNK)

    for c in range(nc):
        rows = slice(c * CHUNK, (c + 1) * CHUNK)
        gc = _hdot(tril, g_ref[rows, :])
        gct = gc.T
        gl = gc[CHUNK - 1:CHUNK, :]
        eg = jnp.exp(gc)
        eendt = jnp.exp(gl - gc).T
        dec = jnp.exp(gl)
        bt = bt_ref[rows, :]
        beg = bt * eg
        hq_r = range(GDN_QK_HEADS)
        hv_r = range(GDN_V_HEADS)
        qs = [q_ref[rows, hq * hd_:(hq + 1) * hd_] for hq in hq_r]
        ks = [k_ref[rows, hq * hd_:(hq + 1) * hd_] for hq in hq_r]
        kks = [_bdot_nt(ks[hq], ks[hq]) for hq in hq_r]
        qk0s = [_bdot_nt(qs[hq], ks[hq]) for hq in hq_r]
        khts = [ks[hq].T for hq in hq_r]
        cols = [slice(hvi, hvi + 1) for hvi in hv_r]
        decays = []
        for hvi in hv_r:
            diff = gc[:, cols[hvi]] - gct[hvi:hvi + 1, :]
            decays.append(jnp.where(lower, jnp.exp(jnp.where(lower, diff, 0.0)), 0.0))
        amats = [jnp.where(strict, bt[:, cols[hvi]] * kks[hvi // rep] * decays[hvi], 0.0) for hvi in hv_r]
        tinvs = _unit_lower_inverse(amats, inv_masks)
        rhs = [jnp.concatenate([bt[:, cols[hvi]] * v_ref[rows, hvi * hd_:(hvi + 1) * hd_],
                                beg[:, cols[hvi]] * ks[hvi // rep]], axis=1) for hvi in hv_r]
        uws = [_bdot(tinvs[hvi], rhs[hvi]) for hvi in hv_r]
        ss = [s_ref[hvi] for hvi in hv_r]
        wss = [_bdot(uws[hvi][:, hd_:], ss[hvi]) for hvi in hv_r]
        qss = [_bdot(qs[hvi // rep] * eg[:, cols[hvi]], ss[hvi]) for hvi in hv_r]
        vns = [uws[hvi][:, :hd_] - wss[hvi] for hvi in hv_r]
        outs = [qss[hvi] + _bdot(qk0s[hvi // rep] * decays[hvi], vns[hvi]) for hvi in hv_r]
        upds = [_bdot(khts[hvi // rep] * eendt[hvi:hvi + 1, :], vns[hvi]) for hvi in hv_r]
        for hvi in hv_r:
            ob_ref[rows, hvi * hd_:(hvi + 1) * hd_] = outs[hvi]
            s_ref[hvi] = dec[:, cols[hvi]] * ss[hvi] + upds[hvi]

    hg = hg_ref[...]
    parts = []
    for hvi in range(GDN_V_HEADS):
        vs = slice(hvi * hd_, (hvi + 1) * hd_)
        oh = ob_ref[:, vs]
        ms = jnp.mean(oh * oh, axis=-1, keepdims=True)
        parts.append((oh * lax.rsqrt(ms + RMS_EPS) * hg * _silu(z_ref[:, vs])).astype(BF16))
    y = jnp.concatenate(parts, axis=-1)
    m = jnp.dot(y, wout_ref[...], preferred_element_type=F32)
    o_ref[0] = x + gt_ref[0] * m


def _gdn_call(x, sh, sc, gt, ng, w_in, conv_w, a_log, dt_bias, head_g, w_out, *, tb):
    b, s, d = x.shape
    dqk = GDN_QK_HEADS * GDN_HEAD_DIM
    dvv = GDN_V_HEADS * GDN_HEAD_DIM
    nqkv = 2 * dqk + dvv
    wqkv = w_in[:, :nqkv].astype(BF16)
    wz = w_in[:, nqkv:nqkv + dvv].astype(BF16)
    padh = ((0, 0), (0, LANE - GDN_V_HEADS))
    wa = jnp.pad(w_in[:, nqkv + dvv:nqkv + dvv + GDN_V_HEADS], padh).astype(BF16)
    wb = jnp.pad(w_in[:, nqkv + dvv + GDN_V_HEADS:], padh).astype(BF16)
    alog = jnp.pad(a_log.reshape(1, -1), padh)
    dtb = jnp.pad(dt_bias.reshape(1, -1), padh)
    const2 = lambda bi, i: (0, 0)
    vec = pl.BlockSpec((1, 1, d), lambda bi, i: (bi, 0, 0))
    kern = functools.partial(_gdn_kernel, tb=tb)
    return pl.pallas_call(
        kern,
        grid=(b, s // tb),
        in_specs=[
            pl.BlockSpec((1, tb, d), lambda bi, i: (bi, i, 0)),
            vec, vec, vec,
            pl.BlockSpec((1, d), const2),
            pl.BlockSpec(wqkv.shape, const2),
            pl.BlockSpec(wz.shape, const2),
            pl.BlockSpec(wa.shape, const2),
            pl.BlockSpec(wb.shape, const2),
            pl.BlockSpec(conv_w.shape, const2),
            pl.BlockSpec((1, LANE), const2),
            pl.BlockSpec((1, LANE), const2),
            pl.BlockSpec((1, GDN_HEAD_DIM), const2),
            pl.BlockSpec((dvv, d), const2),
        ],
        out_specs=pl.BlockSpec((1, tb, d), lambda bi, i: (bi, i, 0)),
        out_shape=jax.ShapeDtypeStruct(x.shape, F32),
        scratch_shapes=[
            pltpu.VMEM((GDN_V_HEADS, GDN_HEAD_DIM, GDN_HEAD_DIM), F32),
            pltpu.VMEM((tb + SUBLANE, nqkv), F32),
            pltpu.VMEM((tb, dqk), F32),
            pltpu.VMEM((tb, dqk), F32),
            pltpu.VMEM((tb, dvv), F32),
            pltpu.VMEM((tb, dvv), F32),
            pltpu.VMEM((tb, LANE), F32),
            pltpu.VMEM((tb, LANE), F32),
            pltpu.VMEM((tb, dvv), F32),
        ],
        compiler_params=pltpu.CompilerParams(
            dimension_semantics=("arbitrary", "arbitrary"), vmem_limit_bytes=VMEM_LIMIT),
        name="gdn_layer",
    )(x, sh, sc, gt, ng.reshape(1, d), wqkv, wz, wa, wb, conv_w, alog, dtb,
      head_g.reshape(1, GDN_HEAD_DIM), w_out.astype(BF16))


def _top_values(x, n, with_rank=False):
    vals = []
    rank = jnp.full(x.shape, float(n), F32) if with_rank else None
    for r in range(n):
        m = jnp.max(x, axis=0, keepdims=True)
        vals.append(m)
        if with_rank:
            rank = jnp.where(x == m, float(r), rank)
        if r + 1 < n:
            x = jnp.where(x == m, -jnp.inf, x)
    return (vals, rank) if with_rank else vals


def _pair_pack(x):
    bits = lax.bitcast_convert_type(x.astype(BF16).astype(F32), jnp.int32)
    return bits | lax.shift_right_logical(bits, jnp.int32(16))


def _stack_rows(rows):
    l = rows[0].shape[1]
    sub = lax.broadcasted_iota(jnp.int32, (SUBLANE, l), 0)
    out = jnp.broadcast_to(rows[0], (SUBLANE, l))
    for i in range(1, len(rows)):
        out = jnp.where(sub == i, rows[i], out)
    return out


def _peer_kernel(x_ref, sh_ref, sc_ref, gt_ref, ng_ref, fg_ref, wq_ref, k1_ref, k2_ref, u_ref, vt_ref, o_ref,
                 tt_ref, s1_ref, s2_ref, r1_ref, w1_ref, jb_ref, e2_ref, yt_ref, *half_refs, tt, eb, ne, final):
    g = pl.program_id(2)
    nh = tt // PEER_NCOLS
    ht_refs = half_refs[:nh]
    at_refs = half_refs[nh:]
    nj = tt // LANE
    ab = eb // PEER_NKEYS
    k = PEER_TOPK

    @pl.when(g == 0)
    def _():
        x = x_ref[0]
        t = _norm_mod(x, ng_ref[...], sc_ref[0], sh_ref[0])
        tb = t.astype(BF16)
        tt_ref[...] = t.T.astype(BF16)
        q = jnp.dot(tb, wq_ref[...], preferred_element_type=F32)
        q_hi, q_lo = _split_bf16(q)
        k_parts = [_split_bf16(k1_ref[...]), _split_bf16(k2_ref[...])]
        for h in range(PEER_HEADS):
            for side, s_ref in enumerate((s1_ref, s2_ref)):
                qs = slice((2 * h + side) * PEER_HALF, (2 * h + side + 1) * PEER_HALF)
                k_hi, k_lo = k_parts[side]
                s_ref[h] = (lax.dot_general(k_hi, q_hi[:, qs], NT, preferred_element_type=F32)
                            + lax.dot_general(k_hi, q_lo[:, qs], NT, preferred_element_type=F32)
                            + lax.dot_general(k_lo, q_hi[:, qs], NT, preferred_element_type=F32))
        yt_ref[...] = jnp.zeros_like(yt_ref)

        def route_one(idx):
            h = idx // nj
            c0 = pl.multiple_of((idx % nj) * LANE, LANE)
            cols = pl.ds(c0, LANE)
            x1 = s1_ref[h, :, cols]
            x2 = s2_ref[h, :, cols]
            v1, rank1 = _top_values(x1, k + 1, with_rank=True)
            v2 = _top_values(x2, k + 1)
            v2lo = _stack_rows(v2[:SUBLANE])
            v2hi = _stack_rows(v2[SUBLANE:2 * SUBLANE])
            v1hi = _stack_rows(v1[SUBLANE:2 * SUBLANE])
            sub = lax.broadcasted_iota(jnp.int32, (SUBLANE, LANE), 0)
            last = jnp.where(sub == 0, v1[0] + v2[k], jnp.where(sub == 1, v1[k] + v2[0], -jnp.inf))
            tiles = [v1[0] + v2lo, v1[0] + v2hi, v1hi + v2[0], last]
            tiles += [v1[i] + v2lo for i in range(1, SUBLANE)]
            cand = jnp.concatenate(tiles, axis=0)
            tops = _top_values(cand, k + 1)
            kth = tops[k - 1]
            cut = 0.5 * (kth + tops[k])
            top = v1[0] + v2[0]
            z = jnp.sum(jnp.where(cand >= kth, jnp.exp(cand - top), 0.0), axis=0, keepdims=True)
            c2 = cut - x2
            passing = jnp.zeros_like(c2)
            for i in range(k + 1):
                passing = jnp.where(v1[i] >= c2, float(i + 1), passing)
            r1_ref[h, :, cols] = _pair_pack(rank1)
            w1_ref[h, :, cols] = _pair_pack(jnp.exp(x1 - v1[0]) / z)
            jb_ref[h, :, cols] = passing.astype(BF16)
            e2_ref[h, :, cols] = jnp.exp(x2 - v2[0]).astype(BF16)

        def route(i, carry):
            for r in range(PEER_ROUTE_UNROLL):
                route_one(PEER_ROUTE_UNROLL * i + r)
            return carry

        lax.fori_loop(0, PEER_HEADS * nj // PEER_ROUTE_UNROLL, route, 0)

    em = jnp.clip(g - 1, 0, ne - 1)
    arows = pl.ds(pl.multiple_of(em * ab, SUBLANE), SUBLANE)
    bh = PEER_NKEYS // 2
    d = yt_ref.shape[0]
    ngroup = PEER_NCOLS // LANE

    def gate_block(n, jj, ag, hb):
        cols = slice(n * PEER_NCOLS + jj * LANE, n * PEER_NCOLS + (jj + 1) * LANE)
        lcols = slice(jj * LANE, (jj + 1) * LANE)
        brows = slice(hb * bh, (hb + 1) * bh)
        als = tuple(range(ag * PEER_AROWS, (ag + 1) * PEER_AROWS))
        r1t = [r1_ref[h, arows, cols] for h in range(PEER_HEADS)]
        w1t = [w1_ref[h, arows, cols] for h in range(PEER_HEADS)]

        def row(tile, al):
            return pltpu.bitcast(jnp.broadcast_to(tile[al:al + 1, :], (SUBLANE, LANE)), BF16)

        nv = bh // PACK
        accs = [[None] * nv for _ in als]
        for h in range(PEER_HEADS):
            jb = jb_ref[h, brows, cols]
            e2 = e2_ref[h, brows, cols]
            r1rows = [row(r1t[h], al) for al in als]
            w1rows = [row(w1t[h], al) for al in als]
            for v in range(nv):
                vr = slice(v * PACK, (v + 1) * PACK)
                jbv = jb[vr]
                e2v = e2[vr]
                for i in range(len(als)):
                    term = jnp.where(r1rows[i] < jbv, e2v * w1rows[i], jnp.zeros_like(e2v))
                    accs[i][v] = term if h == 0 else accs[i][v] + term
        fold = None
        for i, al in enumerate(als):
            for v in range(nv):
                r0 = al * PEER_NKEYS + hb * bh + v * PACK
                out = accs[i][v] * _gelu_tanh(ht_refs[n][r0:r0 + PACK, lcols].astype(BF16))
                at_refs[n][r0:r0 + PACK, lcols] = out
                bits = pltpu.bitcast(out, jnp.int32)
                fold = bits if fold is None else fold | bits
        return fold & jnp.minimum(g, 0)

    def after(zero, rhs):
        head = pltpu.bitcast(rhs[:PEER_KTILE], jnp.int32)
        head = head | jnp.tile(zero, (head.shape[0] // SUBLANE, head.shape[1] // LANE))
        return jnp.concatenate([pltpu.bitcast(head, BF16), rhs[PEER_KTILE:]], axis=0)

    def matmul_pieces(n, second_mm, first_mm):
        ncols = slice(n * PEER_NCOLS, (n + 1) * PEER_NCOLS)
        pieces = []
        for m in range(d // PEER_MROWS):
            mrows = slice(m * PEER_MROWS, (m + 1) * PEER_MROWS)

            def second(zero, mrows=mrows):
                yt_ref[mrows, ncols] += jnp.dot(vt_ref[0, mrows, :], after(zero, at_refs[n][...]),
                                                preferred_element_type=F32)

            def first(zero, mrows=mrows):
                ht_refs[n][mrows, :] = jnp.dot(u_ref[mrows, :], after(zero, tt_ref[:, ncols]),
                                               preferred_element_type=F32)
            pieces += ([second] if second_mm else []) + ([first] if first_mm else [])
        return pieces

    def run_step(gate, second_mm, first_mm):
        pending = []
        zero = lax.broadcasted_iota(jnp.int32, (SUBLANE, LANE), 0) & jnp.minimum(g, 0)
        for n in range(tt // PEER_NCOLS):
            blk = 0
            if gate:
                for jj in range(ngroup):
                    for ag in range(ab // PEER_AROWS):
                        for hb in range(2):
                            zero = gate_block(n, jj, ag, hb)
                            if pending and blk % PEER_GATE_PER_PIECE == 1:
                                pending.pop(0)(zero)
                            blk += 1
            for piece in pending:
                piece(zero)
            pending = matmul_pieces(n, second_mm, first_mm)
        for piece in pending:
            piece(zero)

    @pl.when(g == 0)
    def _():
        run_step(False, False, True)

    @pl.when(jnp.logical_and(g > 0, g < ne))
    def _():
        run_step(True, True, True)

    @pl.when(g == ne)
    def _():
        run_step(True, True, False)
        o = x_ref[0] + gt_ref[0] * yt_ref[...].T
        if final:
            o = o * lax.rsqrt(jnp.mean(o * o, axis=-1, keepdims=True) + RMS_EPS) * fg_ref[...]
        o_ref[0] = o


def _peer_call(x, sh, sc, gt, ng, fg, wq, k1, k2, u, v, *, tt, eb, final):
    b, s, d = x.shape
    assert eb == SUBLANE * PEER_NKEYS and u.shape[0] % eb == 0
    ne = u.shape[0] // eb
    ub = u.astype(BF16)
    vtb = v.astype(BF16).reshape(ne, eb, d).transpose(0, 2, 1)
    const3 = lambda bi, i, e: (0, 0)
    vec = pl.BlockSpec((1, 1, d), lambda bi, i, e: (bi, 0, 0))
    kern = functools.partial(_peer_kernel, tt=tt, eb=eb, ne=ne, final=final)
    sshape = (PEER_HEADS, PEER_NKEYS, tt)
    return pl.pallas_call(
        kern,
        grid=(b, s // tt, ne + 1),
        in_specs=[
            pl.BlockSpec((1, tt, d), lambda bi, i, e: (bi, i, 0)),
            vec, vec, vec,
            pl.BlockSpec((1, d), const3),
            pl.BlockSpec((1, d), const3),
            pl.BlockSpec(wq.shape, const3),
            pl.BlockSpec(k1.shape, const3),
            pl.BlockSpec(k2.shape, const3),
            pl.BlockSpec((eb, d), lambda bi, i, g: (jnp.minimum(g, ne - 1), 0)),
            pl.BlockSpec((1, d, eb), lambda bi, i, g: (jnp.maximum(g - 1, 0), 0, 0)),
        ],
        out_specs=pl.BlockSpec((1, tt, d), lambda bi, i, e: (bi, i, 0)),
        out_shape=jax.ShapeDtypeStruct(x.shape, F32),
        scratch_shapes=[
            pltpu.VMEM((d, tt), BF16),
            pltpu.VMEM(sshape, F32),
            pltpu.VMEM(sshape, F32),
            pltpu.VMEM(sshape, jnp.int32),
            pltpu.VMEM(sshape, jnp.int32),
            pltpu.VMEM(sshape, BF16),
            pltpu.VMEM(sshape, BF16),
            pltpu.VMEM((d, tt), F32),
        ] + [pltpu.VMEM((eb, PEER_NCOLS), F32)] * (tt // PEER_NCOLS)
          + [pltpu.VMEM((eb, PEER_NCOLS), BF16)] * (tt // PEER_NCOLS),
        compiler_params=pltpu.CompilerParams(
            dimension_semantics=("arbitrary", "arbitrary", "arbitrary"), vmem_limit_bytes=VMEM_LIMIT),
        name="peer_layer",
    )(x, sh, sc, gt, ng.reshape(1, d), fg.reshape(1, d), wq.astype(BF16), k1, k2, ub, vtb)


def _pick(n, pref):
    return pref if n % pref == 0 else n


def kernel(x, c, ada_w, ada_b, norm_mix_g, norm_ffn_g, gla_w_in, gla_w_g2, gla_b_g2, gla_norm_g,
           gla_w_out, gdn_w_in, gdn_conv_w, gdn_a_log, gdn_dt_bias, gdn_norm_g, gdn_w_out,
           peer_wq, peer_k1, peer_k2, peer_u, peer_v, final_norm_g):
    b, s, d = x.shape
    depth = ada_w.shape[0]
    c_pad = jnp.pad(c, ((0, SUBLANE - b % SUBLANE if b % SUBLANE else 0), (0, 0)))
    mod = _mod_call(c_pad, ada_w, ada_b)[:, :b, :].reshape(depth, b, 6, 1, d)
    tb_gla = _pick(s, 512)
    tb_mix = _pick(s, 256)
    tt = _pick(s, 512)
    eb = 1024
    for i in range(depth):
        sh1, sc1, gt1, sh2, sc2, gt2 = (mod[i, :, j] for j in range(6))
        j = i // 2
        if i % 2 == 0:
            x = _gla_call(x, sh1, sc1, gt1, norm_mix_g[i], gla_w_in[j], gla_w_g2[j], gla_b_g2[j],
                          gla_norm_g[j], gla_w_out[j], tb=tb_gla)
        else:
            x = _gdn_call(x, sh1, sc1, gt1, norm_mix_g[i], gdn_w_in[j], gdn_conv_w[j], gdn_a_log[j],
                          gdn_dt_bias[j], gdn_norm_g[j], gdn_w_out[j], tb=tb_mix)
        x = _peer_call(x, sh2, sc2, gt2, norm_ffn_g[i], final_norm_g, peer_wq[i], peer_k1[i], peer_k2[i],
                       peer_u[i], peer_v[i], tt=tt, eb=eb, final=(i == depth - 1))
    return x
```

```python
import functools

import jax
import jax.numpy as jnp
from jax import lax
from jax.experimental import pallas as pl
from jax.experimental.pallas import tpu as pltpu

F32 = jnp.float32
BF16 = jnp.bfloat16
HI = lax.Precision.HIGHEST

RMS_EPS = 1e-6
CHUNK = 64

GLA_HEADS = 4
GLA_GATE_RANK = 16
GLA_TAU = 16.0

GDN_QK_HEADS = 8
GDN_V_HEADS = 16
GDN_HEAD_DIM = 128
GDN_CONV = 4

PEER_HEADS = 8
PEER_TOPK = 16
PEER_NKEYS = 128
PEER_HALF = 128
PEER_NCOLS = 256
PEER_KTILE = 256
PEER_MROWS = 1024
PEER_AROWS = 4
PEER_GATE_PER_PIECE = 4
PEER_ROUTE_UNROLL = 4

LANE = 128
SUBLANE = 8
PACK = 16
VMEM_LIMIT = 56 * 1024 * 1024

NT = (((1,), (1,)), ((), ()))


def _sigmoid(x):
    return 1.0 / (1.0 + jnp.exp(-x))


def _silu(x):
    return x * _sigmoid(x)


def _softplus(x):
    return jnp.maximum(x, 0.0) + jnp.log1p(jnp.exp(-jnp.abs(x)))


def _gelu_tanh(x):
    c = 0.7978845608028654
    hx = 0.5 * x
    return hx + hx * jnp.tanh(x * (c + (c * 0.044715) * (x * x)))


def _norm_mod(x, g, sc, sh):
    ms = jnp.mean(x * x, axis=-1, keepdims=True)
    return (x * lax.rsqrt(ms + RMS_EPS) * g) * (1.0 + sc) + sh


def _bdot(a, b):
    return jnp.dot(a.astype(BF16), b.astype(BF16), preferred_element_type=F32)


def _bdot_nt(a, b):
    return lax.dot_general(a.astype(BF16), b.astype(BF16), NT, preferred_element_type=F32)


def _hdot(a, b):
    return jnp.dot(a, b, precision=HI, preferred_element_type=F32)


def _split_bf16(x):
    hi = x.astype(BF16)
    return hi, (x - hi.astype(F32)).astype(BF16)


def _tri_cumsum(tril_b, x):
    hi = x.astype(BF16)
    rest = x - hi.astype(F32)
    mid = rest.astype(BF16)
    lo = (rest - mid.astype(F32)).astype(BF16)
    return (jnp.dot(tril_b, hi, preferred_element_type=F32) + jnp.dot(tril_b, mid, preferred_element_type=F32)
            + jnp.dot(tril_b, lo, preferred_element_type=F32))


def _tril(n, k=0):
    r = lax.broadcasted_iota(jnp.int32, (n, n), 0)
    c = lax.broadcasted_iota(jnp.int32, (n, n), 1)
    return (c - r) <= k


def _mod_kernel(c_ref, w_ref, b_ref, o_ref):
    c = c_ref[...]
    o_ref[0] = _hdot(_silu(c), w_ref[0]) + b_ref[0]


def _mod_call(c_pad, ada_w, ada_b):
    depth, d, d6 = ada_w.shape
    nb = d6 // d
    return pl.pallas_call(
        _mod_kernel,
        grid=(depth, nb),
        in_specs=[
            pl.BlockSpec(c_pad.shape, lambda i, j: (0, 0)),
            pl.BlockSpec((1, d, d), lambda i, j: (i, 0, j)),
            pl.BlockSpec((1, 1, d), lambda i, j: (i, 0, j)),
        ],
        out_specs=pl.BlockSpec((1, c_pad.shape[0], d), lambda i, j: (i, 0, j)),
        out_shape=jax.ShapeDtypeStruct((depth, c_pad.shape[0], d6), F32),
        name="adaln_mod",
    )(c_pad, ada_w, ada_b.reshape(depth, 1, d6))


def _gla_kernel(x_ref, sh_ref, sc_ref, gt_ref, ng_ref, wqkvr_ref, wg1_ref, wg2_ref, bg2_ref,
                hg_ref, wout_ref, o_ref, st_ref, p_ref, la_ref, ob_ref, *, tb, dk, dv):
    hk = dk // GLA_HEADS
    hv = dv // GLA_HEADS
    nc = tb // CHUNK

    @pl.when(pl.program_id(1) == 0)
    def _():
        st_ref[...] = jnp.zeros_like(st_ref)

    x = x_ref[0]
    h = _norm_mod(x, ng_ref[...], sc_ref[0], sh_ref[0])
    hb = h.astype(BF16)
    p_ref[...] = jnp.dot(hb, wqkvr_ref[...], preferred_element_type=F32)
    g1 = jnp.dot(hb, wg1_ref[...], preferred_element_type=F32)
    g = jnp.dot(g1.astype(BF16), wg2_ref[...], preferred_element_type=F32) + bg2_ref[...]
    la_ref[...] = (jnp.minimum(g, 0.0) - jnp.log1p(jnp.exp(-jnp.abs(g)))) * (1.0 / GLA_TAU)

    tril = _tril(CHUNK).astype(BF16)
    causal = _tril(CHUNK)
    scale = hk ** -0.5

    for c in range(nc):
        rows = slice(c * CHUNK, (c + 1) * CHUNK)
        cum = _tri_cumsum(tril, la_ref[rows, :])
        cl = cum[CHUNK - 1:CHUNK, :]
        e_pos = jnp.exp(cum)
        e_neg = jnp.exp(-cum)
        e_end = jnp.exp(cl - cum)
        dec = jnp.exp(cl)
        hr = range(GLA_HEADS)
        ksl = [slice(hd * hk, (hd + 1) * hk) for hd in hr]
        qis = [p_ref[rows, hd * hk:(hd + 1) * hk] * (scale * e_pos[:, ksl[hd]]) for hd in hr]
        kraw = [p_ref[rows, dk + hd * hk:dk + (hd + 1) * hk] for hd in hr]
        vs = [p_ref[rows, 2 * dk + hd * hv:2 * dk + (hd + 1) * hv] for hd in hr]
        attns = [jnp.where(causal, _bdot_nt(qis[hd], kraw[hd] * e_neg[:, ksl[hd]]), 0.0) for hd in hr]
        sts = [st_ref[hd] for hd in hr]
        inter = [_bdot_nt(qis[hd], sts[hd]) for hd in hr]
        upds = [_bdot(vs[hd].T, kraw[hd] * e_end[:, ksl[hd]]) for hd in hr]
        intra = [_bdot(attns[hd], vs[hd]) for hd in hr]
        for hd in hr:
            ob_ref[rows, hd * hv:(hd + 1) * hv] = intra[hd] + inter[hd]
            st_ref[hd] = dec[:, ksl[hd]] * sts[hd] + upds[hd]

    hg = hg_ref[...]
    parts = []
    for hd in range(GLA_HEADS):
        oh = ob_ref[:, hd * hv:(hd + 1) * hv]
        ms = jnp.mean(oh * oh, axis=-1, keepdims=True)
        r = p_ref[:, 2 * dk + dv + hd * hv:2 * dk + dv + (hd + 1) * hv]
        parts.append((oh * lax.rsqrt(ms + RMS_EPS) * hg * _silu(r)).astype(BF16))
    y = jnp.concatenate(parts, axis=-1)
    m = jnp.dot(y, wout_ref[...], preferred_element_type=F32)
    o_ref[0] = x + gt_ref[0] * m


def _gla_call(x, sh, sc, gt, ng, w_in, w_g2, b_g2, head_g, w_out, *, tb):
    b, s, d = x.shape
    dk = w_g2.shape[1]
    dv = w_out.shape[0]
    wqkvr = w_in[:, :2 * dk + 2 * dv].astype(BF16)
    wg1 = jnp.pad(w_in[:, 2 * dk + 2 * dv:], ((0, 0), (0, LANE - GLA_GATE_RANK))).astype(BF16)
    wg2 = jnp.pad(w_g2, ((0, LANE - GLA_GATE_RANK), (0, 0))).astype(BF16)
    hv = dv // GLA_HEADS
    hk = dk // GLA_HEADS
    const2 = lambda bi, i: (0, 0)
    vec = pl.BlockSpec((1, 1, d), lambda bi, i: (bi, 0, 0))
    kern = functools.partial(_gla_kernel, tb=tb, dk=dk, dv=dv)
    return pl.pallas_call(
        kern,
        grid=(b, s // tb),
        in_specs=[
            pl.BlockSpec((1, tb, d), lambda bi, i: (bi, i, 0)),
            vec, vec, vec,
            pl.BlockSpec((1, d), const2),
            pl.BlockSpec(wqkvr.shape, const2),
            pl.BlockSpec(wg1.shape, const2),
            pl.BlockSpec(wg2.shape, const2),
            pl.BlockSpec((1, dk), const2),
            pl.BlockSpec((1, hv), const2),
            pl.BlockSpec((dv, d), const2),
        ],
        out_specs=pl.BlockSpec((1, tb, d), lambda bi, i: (bi, i, 0)),
        out_shape=jax.ShapeDtypeStruct(x.shape, F32),
        scratch_shapes=[
            pltpu.VMEM((GLA_HEADS, hv, hk), F32),
            pltpu.VMEM((tb, 2 * dk + 2 * dv), F32),
            pltpu.VMEM((tb, dk), F32),
            pltpu.VMEM((tb, dv), F32),
        ],
        compiler_params=pltpu.CompilerParams(
            dimension_semantics=("arbitrary", "arbitrary"), vmem_limit_bytes=VMEM_LIMIT),
        name="gla_layer",
    )(x, sh, sc, gt, ng.reshape(1, d), wqkvr, wg1, wg2, b_g2.reshape(1, dk),
      head_g.reshape(1, hv), w_out.astype(BF16))


def _block_inverse_masks(n):
    r = lax.broadcasted_iota(jnp.int32, (n, n), 0)
    c = lax.broadcasted_iota(jnp.int32, (n, n), 1)
    eye = (r == c).astype(F32)
    first = (r // 2) == (c // 2)
    quads = []
    s = 2
    while s < n:
        quads.append(((r // (2 * s)) == (c // (2 * s))) & ((r // s) != (c // s)))
        s *= 2
    return eye, first, quads


def _unit_lower_inverse(a_list, masks):
    eye, first, quads = masks
    ds = [eye - jnp.where(first, a, 0.0) for a in a_list]
    for quad in quads:
        ms = [_bdot(d, jnp.where(quad, a, 0.0)) for d, a in zip(ds, a_list)]
        ds = [d - _bdot(m, d) for d, m in zip(ds, ms)]
    return ds


def _gdn_kernel(x_ref, sh_ref, sc_ref, gt_ref, ng_ref, wqkv_ref, wz_ref, wa_ref, wb_ref, cw_ref,
                alog_ref, dtb_ref, hg_ref, wout_ref, o_ref,
                s_ref, cb_ref, q_ref, k_ref, v_ref, z_ref, g_ref, bt_ref, ob_ref, *, tb):
    dqk = GDN_QK_HEADS * GDN_HEAD_DIM
    dvv = GDN_V_HEADS * GDN_HEAD_DIM
    hd_ = GDN_HEAD_DIM
    nc = tb // CHUNK
    rep = GDN_V_HEADS // GDN_QK_HEADS

    @pl.when(pl.program_id(1) == 0)
    def _():
        s_ref[...] = jnp.zeros_like(s_ref)
        cb_ref[0:SUBLANE, :] = jnp.zeros((SUBLANE, cb_ref.shape[1]), F32)

    x = x_ref[0]
    h = _norm_mod(x, ng_ref[...], sc_ref[0], sh_ref[0])
    hb = h.astype(BF16)
    cb_ref[SUBLANE:SUBLANE + tb, :] = jnp.dot(hb, wqkv_ref[...], preferred_element_type=F32)
    z_ref[...] = jnp.dot(hb, wz_ref[...], preferred_element_type=F32)
    a = jnp.dot(hb, wa_ref[...], preferred_element_type=F32)
    bb = jnp.dot(hb, wb_ref[...], preferred_element_type=F32)
    g_ref[...] = -jnp.exp(alog_ref[...]) * _softplus(a + dtb_ref[...])
    bt_ref[...] = _sigmoid(bb)

    off = SUBLANE - (GDN_CONV - 1)
    conv = cw_ref[0:1, :] * cb_ref[off:off + tb, :]
    for j in range(1, GDN_CONV):
        conv = conv + cw_ref[j:j + 1, :] * cb_ref[off + j:off + j + tb, :]
    cb_ref[0:SUBLANE, :] = cb_ref[tb:tb + SUBLANE, :]
    qkv = _silu(conv)
    qscale = hd_ ** -0.5
    for hq in range(GDN_QK_HEADS):
        sl = slice(hq * hd_, (hq + 1) * hd_)
        qh = qkv[:, hq * hd_:(hq + 1) * hd_]
        kh = qkv[:, dqk + hq * hd_:dqk + (hq + 1) * hd_]
        q_ref[:, sl] = qh * (lax.rsqrt(jnp.sum(qh * qh, axis=-1, keepdims=True) + 1e-6) * qscale)
        k_ref[:, sl] = kh * lax.rsqrt(jnp.sum(kh * kh, axis=-1, keepdims=True) + 1e-6)
    v_ref[...] = qkv[:, 2 * dqk:]

    tril = _tril(CHUNK).astype(BF16)
    lower = _tril(CHUNK)
    strict = _tril(CHUNK, -1)
    inv_masks = _block_inverse_masks(CHUNK)

    for c in range(nc):
        rows = slice(c * CHUNK, (c + 1) * CHUNK)
        gc = _tri_cumsum(tril, g_ref[rows, :])
        gct = gc.T
        gl = gc[CHUNK - 1:CHUNK, :]
        eg = jnp.exp(gc)
        eendt = jnp.exp(gl - gc).T
        dec = jnp.exp(gl)
        bt = bt_ref[rows, :]
        beg = bt * eg
        hq_r = range(GDN_QK_HEADS)
        hv_r = range(GDN_V_HEADS)
        qs = [q_ref[rows, hq * hd_:(hq + 1) * hd_] for hq in hq_r]
        ks = [k_ref[rows, hq * hd_:(hq + 1) * hd_] for hq in hq_r]
        kks = [_bdot_nt(ks[hq], ks[hq]) for hq in hq_r]
        qk0s = [_bdot_nt(qs[hq], ks[hq]) for hq in hq_r]
        khts = [ks[hq].T for hq in hq_r]
        cols = [slice(hvi, hvi + 1) for hvi in hv_r]
        decays = []
        for hvi in hv_r:
            diff = gc[:, cols[hvi]] - gct[hvi:hvi + 1, :]
            decays.append(jnp.where(lower, jnp.exp(jnp.where(lower, diff, 0.0)), 0.0))
        amats = [jnp.where(strict, bt[:, cols[hvi]] * kks[hvi // rep] * decays[hvi], 0.0) for hvi in hv_r]
        tinvs = _unit_lower_inverse(amats, inv_masks)
        rhs = [jnp.concatenate([bt[:, cols[hvi]] * v_ref[rows, hvi * hd_:(hvi + 1) * hd_],
                                beg[:, cols[hvi]] * ks[hvi // rep]], axis=1) for hvi in hv_r]
        uws = [_bdot(tinvs[hvi], rhs[hvi]) for hvi in hv_r]
        ss = [s_ref[hvi] for hvi in hv_r]
        wss = [_bdot(uws[hvi][:, hd_:], ss[hvi]) for hvi in hv_r]
        qss = [_bdot(qs[hvi // rep] * eg[:, cols[hvi]], ss[hvi]) for hvi in hv_r]
        vns = [uws[hvi][:, :hd_] - wss[hvi] for hvi in hv_r]
        outs = [qss[hvi] + _bdot(qk0s[hvi // rep] * decays[hvi], vns[hvi]) for hvi in hv_r]
        upds = [_bdot(khts[hvi // rep] * eendt[hvi:hvi + 1, :], vns[hvi]) for hvi in hv_r]
        for hvi in hv_r:
            ob_ref[rows, hvi * hd_:(hvi + 1) * hd_] = outs[hvi]
            s_ref[hvi] = dec[:, cols[hvi]] * ss[hvi] + upds[hvi]

    hg = hg_ref[...]
    parts = []
    for hvi in range(GDN_V_HEADS):
        vs = slice(hvi * hd_, (hvi + 1) * hd_)
        oh = ob_ref[:, vs]
        ms = jnp.mean(oh * oh, axis=-1, keepdims=True)
        parts.append((oh * lax.rsqrt(ms + RMS_EPS) * hg * _silu(z_ref[:, vs])).astype(BF16))
    y = jnp.concatenate(parts, axis=-1)
    m = jnp.dot(y, wout_ref[...], preferred_element_type=F32)
    o_ref[0] = x + gt_ref[0] * m


def _gdn_call(x, sh, sc, gt, ng, w_in, conv_w, a_log, dt_bias, head_g, w_out, *, tb):
    b, s, d = x.shape
    dqk = GDN_QK_HEADS * GDN_HEAD_DIM
    dvv = GDN_V_HEADS * GDN_HEAD_DIM
    nqkv = 2 * dqk + dvv
    wqkv = w_in[:, :nqkv].astype(BF16)
    wz = w_in[:, nqkv:nqkv + dvv].astype(BF16)
    padh = ((0, 0), (0, LANE - GDN_V_HEADS))
    wa = jnp.pad(w_in[:, nqkv + dvv:nqkv + dvv + GDN_V_HEADS], padh).astype(BF16)
    wb = jnp.pad(w_in[:, nqkv + dvv + GDN_V_HEADS:], padh).astype(BF16)
    alog = jnp.pad(a_log.reshape(1, -1), padh)
    dtb = jnp.pad(dt_bias.reshape(1, -1), padh)
    const2 = lambda bi, i: (0, 0)
    vec = pl.BlockSpec((1, 1, d), lambda bi, i: (bi, 0, 0))
    kern = functools.partial(_gdn_kernel, tb=tb)
    return pl.pallas_call(
        kern,
        grid=(b, s // tb),
        in_specs=[
            pl.BlockSpec((1, tb, d), lambda bi, i: (bi, i, 0)),
            vec, vec, vec,
            pl.BlockSpec((1, d), const2),
            pl.BlockSpec(wqkv.shape, const2),
            pl.BlockSpec(wz.shape, const2),
            pl.BlockSpec(wa.shape, const2),
            pl.BlockSpec(wb.shape, const2),
            pl.BlockSpec(conv_w.shape, const2),
            pl.BlockSpec((1, LANE), const2),
            pl.BlockSpec((1, LANE), const2),
            pl.BlockSpec((1, GDN_HEAD_DIM), const2),
            pl.BlockSpec((dvv, d), const2),
        ],
        out_specs=pl.BlockSpec((1, tb, d), lambda bi, i: (bi, i, 0)),
        out_shape=jax.ShapeDtypeStruct(x.shape, F32),
        scratch_shapes=[
            pltpu.VMEM((GDN_V_HEADS, GDN_HEAD_DIM, GDN_HEAD_DIM), F32),
            pltpu.VMEM((tb + SUBLANE, nqkv), F32),
            pltpu.VMEM((tb, dqk), F32),
            pltpu.VMEM((tb, dqk), F32),
            pltpu.VMEM((tb, dvv), F32),
            pltpu.VMEM((tb, dvv), F32),
            pltpu.VMEM((tb, LANE), F32),
            pltpu.VMEM((tb, LANE), F32),
            pltpu.VMEM((tb, dvv), F32),
        ],
        compiler_params=pltpu.CompilerParams(
            dimension_semantics=("arbitrary", "arbitrary"), vmem_limit_bytes=VMEM_LIMIT),
        name="gdn_layer",
    )(x, sh, sc, gt, ng.reshape(1, d), wqkv, wz, wa, wb, conv_w, alog, dtb,
      head_g.reshape(1, GDN_HEAD_DIM), w_out.astype(BF16))


def _top_values(x, n, with_rank=False):
    vals = []
    rank = jnp.full(x.shape, float(n), F32) if with_rank else None
    for r in range(n):
        m = jnp.max(x, axis=0, keepdims=True)
        vals.append(m)
        if with_rank:
            rank = jnp.where(x == m, float(r), rank)
        if r + 1 < n:
            x = jnp.where(x == m, -jnp.inf, x)
    return (vals, rank) if with_rank else vals


def _pair_pack(x):
    bits = lax.bitcast_convert_type(x.astype(BF16).astype(F32), jnp.int32)
    return bits | lax.shift_right_logical(bits, jnp.int32(16))


def _stack_rows(rows):
    l = rows[0].shape[1]
    sub = lax.broadcasted_iota(jnp.int32, (SUBLANE, l), 0)
    out = jnp.broadcast_to(rows[0], (SUBLANE, l))
    for i in range(1, len(rows)):
        out = jnp.where(sub == i, rows[i], out)
    return out


def _peer_kernel(x_ref, sh_ref, sc_ref, gt_ref, ng_ref, fg_ref, wq_ref, k1_ref, k2_ref, u_ref, vt_ref, o_ref,
                 tt_ref, s1_ref, s2_ref, r1_ref, w1_ref, jb_ref, e2_ref, yt_ref, *half_refs, tt, eb, ne, final):
    g = pl.program_id(2)
    nh = tt // PEER_NCOLS
    ht_refs = half_refs[:nh]
    at_refs = half_refs[nh:]
    nj = tt // LANE
    ab = eb // PEER_NKEYS
    k = PEER_TOPK

    @pl.when(g == 0)
    def _():
        x = x_ref[0]
        t = _norm_mod(x, ng_ref[...], sc_ref[0], sh_ref[0])
        tb = t.astype(BF16)
        tt_ref[...] = t.T.astype(BF16)
        q = jnp.dot(tb, wq_ref[...], preferred_element_type=F32)
        q_hi, q_lo = _split_bf16(q)
        k_parts = [_split_bf16(k1_ref[...]), _split_bf16(k2_ref[...])]
        for h in range(PEER_HEADS):
            for side, s_ref in enumerate((s1_ref, s2_ref)):
                qs = slice((2 * h + side) * PEER_HALF, (2 * h + side + 1) * PEER_HALF)
                k_hi, k_lo = k_parts[side]
                s_ref[h] = (lax.dot_general(k_hi, q_hi[:, qs], NT, preferred_element_type=F32)
                            + lax.dot_general(k_hi, q_lo[:, qs], NT, preferred_element_type=F32)
                            + lax.dot_general(k_lo, q_hi[:, qs], NT, preferred_element_type=F32))
        yt_ref[...] = jnp.zeros_like(yt_ref)

        def route_one(idx):
            h = idx // nj
            c0 = pl.multiple_of((idx % nj) * LANE, LANE)
            cols = pl.ds(c0, LANE)
            x1 = s1_ref[h, :, cols]
            x2 = s2_ref[h, :, cols]
            v1, rank1 = _top_values(x1, k + 1, with_rank=True)
            v2 = _top_values(x2, k + 1)
            v2lo = _stack_rows(v2[:SUBLANE])
            v2hi = _stack_rows(v2[SUBLANE:2 * SUBLANE])
            v1hi = _stack_rows(v1[SUBLANE:2 * SUBLANE])
            sub = lax.broadcasted_iota(jnp.int32, (SUBLANE, LANE), 0)
            last = jnp.where(sub == 0, v1[0] + v2[k], jnp.where(sub == 1, v1[k] + v2[0], -jnp.inf))
            tiles = [v1[0] + v2lo, v1[0] + v2hi, v1hi + v2[0], last]
            tiles += [v1[i] + v2lo for i in range(1, SUBLANE)]
            cand = jnp.concatenate(tiles, axis=0)
            tops = _top_values(cand, k + 1)
            kth = tops[k - 1]
            cut = 0.5 * (kth + tops[k])
            top = v1[0] + v2[0]
            z = jnp.sum(jnp.where(cand >= kth, jnp.exp(cand - top), 0.0), axis=0, keepdims=True)
            c2 = cut - x2
            passing = jnp.zeros_like(c2)
            for i in range(k + 1):
                passing = jnp.where(v1[i] >= c2, float(i + 1), passing)
            r1_ref[h, :, cols] = _pair_pack(rank1)
            w1_ref[h, :, cols] = _pair_pack(jnp.exp(x1 - v1[0]) / z)
            jb_ref[h, :, cols] = passing.astype(BF16)
            e2_ref[h, :, cols] = jnp.exp(x2 - v2[0]).astype(BF16)

        def route(i, carry):
            for r in range(PEER_ROUTE_UNROLL):
                route_one(PEER_ROUTE_UNROLL * i + r)
            return carry

        lax.fori_loop(0, PEER_HEADS * nj // PEER_ROUTE_UNROLL, route, 0)

    em = jnp.clip(g - 1, 0, ne - 1)
    arows = pl.ds(pl.multiple_of(em * ab, SUBLANE), SUBLANE)
    bh = PEER_NKEYS // 2
    d = yt_ref.shape[0]
    ngroup = PEER_NCOLS // LANE

    def gate_block(n, jj, ag, hb):
        cols = slice(n * PEER_NCOLS + jj * LANE, n * PEER_NCOLS + (jj + 1) * LANE)
        lcols = slice(jj * LANE, (jj + 1) * LANE)
        brows = slice(hb * bh, (hb + 1) * bh)
        als = tuple(range(ag * PEER_AROWS, (ag + 1) * PEER_AROWS))
        r1t = [r1_ref[h, arows, cols] for h in range(PEER_HEADS)]
        w1t = [w1_ref[h, arows, cols] for h in range(PEER_HEADS)]

        def row(tile, al):
            return pltpu.bitcast(jnp.broadcast_to(tile[al:al + 1, :], (SUBLANE, LANE)), BF16)

        nv = bh // PACK
        accs = [[None] * nv for _ in als]
        for h in range(PEER_HEADS):
            jb = jb_ref[h, brows, cols]
            e2 = e2_ref[h, brows, cols]
            r1rows = [row(r1t[h], al) for al in als]
            w1rows = [row(w1t[h], al) for al in als]
            for v in range(nv):
                vr = slice(v * PACK, (v + 1) * PACK)
                jbv = jb[vr]
                e2v = e2[vr]
                for i in range(len(als)):
                    term = jnp.where(r1rows[i] < jbv, e2v * w1rows[i], jnp.zeros_like(e2v))
                    accs[i][v] = term if h == 0 else accs[i][v] + term
        fold = None
        for i, al in enumerate(als):
            for v in range(nv):
                r0 = al * PEER_NKEYS + hb * bh + v * PACK
                out = accs[i][v] * _gelu_tanh(ht_refs[n][r0:r0 + PACK, lcols].astype(BF16))
                at_refs[n][r0:r0 + PACK, lcols] = out
                bits = pltpu.bitcast(out, jnp.int32)
                fold = bits if fold is None else fold | bits
        return fold & jnp.minimum(g, 0)

    def after(zero, rhs):
        head = pltpu.bitcast(rhs[:PEER_KTILE], jnp.int32)
        head = head | jnp.tile(zero, (head.shape[0] // SUBLANE, head.shape[1] // LANE))
        return jnp.concatenate([pltpu.bitcast(head, BF16), rhs[PEER_KTILE:]], axis=0)

    def matmul_pieces(n, second_mm, first_mm):
        ncols = slice(n * PEER_NCOLS, (n + 1) * PEER_NCOLS)
        pieces = []
        for m in range(d // PEER_MROWS):
            mrows = slice(m * PEER_MROWS, (m + 1) * PEER_MROWS)

            def second(zero, mrows=mrows):
                yt_ref[mrows, ncols] += jnp.dot(vt_ref[0, mrows, :], after(zero, at_refs[n][...]),
                                                preferred_element_type=F32)

            def first(zero, mrows=mrows):
                ht_refs[n][mrows, :] = jnp.dot(u_ref[mrows, :], after(zero, tt_ref[:, ncols]),
                                               preferred_element_type=F32)
            pieces += ([second] if second_mm else []) + ([first] if first_mm else [])
        return pieces

    def run_step(gate, second_mm, first_mm):
        pending = []
        zero = lax.broadcasted_iota(jnp.int32, (SUBLANE, LANE), 0) & jnp.minimum(g, 0)
        for n in range(tt // PEER_NCOLS):
            blk = 0
            if gate:
                for jj in range(ngroup):
                    for ag in range(ab // PEER_AROWS):
                        for hb in range(2):
                            zero = gate_block(n, jj, ag, hb)
                            if pending and blk % PEER_GATE_PER_PIECE == 1:
                                pending.pop(0)(zero)
                            blk += 1
            for piece in pending:
                piece(zero)
            pending = matmul_pieces(n, second_mm, first_mm)
        for piece in pending:
            piece(zero)

    @pl.when(g == 0)
    def _():
        run_step(False, False, True)

    @pl.when(jnp.logical_and(g > 0, g < ne))
    def _():
        run_step(True, True, True)

    @pl.when(g == ne)
    def _():
        run_step(True, True, False)
        o = x_ref[0] + gt_ref[0] * yt_ref[...].T
        if final:
            o = o * lax.rsqrt(jnp.mean(o * o, axis=-1, keepdims=True) + RMS_EPS) * fg_ref[...]
        o_ref[0] = o


def _peer_call(x, sh, sc, gt, ng, fg, wq, k1, k2, u, v, *, tt, eb, final):
    b, s, d = x.shape
    assert eb == SUBLANE * PEER_NKEYS and u.shape[0] % eb == 0
    ne = u.shape[0] // eb
    ub = u.astype(BF16)
    vtb = v.astype(BF16).reshape(ne, eb, d).transpose(0, 2, 1)
    const3 = lambda bi, i, e: (0, 0)
    vec = pl.BlockSpec((1, 1, d), lambda bi, i, e: (bi, 0, 0))
    kern = functools.partial(_peer_kernel, tt=tt, eb=eb, ne=ne, final=final)
    sshape = (PEER_HEADS, PEER_NKEYS, tt)
    return pl.pallas_call(
        kern,
        grid=(b, s // tt, ne + 1),
        in_specs=[
            pl.BlockSpec((1, tt, d), lambda bi, i, e: (bi, i, 0)),
            vec, vec, vec,
            pl.BlockSpec((1, d), const3),
            pl.BlockSpec((1, d), const3),
            pl.BlockSpec(wq.shape, const3),
            pl.BlockSpec(k1.shape, const3),
            pl.BlockSpec(k2.shape, const3),
            pl.BlockSpec((eb, d), lambda bi, i, g: (jnp.minimum(g, ne - 1), 0)),
            pl.BlockSpec((1, d, eb), lambda bi, i, g: (jnp.maximum(g - 1, 0), 0, 0)),
        ],
        out_specs=pl.BlockSpec((1, tt, d), lambda bi, i, e: (bi, i, 0)),
        out_shape=jax.ShapeDtypeStruct(x.shape, F32),
        scratch_shapes=[
            pltpu.VMEM((d, tt), BF16),
            pltpu.VMEM(sshape, F32),
            pltpu.VMEM(sshape, F32),
            pltpu.VMEM(sshape, jnp.int32),
            pltpu.VMEM(sshape, jnp.int32),
            pltpu.VMEM(sshape, BF16),
            pltpu.VMEM(sshape, BF16),
            pltpu.VMEM((d, tt), F32),
        ] + [pltpu.VMEM((eb, PEER_NCOLS), F32)] * (tt // PEER_NCOLS)
          + [pltpu.VMEM((eb, PEER_NCOLS), BF16)] * (tt // PEER_NCOLS),
        compiler_params=pltpu.CompilerParams(
            dimension_semantics=("arbitrary", "arbitrary", "arbitrary"), vmem_limit_bytes=VMEM_LIMIT),
        name="peer_layer",
    )(x, sh, sc, gt, ng.reshape(1, d), fg.reshape(1, d), wq.astype(BF16), k1, k2, ub, vtb)


def _pick(n, pref):
    return pref if n % pref == 0 else n


def kernel(x, c, ada_w, ada_b, norm_mix_g, norm_ffn_g, gla_w_in, gla_w_g2, gla_b_g2, gla_norm_g,
           gla_w_out, gdn_w_in, gdn_conv_w, gdn_a_log, gdn_dt_bias, gdn_norm_g, gdn_w_out,
           peer_wq, peer_k1, peer_k2, peer_u, peer_v, final_norm_g):
    b, s, d = x.shape
    depth = ada_w.shape[0]
    c_pad = jnp.pad(c, ((0, SUBLANE - b % SUBLANE if b % SUBLANE else 0), (0, 0)))
    mod = _mod_call(c_pad, ada_w, ada_b)[:, :b, :].reshape(depth, b, 6, 1, d)
    tb_gla = _pick(s, 512)
    tb_mix = _pick(s, 256)
    tt = _pick(s, 512)
    eb = 1024
    for i in range(depth):
        sh1, sc1, gt1, sh2, sc2, gt2 = (mod[i, :, j] for j in range(6))
        j = i // 2
        if i % 2 == 0:
            x = _gla_call(x, sh1, sc1, gt1, norm_mix_g[i], gla_w_in[j], gla_w_g2[j], gla_b_g2[j],
                          gla_norm_g[j], gla_w_out[j], tb=tb_gla)
        else:
            x = _gdn_call(x, sh1, sc1, gt1, norm_mix_g[i], gdn_w_in[j], gdn_conv_w[j], gdn_a_log[j],
                          gdn_dt_bias[j], gdn_norm_g[j], gdn_w_out[j], tb=tb_mix)
        x = _peer_call(x, sh2, sc2, gt2, norm_ffn_g[i], final_norm_g, peer_wq[i], peer_k1[i], peer_k2[i],
                       peer_u[i], peer_v[i], tt=tt, eb=eb, final=(i == depth - 1))
    return x
```
